```python
import math
import jax, jax.numpy as jnp
from jax import lax
import numpy as np

D_MODEL = 1024
BATCH = 8
SEQ = 4096
DEPTH = 2

GDN_HEADS = 4
GDN_DK = 128
GDN_DV = 128
GDN_CONV = 4
GDN_CHUNK = 64
RWKV_HEADS = 8
RWKV_N = 64
RWKV_DECAY_LORA = 64
RWKV_AAA_LORA = 64
RWKV_GATE_LORA = 128
RWKV_LN_EPS = 64e-5
NSA_HEADS = 16
NSA_GROUPS = 2
NSA_HD = 64
CMP_LEN = 32
CMP_STRIDE = 16
CMP_HIDDEN = 256
SEL_BLOCK = 64
SEL_TOPN = 16
WINDOW = 512
NSA_QBLOCK = 64
ROPE_THETA = 500000.0
ROPE_DIM = NSA_HD // 4
D_FF = 4 * D_MODEL
NORM_EPS = 1e-6

GDN_QK_W = GDN_HEADS * GDN_DK
GDN_V_W = GDN_HEADS * GDN_DV
GDN_CONV_W = 2 * GDN_QK_W + GDN_V_W
GDN_W = GDN_CONV_W + GDN_V_W + 2 * GDN_HEADS
RWKV_W = RWKV_HEADS * RWKV_N
RWKV_IN_W = 3 * RWKV_W + RWKV_DECAY_LORA + RWKV_AAA_LORA + RWKV_GATE_LORA
AB_IN_W = GDN_W + RWKV_IN_W
AB_OUT_W = GDN_V_W + RWKV_W
NSA_KV_W = NSA_GROUPS * NSA_HD
NSA_IN_W = NSA_HEADS * NSA_HD + 6 * NSA_KV_W + 3 * NSA_HEADS
NSA_OUT_W = NSA_HEADS * NSA_HD
N_EVEN = (DEPTH + 1) // 2
N_ODD = DEPTH // 2

kernel_name = 'hybrid_gdn_rwkv7_nsa_trunk'


def rmsnorm(x, w):
    x32 = x.astype(jnp.float32)
    y = x32 * lax.rsqrt(jnp.mean(x32 * x32, axis=-1, keepdims=True) + NORM_EPS)
    return (y * w.astype(jnp.float32)).astype(x.dtype)


def l2norm(x):
    return x * lax.rsqrt(jnp.sum(x * x, axis=-1, keepdims=True) + 1e-6)


def partial_rotary(x, pos):
    half = ROPE_DIM // 2
    inv_freq = ROPE_THETA ** (-jnp.arange(half, dtype=jnp.float32) * (2.0 / ROPE_DIM))
    ang = pos.astype(jnp.float32)[:, None, :, None] * inv_freq
    cos = jnp.cos(ang).astype(x.dtype)
    sin = jnp.sin(ang).astype(x.dtype)
    x1, x2, rest = x[..., :half], x[..., half:ROPE_DIM], x[..., ROPE_DIM:]
    return jnp.concatenate([x1 * cos - x2 * sin, x2 * cos + x1 * sin, rest], axis=-1)


def masked_softmax(s, mask):
    s = jnp.where(mask, s.astype(jnp.float32), -jnp.inf)
    m = jnp.max(s, axis=-1, keepdims=True)
    m = jnp.where(jnp.isfinite(m), m, 0.0)
    e = jnp.where(mask, jnp.exp(s - m), 0.0)
    return e / jnp.maximum(jnp.sum(e, axis=-1, keepdims=True), 1e-30)


def causal_dwconv(x, w):
    return lax.conv_general_dilated(
        x, w[:, None, :].astype(x.dtype), window_strides=(1,),
        padding=[(w.shape[0] - 1, 0)], dimension_numbers=('NWC', 'WIO', 'NWC'),
        feature_group_count=x.shape[-1])


def gated_delta_chunked(q, k, v, beta, g):
    B_, H_, T_, dk = q.shape
    dv = v.shape[-1]
    C = GDN_CHUNK
    NC = T_ // C
    q = q.reshape(B_, H_, NC, C, dk)
    k = k.reshape(B_, H_, NC, C, dk)
    v = v.reshape(B_, H_, NC, C, dv)
    beta = beta.reshape(B_, H_, NC, C)
    gc = jnp.cumsum(g.reshape(B_, H_, NC, C), axis=-1)
    tril = jnp.tril(jnp.ones((C, C), bool))
    strict = jnp.tril(jnp.ones((C, C), bool), -1)
    decay = jnp.exp(jnp.where(tril, gc[..., :, None] - gc[..., None, :], -jnp.inf))
    a_mat = jnp.where(strict, beta[..., :, None] * jnp.einsum('bhncd,bhned->bhnce', k, k) * decay, 0.0)
    rhs = jnp.concatenate([v * beta[..., None], k * (beta * jnp.exp(gc))[..., None]], axis=-1)
    sol = lax.linalg.triangular_solve(a_mat + jnp.eye(C, dtype=a_mat.dtype), rhs,
                                      left_side=True, lower=True, unit_diagonal=True)
    u, w = sol[..., :dv], sol[..., dv:]
    intra = jnp.where(tril, jnp.einsum('bhncd,bhned->bhnce', q, k) * decay, 0.0)
    q_g = q * jnp.exp(gc)[..., None]
    k_g = k * jnp.exp(gc[..., -1:] - gc)[..., None]
    g_last = jnp.exp(gc[..., -1])

    def step(S, inp):
        u_n, w_n, q_n, k_n, a_n, gl = inp
        v_new = u_n - jnp.einsum('bhcd,bhdv->bhcv', w_n, S)
        o = jnp.einsum('bhcd,bhdv->bhcv', q_n, S) + jnp.einsum('bhce,bhev->bhcv', a_n, v_new)
        S = S * gl[..., None, None] + jnp.einsum('bhcd,bhcv->bhdv', k_n, v_new)
        return S, o

    xs = tuple(jnp.moveaxis(t, 2, 0) for t in (u, w, q_g, k_g, intra, g_last))
    S0 = jnp.zeros((B_, H_, dk, dv), jnp.float32)
    _, o = lax.scan(step, S0, xs)
    return jnp.moveaxis(o, 0, 2).reshape(B_, H_, T_, dv)


def gdn_mixer(f, conv_w, a_log, dt_bias, norm_w):
    B_, T_, _ = f.shape
    qkv = jax.nn.silu(causal_dwconv(f[..., :GDN_CONV_W], conv_w))
    q = qkv[..., :GDN_QK_W]
    k = qkv[..., GDN_QK_W:2 * GDN_QK_W]
    v = qkv[..., 2 * GDN_QK_W:]
    z = f[..., GDN_CONV_W:GDN_CONV_W + GDN_V_W]
    b = f[..., GDN_CONV_W + GDN_V_W:GDN_CONV_W + GDN_V_W + GDN_HEADS]
    a = f[..., GDN_CONV_W + GDN_V_W + GDN_HEADS:]

    def heads(t, d):
        return t.reshape(B_, T_, GDN_HEADS, d).transpose(0, 2, 1, 3)

    q = l2norm(heads(q, GDN_DK)) * GDN_DK ** -0.5
    k = l2norm(heads(k, GDN_DK))
    v = heads(v, GDN_DV)
    beta = jax.nn.sigmoid(b).transpose(0, 2, 1)
    g = (-jnp.exp(a_log) * jax.nn.softplus(a + dt_bias)).transpose(0, 2, 1)
    o = gated_delta_chunked(q, k, v, beta, g).transpose(0, 2, 1, 3)
    o = rmsnorm(o, norm_w) * jax.nn.silu(z.reshape(B_, T_, GDN_HEADS, GDN_DV))
    return o.reshape(B_, T_, GDN_V_W)


def rwkv7_mixer(f, mu, w0, w2, a0, a2, g2, k_k, k_a, r_k, ln_w, ln_b):
    B_, T_, _ = f.shape
    f_prev = jnp.concatenate([jnp.zeros_like(f[:, :1]), f[:, :-1]], axis=1)
    f = f + (f_prev - f) * mu
    W = RWKV_W
    r, k, v = f[..., :W], f[..., W:2 * W], f[..., 2 * W:3 * W]
    o0 = 3 * W
    wd = f[..., o0:o0 + RWKV_DECAY_LORA]
    ad = f[..., o0 + RWKV_DECAY_LORA:o0 + RWKV_DECAY_LORA + RWKV_AAA_LORA]
    gd = f[..., o0 + RWKV_DECAY_LORA + RWKV_AAA_LORA:]
    w_log = -jax.nn.softplus(-(w0 + jnp.tanh(wd) @ w2)) - 0.5
    decay = jnp.exp(-jnp.exp(w_log))
    a = jax.nn.sigmoid(a0 + ad @ a2)
    g = jax.nn.sigmoid(gd) @ g2

    def heads(t):
        return t.reshape(B_, T_, RWKV_HEADS, RWKV_N)

    kk = l2norm(heads(k * k_k))
    k = k * (1.0 + (a - 1.0) * k_a)
    r_h, k_h, v_h, w_h, a_h = heads(r), heads(k), heads(v), heads(decay), heads(a)

    def step(S, inp):
        r_t, w_t, k_t, v_t, kk_t, a_t = inp
        sa = jnp.einsum('bhvk,bhk->bhv', S, -kk_t)
        S = (S * w_t[:, :, None, :] + sa[..., None] * (kk_t * a_t)[:, :, None, :]
             + v_t[..., None] * k_t[:, :, None, :])
        return S, jnp.einsum('bhvk,bhk->bhv', S, r_t)

    xs = tuple(jnp.moveaxis(t, 1, 0) for t in (r_h, w_h, k_h, v_h, kk, a_h))
    S0 = jnp.zeros((B_, RWKV_HEADS, RWKV_N, RWKV_N), jnp.float32)
    _, y = lax.scan(step, S0, xs)
    y = jnp.moveaxis(y, 0, 1)
    mean = jnp.mean(y, axis=-1, keepdims=True)
    var = jnp.mean(jnp.square(y - mean), axis=-1, keepdims=True)
    y = ((y - mean) * lax.rsqrt(var + RWKV_LN_EPS)).reshape(B_, T_, W) * ln_w + ln_b
    bonus = jnp.sum(r_h * k_h * r_k, axis=-1, keepdims=True) * v_h
    return (y + bonus.reshape(B_, T_, W)) * g


def compress_blocks(t, pe, w1, w2):
    B_, G_, T_, d = t.shape
    c = t.reshape(B_, G_, T_ // CMP_STRIDE, CMP_STRIDE, d)
    blocks = jnp.concatenate([c[:, :, :-1], c[:, :, 1:]], axis=3) + pe
    h = jax.nn.silu(blocks.reshape(B_, G_, -1, CMP_LEN * d) @ w1)
    return h @ w2


def nsa_mixer(f, pos, pe_k, w1_k, w2_k, pe_v, w1_v, w2_v):
    B_, T_, _ = f.shape
    H, G, D = NSA_HEADS, NSA_GROUPS, NSA_HD
    HPG = H // G
    QW, KV = H * D, G * D
    q = f[..., :QW].reshape(B_, T_, H, D).transpose(0, 2, 1, 3)
    k_c, v_c, k_s, v_s, k_w, v_w = [
        f[..., QW + i * KV:QW + (i + 1) * KV].reshape(B_, T_, G, D).transpose(0, 2, 1, 3)
        for i in range(6)]
    gates = jax.nn.sigmoid(f[..., QW + 6 * KV:]).reshape(B_, T_, H, 3).transpose(0, 2, 1, 3)
    q = partial_rotary(q, pos) * D ** -0.5
    k_s = partial_rotary(k_s, pos)
    k_w = partial_rotary(k_w, pos)
    n_cmp = T_ // CMP_STRIDE - 1
    kc = partial_rotary(compress_blocks(k_c, pe_k, w1_k, w2_k), pos[:, CMP_LEN - 1::CMP_STRIDE])
    vc = compress_blocks(v_c, pe_v, w1_v, w2_v)
    n_sel = T_ // SEL_BLOCK
    n_top = min(SEL_TOPN, n_sel)
    ks_blk = k_s.reshape(B_, G, n_sel, SEL_BLOCK, D)
    vs_blk = v_s.reshape(B_, G, n_sel, SEL_BLOCK, D)
    pad = ((0, 0), (0, 0), (WINDOW, 0), (0, 0))
    kw_pad = jnp.pad(k_w, pad)
    vw_pad = jnp.pad(v_w, pad)
    c_start = jnp.arange(n_cmp) * CMP_STRIDE
    s_start = jnp.arange(n_sel) * SEL_BLOCK
    overlap = jnp.clip(jnp.minimum(c_start[:, None] + CMP_LEN, s_start[None, :] + SEL_BLOCK)
                       - jnp.maximum(c_start[:, None], s_start[None, :]), 0, None).astype(jnp.float32) / CMP_LEN
    cmp_end = c_start + CMP_LEN - 1
    sel_id = jnp.arange(n_sel)
    bi = jnp.arange(B_)[:, None, None, None]
    gi = jnp.arange(G)[None, :, None, None]

    def q_block(i):
        s = i * NSA_QBLOCK
        t = s + jnp.arange(NSA_QBLOCK)
        qb = lax.dynamic_slice_in_dim(q, s, NSA_QBLOCK, axis=2).reshape(B_, G, HPG, NSA_QBLOCK, D)
        sc = jnp.einsum('bghqd,bgnd->bghqn', qb, kc, preferred_element_type=jnp.float32)
        p_c = masked_softmax(sc, cmp_end[None, :] <= t[:, None])
        o_c = jnp.einsum('bghqn,bgnd->bghqd', p_c, vc)
        imp = jnp.einsum('bghqn,nj->bgqj', p_c, overlap)
        cur = t[:, None] // SEL_BLOCK
        valid = sel_id[None, :] <= cur
        forced = (sel_id[None, :] == 0) | (sel_id[None, :] == cur) | (sel_id[None, :] == cur - 1)
        score = jnp.where(valid, jnp.where(forced, jnp.inf, imp), -jnp.inf)
        _, idx = lax.top_k(score, n_top)
        ksel = ks_blk[bi, gi, idx].reshape(B_, G, NSA_QBLOCK, n_top * SEL_BLOCK, D)
        vsel = vs_blk[bi, gi, idx].reshape(B_, G, NSA_QBLOCK, n_top * SEL_BLOCK, D)
        kpos = (idx[..., None] * SEL_BLOCK + jnp.arange(SEL_BLOCK)).reshape(B_, G, NSA_QBLOCK, -1)
        ss = jnp.einsum('bghqd,bgqkd->bghqk', qb, ksel, preferred_element_type=jnp.float32)
        p_s = masked_softmax(ss, (kpos <= t[:, None])[:, :, None])
        o_s = jnp.einsum('bghqk,bgqkd->bghqd', p_s, vsel)
        kwb = lax.dynamic_slice_in_dim(kw_pad, s, WINDOW + NSA_QBLOCK, axis=2)
        vwb = lax.dynamic_slice_in_dim(vw_pad, s, WINDOW + NSA_QBLOCK, axis=2)
        kp = s - WINDOW + jnp.arange(WINDOW + NSA_QBLOCK)
        wmask = (kp[None, :] >= 0) & (kp[None, :] <= t[:, None]) & (kp[None, :] > t[:, None] - WINDOW)
        sw = jnp.einsum('bghqd,bgkd->bghqk', qb, kwb, preferred_element_type=jnp.float32)
        p_w = masked_softmax(sw, wmask)
        o_w = jnp.einsum('bghqk,bgkd->bghqd', p_w, vwb)
        gb = lax.dynamic_slice_in_dim(gates, s, NSA_QBLOCK, axis=2).reshape(B_, G, HPG, NSA_QBLOCK, 3)
        o = gb[..., 0:1] * o_c + gb[..., 1:2] * o_s + gb[..., 2:3] * o_w
        return o.reshape(B_, H, NSA_QBLOCK, D)

    o = lax.map(q_block, jnp.arange(T_ // NSA_QBLOCK))
    return o.transpose(1, 0, 3, 2, 4).reshape(B_, T_, H * D)


def setup_inputs(seed: int = 0) -> dict:
    key = jax.random.key(seed)
    keys = iter(jax.random.split(key, 40))

    def nrm(shape, scale):
        return jax.random.normal(next(keys), shape, jnp.float32) * scale

    def gain(shape):
        return 1.0 + 0.05 * jax.random.normal(next(keys), shape, jnp.float32)

    def unif(shape, lo, hi):
        return jax.random.uniform(next(keys), shape, jnp.float32, lo, hi)

    E, O = N_EVEN, N_ODD
    x = nrm((BATCH, SEQ, D_MODEL), 1.0)
    positions = (jax.random.randint(next(keys), (BATCH, 1), 0, 1024, dtype=jnp.int32)
                 + jnp.arange(SEQ, dtype=jnp.int32)[None, :])
    dt = jnp.exp(unif((E, GDN_HEADS), math.log(1e-3), math.log(1e-1)))
    return {
        'x': x,
        'positions': positions,
        'norm_mix_pre': gain((DEPTH, D_MODEL)),
        'norm_mix_post': gain((DEPTH, D_MODEL)),
        'norm_ffn_pre': gain((DEPTH, D_MODEL)),
        'norm_ffn_post': gain((DEPTH, D_MODEL)),
        'w_ffn_up': nrm((DEPTH, D_MODEL, D_FF), D_MODEL ** -0.5),
        'w_ffn_down': nrm((DEPTH, D_FF, D_MODEL), D_FF ** -0.5),
        'ab_w_in': nrm((E, D_MODEL, AB_IN_W), D_MODEL ** -0.5),
        'ab_w_out': nrm((E, AB_OUT_W, D_MODEL), AB_OUT_W ** -0.5),
        'gdn_conv': nrm((E, GDN_CONV, GDN_CONV_W), GDN_CONV ** -0.5),
        'gdn_a_log': jnp.log(unif((E, GDN_HEADS), 1.0, 16.0)),
        'gdn_dt_bias': dt + jnp.log(-jnp.expm1(-dt)),
        'gdn_norm': gain((E, GDN_DV)),
        'rwkv_mu': unif((E, RWKV_IN_W), 0.0, 1.0),
        'rwkv_w0': unif((E, RWKV_W), -6.0, -1.0),
        'rwkv_w2': nrm((E, RWKV_DECAY_LORA, RWKV_W), 0.5 * RWKV_DECAY_LORA ** -0.5),
        'rwkv_a0': nrm((E, RWKV_W), 0.1),
        'rwkv_a2': nrm((E, RWKV_AAA_LORA, RWKV_W), 0.5 * RWKV_AAA_LORA ** -0.5),
        'rwkv_g2': nrm((E, RWKV_GATE_LORA, RWKV_W), RWKV_GATE_LORA ** -0.5),
        'rwkv_k_k': 0.85 + nrm((E, RWKV_W), 0.05),
        'rwkv_k_a': gain((E, RWKV_W)),
        'rwkv_r_k': nrm((E, RWKV_HEADS, RWKV_N), 0.1),
        'rwkv_ln_w': gain((E, RWKV_W)),
        'rwkv_ln_b': nrm((E, RWKV_W), 0.01),
        'nsa_w_in': nrm((O, D_MODEL, NSA_IN_W), D_MODEL ** -0.5),
        'nsa_w_out': nrm((O, NSA_OUT_W, D_MODEL), NSA_OUT_W ** -0.5),
        'nsa_pe_k': nrm((O, CMP_LEN, NSA_HD), 0.1),
        'nsa_w1_k': nrm((O, CMP_LEN * NSA_HD, CMP_HIDDEN), (CMP_LEN * NSA_HD) ** -0.5),
        'nsa_w2_k': nrm((O, CMP_HIDDEN, NSA_HD), CMP_HIDDEN ** -0.5),
        'nsa_pe_v': nrm((O, CMP_LEN, NSA_HD), 0.1),
        'nsa_w1_v': nrm((O, CMP_LEN * NSA_HD, CMP_HIDDEN), (CMP_LEN * NSA_HD) ** -0.5),
        'nsa_w2_v': nrm((O, CMP_HIDDEN, NSA_HD), CMP_HIDDEN ** -0.5),
    }


def reference(x, positions, norm_mix_pre, norm_mix_post, norm_ffn_pre, norm_ffn_post,
              w_ffn_up, w_ffn_down, ab_w_in, ab_w_out, gdn_conv, gdn_a_log, gdn_dt_bias,
              gdn_norm, rwkv_mu, rwkv_w0, rwkv_w2, rwkv_a0, rwkv_a2, rwkv_g2, rwkv_k_k,
              rwkv_k_a, rwkv_r_k, rwkv_ln_w, rwkv_ln_b, nsa_w_in, nsa_w_out, nsa_pe_k,
              nsa_w1_k, nsa_w2_k, nsa_pe_v, nsa_w1_v, nsa_w2_v):
    h = x
    for layer in range(DEPTH):
        u = rmsnorm(h, norm_mix_pre[layer])
        j = layer // 2
        if layer % 2 == 0:
            f = (u @ ab_w_in[j]).astype(jnp.float32)
            o_a = gdn_mixer(f[..., :GDN_W], gdn_conv[j], gdn_a_log[j], gdn_dt_bias[j], gdn_norm[j])
            o_b = rwkv7_mixer(f[..., GDN_W:], rwkv_mu[j], rwkv_w0[j], rwkv_w2[j], rwkv_a0[j],
                              rwkv_a2[j], rwkv_g2[j], rwkv_k_k[j], rwkv_k_a[j], rwkv_r_k[j],
                              rwkv_ln_w[j], rwkv_ln_b[j])
            mix = jnp.concatenate([o_a, o_b], axis=-1).astype(h.dtype) @ ab_w_out[j]
        else:
            f = u @ nsa_w_in[j]
            o_c = nsa_mixer(f, positions, nsa_pe_k[j], nsa_w1_k[j], nsa_w2_k[j],
                            nsa_pe_v[j], nsa_w1_v[j], nsa_w2_v[j])
            mix = o_c.astype(h.dtype) @ nsa_w_out[j]
        h = h + rmsnorm(mix, norm_mix_post[layer])
        u = rmsnorm(h, norm_ffn_pre[layer])
        ff = jnp.square(jax.nn.relu(u @ w_ffn_up[layer])) @ w_ffn_down[layer]
        h = h + rmsnorm(ff, norm_ffn_post[layer])
    return h
```

```python
import functools
import math

import jax
import jax.numpy as jnp
from jax import lax
from jax.experimental import pallas as pl
from jax.experimental.pallas import tpu as pltpu

f32 = jnp.float32
bf16 = jnp.bfloat16
HI = lax.Precision.HIGHEST

V7X_VMEM_LIMIT_BYTES = 56 * 1024 * 1024
LANES = 128

D_MODEL = 1024
D_FF = 4 * D_MODEL
NORM_EPS = 1e-6
GDN_HEADS = 4
GDN_D = 128
GDN_CONV = 4
GDN_CHUNK = 128
RWKV_HEADS = 8
RWKV_N = 64
RWKV_W = RWKV_HEADS * RWKV_N
RWKV_CHUNK = 64
RWKV_LN_EPS = 64e-5
NSA_HEADS = 16
NSA_GROUPS = 2
NSA_HPG = NSA_HEADS // NSA_GROUPS
NSA_HD = 64
CMP_LEN = 32
CMP_STRIDE = 16
CMP_HIDDEN = 256
SEL_BLOCK = 64
SEL_TOPN = 16
WINDOW = 512
NSA_QBLOCK = 64
ROPE_THETA = 500000.0
ROPE_DIM = NSA_HD // 4
ROPE_HALF = ROPE_DIM // 2

AB_COLS = 4096
AB_RWKV0 = 2048
AB_LORA0 = 3584
AB_BA0 = 3840
NSA_COLS = 2048
NSA_KV0 = 1024
NSA_GATE0 = 1792


def _cparams(sem):
    return pltpu.CompilerParams(dimension_semantics=sem, vmem_limit_bytes=V7X_VMEM_LIMIT_BYTES)


def _rms(x, g):
    return x * lax.rsqrt(jnp.mean(x * x, axis=-1, keepdims=True) + NORM_EPS) * g


def _dot(a, b, precision=None):
    return jnp.dot(a, b, precision=precision, preferred_element_type=f32)


def _dot_nt(a, b, precision=None):
    return lax.dot_general(a, b, (((1,), (1,)), ((), ())), precision=precision, preferred_element_type=f32)


def _dot_tn(a, b, precision=None):
    return lax.dot_general(a, b, (((0,), (0,)), ((), ())), precision=precision, preferred_element_type=f32)


def _iota2(shape, axis):
    return lax.broadcasted_iota(jnp.int32, shape, axis)


def _norm_matmul_body(x_ref, g_ref, w_ref, o_ref, u_ref):
    @pl.when(pl.program_id(1) == 0)
    def _():
        u_ref[...] = _rms(x_ref[...], g_ref[...]).astype(bf16)

    o_ref[...] = _dot(u_ref[...], w_ref[...])


def norm_matmul(x, g, w, tm=1024, tn=512):
    m, d = x.shape
    n = w.shape[1]
    tm = min(tm, m)
    return pl.pallas_call(
        _norm_matmul_body,
        grid=(m // tm, n // tn),
        in_specs=[pl.BlockSpec((tm, d), lambda i, j: (i, 0)),
                  pl.BlockSpec((1, d), lambda i, j: (0, 0)),
                  pl.BlockSpec((d, tn), lambda i, j: (0, j))],
        out_specs=pl.BlockSpec((tm, tn), lambda i, j: (i, j)),
        out_shape=jax.ShapeDtypeStruct((m, n), f32),
        scratch_shapes=[pltpu.VMEM((tm, d), bf16)],
        compiler_params=_cparams(("parallel", "arbitrary")),
        name="norm_matmul",
    )(x, g.reshape(1, d), w)


def _out_res_norm_body(*refs, n_parts):
    a_refs = refs[:n_parts]
    w_refs = refs[n_parts:2 * n_parts]
    h_ref, g_ref, o_ref = refs[2 * n_parts:]
    y = _dot(a_refs[0][...].astype(bf16), w_refs[0][...])
    for a_ref, w_ref in zip(a_refs[1:], w_refs[1:]):
        y = y + _dot(a_ref[...].astype(bf16), w_ref[...])
    o_ref[...] = h_ref[...] + _rms(y, g_ref[...])


def out_res_norm(parts, w_parts, h, g, tm=512):
    m, d = h.shape
    tm = min(tm, m)
    n_parts = len(parts)
    in_specs = ([pl.BlockSpec((tm, p.shape[1]), lambda i: (i, 0)) for p in parts]
                + [pl.BlockSpec(w.shape, lambda i: (0, 0)) for w in w_parts]
                + [pl.BlockSpec((tm, d), lambda i: (i, 0)), pl.BlockSpec((1, d), lambda i: (0, 0))])
    return pl.pallas_call(
        functools.partial(_out_res_norm_body, n_parts=n_parts),
        grid=(m // tm,),
        in_specs=in_specs,
        out_specs=pl.BlockSpec((tm, d), lambda i: (i, 0)),
        out_shape=jax.ShapeDtypeStruct((m, d), f32),
        compiler_params=_cparams(("parallel",)),
        name="out_res_norm",
    )(*parts, *w_parts, h, g.reshape(1, d))


def _ffn_body(h_ref, g1_ref, wup_ref, wdn_ref, g2_ref, o_ref, u_ref, acc_ref):
    k = pl.program_id(1)

    @pl.when(k == 0)
    def _():
        u_ref[...] = _rms(h_ref[...], g1_ref[...]).astype(bf16)
        acc_ref[...] = jnp.zeros_like(acc_ref)

    a = _dot(u_ref[...], wup_ref[...])
    a = jnp.square(jnp.maximum(a, 0.0))
    acc_ref[...] += _dot(a.astype(bf16), wdn_ref[...])

    @pl.when(k == pl.num_programs(1) - 1)
    def _():
        o_ref[...] = h_ref[...] + _rms(acc_ref[...], g2_ref[...])


def ffn(h, g1, wup, wdn, g2, tm=1024, tf=512):
    m, d = h.shape
    ff = wup.shape[1]
    tm = min(tm, m)
    return pl.pallas_call(
        _ffn_body,
        grid=(m // tm, ff // tf),
        in_specs=[pl.BlockSpec((tm, d), lambda i, k: (i, 0)),
                  pl.BlockSpec((1, d), lambda i, k: (0, 0)),
                  pl.BlockSpec((d, tf), lambda i, k: (0, k)),
                  pl.BlockSpec((tf, d), lambda i, k: (k, 0)),
                  pl.BlockSpec((1, d), lambda i, k: (0, 0))],
        out_specs=pl.BlockSpec((tm, d), lambda i, k: (i, 0)),
        out_shape=jax.ShapeDtypeStruct((m, d), f32),
        scratch_shapes=[pltpu.VMEM((tm, d), bf16), pltpu.VMEM((tm, d), f32)],
        compiler_params=_cparams(("parallel", "arbitrary")),
        name="ffn",
    )(h, g1.reshape(1, d), wup, wdn, g2.reshape(1, d))


def _neumann_inverse(n_mat, size):
    eye = (_iota2((size, size), 0) == _iota2((size, size), 1)).astype(f32)
    t = eye + n_mat
    p = n_mat
    for _ in range(int(math.log2(size)) - 1):
        p = _dot(p, p, HI)
        t = t + _dot(t, p, HI)
    return t


def _gdn_body(q_ref, k_ref, v_ref, z_ref, ba_ref, cw_ref, arow_ref, dtrow_ref, nw_ref, o_ref,
              xp_ref, qkv_ref, s_ref, *, tt):
    c = GDN_CHUNK
    head = pl.program_id(1)
    t_idx = pl.program_id(2)

    @pl.when(t_idx == 0)
    def _():
        xp_ref[:, pl.ds(0, 8), :] = jnp.zeros((3, 8, GDN_D), f32)
        s_ref[...] = jnp.zeros_like(s_ref)

    for idx, ref in enumerate((q_ref, k_ref, v_ref)):
        xp_ref[idx, pl.ds(8, tt), :] = ref[0]
        w = cw_ref[idx]
        y = xp_ref[idx, pl.ds(8, tt), :] * w[3:4, :]
        for j in range(GDN_CONV - 1):
            y = y + xp_ref[idx, pl.ds(5 + j, tt), :] * w[j:j + 1, :]
        y = y * jax.nn.sigmoid(y)
        if idx < 2:
            y = y * lax.rsqrt(jnp.sum(y * y, axis=-1, keepdims=True) + 1e-6)
        if idx == 0:
            y = y * (GDN_D ** -0.5)
        qkv_ref[idx] = y
        xp_ref[idx, pl.ds(0, 8), :] = xp_ref[idx, pl.ds(tt, 8), :]

    lane = _iota2((c, LANES), 1)
    row = _iota2((c, c), 0)
    col = _iota2((c, c), 1)
    tril = row >= col
    strict = row > col
    eye = row == col
    cum_l = tril.astype(f32)

    def chunk(ci, carry):
        off = pl.multiple_of(ci * c, c)
        qn = qkv_ref[0, pl.ds(off, c), :]
        kn = qkv_ref[1, pl.ds(off, c), :]
        vv = qkv_ref[2, pl.ds(off, c), :]
        ba = ba_ref[0, pl.ds(off, c), :]
        beta = jnp.sum(jnp.where(lane == head, jax.nn.sigmoid(ba), 0.0), axis=-1, keepdims=True)
        g_all = -jnp.exp(arow_ref[...]) * jax.nn.softplus(ba + dtrow_ref[...])
        gcum = _dot(cum_l, g_all, HI)
        gc = jnp.sum(jnp.where(lane == head + GDN_HEADS, gcum, 0.0), axis=-1, keepdims=True)
        gc_row = jnp.sum(jnp.where(eye, jnp.broadcast_to(gc, (c, c)), 0.0), axis=0, keepdims=True)
        gc_last = jnp.sum(jnp.where(_iota2((c, 1), 0) == c - 1, gc, 0.0), axis=0, keepdims=True)
        decay = jnp.exp(jnp.where(tril, gc - gc_row, -jnp.inf))
        knb = kn.astype(bf16)
        kk = _dot_nt(knb, knb)
        qk = _dot_nt(qn.astype(bf16), knb)
        a_mat = jnp.where(strict, beta * kk * decay, 0.0)
        t_inv = _neumann_inverse(-a_mat, c)
        egc = jnp.exp(gc)
        u = _dot(t_inv, vv * beta, HI)
        w = _dot(t_inv, kn * (beta * egc), HI)
        intra = jnp.where(tril, qk * decay, 0.0)
        q_g = qn * egc
        k_g = kn * jnp.exp(gc_last - gc)
        s = s_ref[...]
        sb = s.astype(bf16)
        v_new = u - _dot(w.astype(bf16), sb)
        vnb = v_new.astype(bf16)
        o = _dot(q_g.astype(bf16), sb) + _dot(intra.astype(bf16), vnb)
        s_ref[...] = s * jnp.exp(gc_last) + _dot_tn(k_g.astype(bf16), vnb)
        z = z_ref[0, pl.ds(off, c), :]
        o_ref[0, pl.ds(off, c), :] = _rms(o, nw_ref[...]) * (z * jax.nn.sigmoid(z))
        return carry

    lax.fori_loop(0, tt // c, chunk, 0)


def gdn_mixer(f3, conv_w, arow, dtrow, norm_w, tt=512):
    b, t, _ = f3.shape
    tt = min(tt, t)
    hq = GDN_HEADS

    def col(base):
        return pl.BlockSpec((1, tt, GDN_D), lambda bi, hi, ti, base=base: (bi, ti, base + hi))

    return pl.pallas_call(
        functools.partial(_gdn_body, tt=tt),
        grid=(b, hq, t // tt),
        in_specs=[col(0), col(hq), col(2 * hq), col(3 * hq),
                  pl.BlockSpec((1, tt, LANES), lambda bi, hi, ti: (bi, ti, AB_BA0 // LANES)),
                  pl.BlockSpec((3, GDN_CONV, GDN_D), lambda bi, hi, ti: (0, 0, hi)),
                  pl.BlockSpec((1, LANES), lambda bi, hi, ti: (0, 0)),
                  pl.BlockSpec((1, LANES), lambda bi, hi, ti: (0, 0)),
                  pl.BlockSpec((1, GDN_D), lambda bi, hi, ti: (0, 0))],
        out_specs=pl.BlockSpec((1, tt, GDN_D), lambda bi, hi, ti: (bi, ti, hi)),
        out_shape=jax.ShapeDtypeStruct((b, t, hq * GDN_D), f32),
        scratch_shapes=[pltpu.VMEM((3, tt + 8, GDN_D), f32),
                        pltpu.VMEM((3, tt, GDN_D), f32),
                        pltpu.VMEM((GDN_D, GDN_D), f32)],
        compiler_params=_cparams(("parallel", "parallel", "arbitrary")),
        name="gdn",
    )(f3, f3, f3, f3, f3, conv_w, arow, dtrow, norm_w)


def _rwkv_prep_body(r_ref, k_ref, v_ref, l_ref, mu_ref, w0_ref, w2_ref, a0_ref, a2_ref, g2_ref, kk_ref, ka_ref,
                    ro_ref, lw_ref, k2_ref, vo_ref, kko_ref, ao_ref, go_ref, xp_ref, *, tt):
    @pl.when(pl.program_id(1) == 0)
    def _():
        xp_ref[pl.ds(0, 8), :] = jnp.zeros((8, xp_ref.shape[1]), f32)

    w = RWKV_W
    xp_ref[pl.ds(8, tt), 0:w] = r_ref[0]
    xp_ref[pl.ds(8, tt), w:2 * w] = k_ref[0]
    xp_ref[pl.ds(8, tt), 2 * w:3 * w] = v_ref[0]
    xp_ref[pl.ds(8, tt), 3 * w:] = l_ref[0]
    x = xp_ref[pl.ds(8, tt), :]
    x = x + (xp_ref[pl.ds(7, tt), :] - x) * mu_ref[...]
    xp_ref[pl.ds(0, 8), :] = xp_ref[pl.ds(tt, 8), :]

    r, k, v = x[:, 0:w], x[:, w:2 * w], x[:, 2 * w:3 * w]
    wd, ad, gd = x[:, 3 * w:3 * w + 64], x[:, 3 * w + 64:3 * w + 128], x[:, 3 * w + 128:]
    w_log = -jax.nn.softplus(-(w0_ref[...] + _dot(jnp.tanh(wd), w2_ref[...], HI))) - 0.5
    a = jax.nn.sigmoid(a0_ref[...] + _dot(ad, a2_ref[...], HI))
    ro_ref[0] = r
    lw_ref[0] = -jnp.exp(w_log)
    k2_ref[0] = k * (1.0 + (a - 1.0) * ka_ref[...])
    vo_ref[0] = v
    kko_ref[0] = k * kk_ref[...]
    ao_ref[0] = a
    go_ref[0] = _dot(jax.nn.sigmoid(gd), g2_ref[...], HI)


def rwkv_prep(f3, mu, w0, w2, a0, a2, g2, k_k, k_a, tt=256):
    b, t, _ = f3.shape
    tt = min(tt, t)
    w = RWKV_W
    wide = 3 * w + 256
    row = lambda n: pl.BlockSpec((1, n), lambda bi, ti: (0, 0))
    full = lambda a: pl.BlockSpec(a.shape, lambda bi, ti: (0, 0))
    out = pl.BlockSpec((1, tt, w), lambda bi, ti: (bi, ti, 0))
    return pl.pallas_call(
        functools.partial(_rwkv_prep_body, tt=tt),
        grid=(b, t // tt),
        in_specs=[pl.BlockSpec((1, tt, w), lambda bi, ti: (bi, ti, AB_RWKV0 // w)),
                  pl.BlockSpec((1, tt, w), lambda bi, ti: (bi, ti, AB_RWKV0 // w + 1)),
                  pl.BlockSpec((1, tt, w), lambda bi, ti: (bi, ti, AB_RWKV0 // w + 2)),
                  pl.BlockSpec((1, tt, 256), lambda bi, ti: (bi, ti, AB_LORA0 // 256)),
                  row(wide), row(w), full(w2), row(w), full(a2), full(g2), row(w), row(w)],
        out_specs=[out] * 7,
        out_shape=[jax.ShapeDtypeStruct((b, t, w), f32)] * 7,
        scratch_shapes=[pltpu.VMEM((tt + 8, wide), f32)],
        compiler_params=_cparams(("parallel", "arbitrary")),
        name="rwkv_prep",
    )(f3, f3, f3, f3, mu.reshape(1, wide), w0.reshape(1, w), w2, a0.reshape(1, w), a2, g2,
      k_k.reshape(1, w), k_a.reshape(1, w))


def _rwkv_scan_body(r_ref, lw_ref, k2_ref, v_ref, kk_ref, a_ref, g_ref, rk_ref, lnw_ref, lnb_ref, o_ref, s_ref,
                    *, tt):
    c = RWKV_CHUNK
    n = RWKV_N

    @pl.when(pl.program_id(1) == 0)
    def _():
        s_ref[...] = jnp.zeros_like(s_ref)

    row = _iota2((c, c), 0)
    col = _iota2((c, c), 1)
    tril = row >= col
    strict = row > col
    cum_l = tril.astype(f32)
    last = _iota2((c, 1), 0) == c - 1

    def chunk(ci, carry):
        off = pl.multiple_of(ci * c, c)
        for h in range(RWKV_HEADS):
            sl = slice(h * n, (h + 1) * n)
            rh = r_ref[0, pl.ds(off, c), sl]
            lw = lw_ref[0, pl.ds(off, c), sl]
            k2 = k2_ref[0, pl.ds(off, c), sl]
            vh = v_ref[0, pl.ds(off, c), sl]
            kkr = kk_ref[0, pl.ds(off, c), sl]
            lr = a_ref[0, pl.ds(off, c), sl]
            gh = g_ref[0, pl.ds(off, c), sl]
            kk = kkr * lax.rsqrt(jnp.sum(kkr * kkr, axis=-1, keepdims=True) + 1e-6)
            p = _dot(cum_l, lw, HI)
            ep = jnp.exp(p)
            em = jnp.exp(-p)
            r_t = rh * ep
            a_t = -kk * jnp.exp(p - lw)
            b_t = kk * lr * em
            k_t = k2 * em
            m_ab = jnp.where(strict, _dot_nt(a_t, b_t, HI), 0.0)
            m_ak = jnp.where(strict, _dot_nt(a_t, k_t, HI), 0.0)
            a_rb = jnp.where(tril, _dot_nt(r_t, b_t, HI), 0.0)
            a_rk = jnp.where(tril, _dot_nt(r_t, k_t, HI), 0.0)
            t_inv = _neumann_inverse(m_ab, c)
            w1 = _dot(t_inv, a_t, HI)
            u2 = _dot(t_inv, _dot(m_ak, vh, HI), HI)
            s0 = s_ref[h]
            u = _dot_nt(w1, s0, HI) + u2
            y = _dot_nt(r_t, s0, HI) + _dot(a_rb, u, HI) + _dot(a_rk, vh, HI)
            p_last = jnp.sum(jnp.where(last, p, 0.0), axis=0, keepdims=True)
            s_ref[h] = (s0 + _dot_tn(u, b_t, HI) + _dot_tn(vh, k_t, HI)) * jnp.exp(p_last)
            mean = jnp.mean(y, axis=-1, keepdims=True)
            yc = y - mean
            var = jnp.mean(yc * yc, axis=-1, keepdims=True)
            y = yc * lax.rsqrt(var + RWKV_LN_EPS) * lnw_ref[:, sl] + lnb_ref[:, sl]
            bonus = jnp.sum(rh * k2 * rk_ref[:, sl], axis=-1, keepdims=True) * vh
            o_ref[0, pl.ds(off, c), sl] = (y + bonus) * gh
        return carry

    lax.fori_loop(0, tt // c, chunk, 0)


def rwkv_scan(r, lw, k2, v, kk, a, g, r_k, ln_w, ln_b, tt=256):
    b, t, w = r.shape
    tt = min(tt, t)
    blk = pl.BlockSpec((1, tt, w), lambda bi, ti: (bi, ti, 0))
    row = pl.BlockSpec((1, w), lambda bi, ti: (0, 0))
    return pl.pallas_call(
        functools.partial(_rwkv_scan_body, tt=tt),
        grid=(b, t // tt),
        in_specs=[blk] * 7 + [row] * 3,
        out_specs=blk,
        out_shape=jax.ShapeDtypeStruct((b, t, w), f32),
        scratch_shapes=[pltpu.VMEM((RWKV_HEADS, RWKV_N, RWKV_N), f32)],
        compiler_params=_cparams(("parallel", "arbitrary")),
        name="rwkv_scan",
    )(r, lw, k2, v, kk, a, g, r_k.reshape(1, w), ln_w.reshape(1, w), ln_b.reshape(1, w))


def _rope128(x, cos, sin, sign_lo, sign_hi):
    r_hi = pltpu.roll(x, ROPE_HALF, 1)
    r_lo = pltpu.roll(x, LANES - ROPE_HALF, 1)
    return x * cos + (r_lo * sign_lo + r_hi * sign_hi) * sin


def _rope_tables(pos, freq_row):
    ang = pos * freq_row
    m = _iota2((1, LANES), 1) % NSA_HD
    sign_lo = jnp.where(m < ROPE_HALF, -1.0, 0.0).astype(f32)
    sign_hi = jnp.where((m >= ROPE_HALF) & (m < ROPE_DIM), 1.0, 0.0).astype(f32)
    return jnp.cos(ang), jnp.sin(ang), sign_lo, sign_hi


def _nsa_prep_body(q_ref, kc_i, vc_i, ks_i, vs_i, kw_i, vw_i, pos_ref, freq_ref,
                   qo_ref, kc_ref, vc_ref, ks_ref, vs_ref, kw_ref, vw_ref):
    cos, sin, s_lo, s_hi = _rope_tables(pos_ref[0], freq_ref[...])
    scale = NSA_HD ** -0.5
    for c in range(NSA_HEADS // 2):
        x = _rope128(q_ref[0, :, c * LANES:(c + 1) * LANES], cos, sin, s_lo, s_hi) * scale
        qo_ref[0, 2 * c] = x[:, :NSA_HD].astype(bf16)
        qo_ref[0, 2 * c + 1] = x[:, NSA_HD:].astype(bf16)

    def split(src, ref, rope, dtype):
        x = src[0]
        if rope:
            x = _rope128(x, cos, sin, s_lo, s_hi)
        ref[0, 0] = x[:, :NSA_HD].astype(dtype)
        ref[0, 1] = x[:, NSA_HD:].astype(dtype)

    split(kc_i, kc_ref, False, f32)
    split(vc_i, vc_ref, False, f32)
    split(ks_i, ks_ref, True, bf16)
    split(vs_i, vs_ref, False, bf16)
    split(kw_i, kw_ref, True, bf16)
    split(vw_i, vw_ref, False, bf16)


def nsa_prep(f3, pos3, freq_row, tt=512):
    b, t, _ = f3.shape
    tt = min(tt, t)
    g = NSA_GROUPS
    kv_in = [pl.BlockSpec((1, tt, LANES), lambda bi, ti, c=NSA_KV0 // LANES + i: (bi, ti, c)) for i in range(6)]
    kv_spec = pl.BlockSpec((1, g, tt, NSA_HD), lambda bi, ti: (bi, 0, ti, 0))
    kv32 = jax.ShapeDtypeStruct((b, g, t, NSA_HD), f32)
    kv16 = jax.ShapeDtypeStruct((b, g, t, NSA_HD), bf16)
    return pl.pallas_call(
        _nsa_prep_body,
        grid=(b, t // tt),
        in_specs=[pl.BlockSpec((1, tt, NSA_HEADS * NSA_HD), lambda bi, ti: (bi, ti, 0))] + kv_in
                 + [pl.BlockSpec((1, tt, 1), lambda bi, ti: (bi, ti, 0)),
                    pl.BlockSpec((1, LANES), lambda bi, ti: (0, 0))],
        out_specs=[pl.BlockSpec((1, NSA_HEADS, tt, NSA_HD), lambda bi, ti: (bi, 0, ti, 0))] + [kv_spec] * 6,
        out_shape=[jax.ShapeDtypeStruct((b, NSA_HEADS, t, NSA_HD), bf16), kv32, kv32, kv16, kv16, kv16, kv16],
        compiler_params=_cparams(("parallel", "parallel")),
        name="nsa_prep",
    )(f3, f3, f3, f3, f3, f3, f3, pos3, freq_row)


def _nsa_compress_body(kc_ref, vc_ref, pek_ref, w1k_ref, w2k_ref, pev_ref, w1v_ref, w2v_ref, pos_ref, freq_ref,
                       ko_ref, vo_ref):
    half = CMP_STRIDE * NSA_HD
    nrow = kc_ref.shape[2]
    last_row = _iota2((nrow, 1), 0) == nrow - 1

    def mlp(x, pe_ref, w1_ref, w2_ref):
        lo = _dot((x + pe_ref[:, :half]).astype(bf16), w1_ref[:half, :])
        hi = _dot((x + pe_ref[:, half:]).astype(bf16), w1_ref[half:, :])
        hi = jnp.where(last_row, 0.0, pltpu.roll(hi, nrow - 1, 0))
        hid = lo + hi
        hid = hid * jax.nn.sigmoid(hid)
        return _dot(hid.astype(bf16), w2_ref[...])

    ks = [mlp(kc_ref[0, g], pek_ref, w1k_ref, w2k_ref) for g in range(NSA_GROUPS)]
    cos, sin, s_lo, s_hi = _rope_tables(pos_ref[0], freq_ref[...])
    kr = _rope128(jnp.concatenate(ks, axis=-1), cos, sin, s_lo, s_hi)
    for g in range(NSA_GROUPS):
        ko_ref[0, g] = kr[:, g * NSA_HD:(g + 1) * NSA_HD].astype(bf16)
        vo_ref[0, g] = mlp(vc_ref[0, g], pev_ref, w1v_ref, w2v_ref).astype(bf16)


def nsa_compress(kc4, vc4, pe_k, w1_k, w2_k, pe_v, w1_v, w2_v, cpos3, freq_row):
    b, g, nrow, wide = kc4.shape
    full = lambda a: pl.BlockSpec(a.shape, lambda bi: (0,) * a.ndim)
    blk = pl.BlockSpec((1, g, nrow, wide), lambda bi: (bi, 0, 0, 0))
    out = pl.BlockSpec((1, g, nrow, NSA_HD), lambda bi: (bi, 0, 0, 0))
    return pl.pallas_call(
        _nsa_compress_body,
        grid=(b,),
        in_specs=[blk, blk, full(pe_k), full(w1_k), full(w2_k), full(pe_v), full(w1_v), full(w2_v),
                  pl.BlockSpec((1, nrow, 1), lambda bi: (bi, 0, 0)), pl.BlockSpec((1, LANES), lambda bi: (0, 0))],
        out_specs=[out, out],
        out_shape=[jax.ShapeDtypeStruct((b, g, nrow, NSA_HD), bf16)] * 2,
        compiler_params=_cparams(("parallel",)),
        name="nsa_compress",
    )(kc4, vc4, pe_k, w1_k, w2_k, pe_v, w1_v, w2_v, cpos3, freq_row)


def _nsa_attn_body(q_ref, kc_ref, vc_ref, ks_ref, vs_ref, kw_ref, vw_ref, gate_ref, ov_ref, o_ref,
                   m_ref, l_ref, acc_ref, *, key_tile):
    qb_n = NSA_QBLOCK
    rows = NSA_HPG * qb_n
    i = pl.program_id(2)
    s0 = i * qb_n
    qb = q_ref[0].reshape(rows, NSA_HD)
    t_row = s0 + _iota2((rows, 1), 0) % qb_n

    n_cmp = kc_ref.shape[2]
    cmp_end = _iota2((1, n_cmp), 1) * CMP_STRIDE + (CMP_LEN - 1)
    mask_c = cmp_end <= t_row
    sc = jnp.where(mask_c, _dot_nt(qb, kc_ref[0, 0]), -jnp.inf)
    mx = jnp.max(sc, axis=-1, keepdims=True)
    mx = jnp.where(mx > -jnp.inf, mx, 0.0)
    e = jnp.where(mask_c, jnp.exp(sc - mx), 0.0)
    p_c = e / jnp.maximum(jnp.sum(e, axis=-1, keepdims=True), 1e-30)
    o_c = _dot(p_c.astype(bf16), vc_ref[0, 0])

    n_sel = ov_ref.shape[1]
    imp = _dot(jnp.sum(p_c.reshape(NSA_HPG, qb_n, n_cmp), axis=0), ov_ref[...], HI)
    blk = _iota2((qb_n, n_sel), 1)
    valid = blk <= i
    forced = (blk == 0) | (blk == i) | (blk == i - 1)
    score = jnp.where(valid, jnp.where(forced, jnp.inf, imp), -jnp.inf)
    reps = LANES // n_sel
    score_w = jnp.concatenate([score] * reps, axis=-1) if reps > 1 else score
    width = score_w.shape[1]
    lane_blk = _iota2((qb_n, width), 1) % n_sel
    cnt = jnp.zeros((qb_n, width), jnp.int32)
    for d in range(1, n_sel):
        other = pltpu.roll(score_w, d, 1)
        ahead = (other > score_w) | ((other == score_w) & (lane_blk >= d))
        cnt = cnt + ahead.astype(jnp.int32)
    sel = ((cnt[:, :n_sel] < SEL_TOPN) & valid).astype(bf16)

    m_ref[...] = jnp.full_like(m_ref, -jnp.inf)
    l_ref[...] = jnp.zeros_like(l_ref)
    acc_ref[...] = jnp.zeros_like(acc_ref)
    blocks_per_tile = key_tile // SEL_BLOCK
    key_lane = _iota2((1, key_tile), 1)
    expand_blk = _iota2((n_sel, key_tile), 1) // SEL_BLOCK
    expand_row = _iota2((n_sel, key_tile), 0)

    def key_step(kt, carry):
        k0 = pl.multiple_of(kt * key_tile, key_tile)
        expand = (expand_blk + kt * blocks_per_tile == expand_row).astype(bf16)
        picked = _dot(sel, expand) > 0.5
        picked = jnp.broadcast_to(picked[None], (NSA_HPG, qb_n, key_tile)).reshape(rows, key_tile)
        mask = picked & (k0 + key_lane <= t_row)
        s = jnp.where(mask, _dot_nt(qb, ks_ref[0, 0, pl.ds(k0, key_tile), :]), -jnp.inf)
        m_old = m_ref[...]
        m_new = jnp.maximum(m_old, jnp.max(s, axis=-1, keepdims=True))
        alpha = jnp.exp(m_old - m_new)
        p = jnp.exp(s - m_new)
        l_ref[...] = alpha * l_ref[...] + jnp.sum(p, axis=-1, keepdims=True)
        acc_ref[...] = alpha * acc_ref[...] + _dot(p.astype(bf16), vs_ref[0, 0, pl.ds(k0, key_tile), :])
        m_ref[...] = m_new
        return carry

    lax.fori_loop(0, (s0 + qb_n + key_tile - 1) // key_tile, key_step, 0)
    o_s = acc_ref[...] / l_ref[...]

    span = WINDOW + qb_n
    w0 = pl.multiple_of(jnp.maximum(s0 - WINDOW, 0), qb_n)
    kp = w0 + _iota2((1, span), 1)
    mask_w = (kp <= t_row) & (kp > t_row - WINDOW)
    sw = jnp.where(mask_w, _dot_nt(qb, kw_ref[0, 0, pl.ds(w0, span), :]), -jnp.inf)
    mw = jnp.max(sw, axis=-1, keepdims=True)
    ew = jnp.exp(sw - mw)
    o_w = _dot(ew.astype(bf16), vw_ref[0, 0, pl.ds(w0, span), :]) / jnp.sum(ew, axis=-1, keepdims=True)

    gates = jax.nn.sigmoid(gate_ref[0])
    for h in range(NSA_HPG):
        r = slice(h * qb_n, (h + 1) * qb_n)
        o_ref[0, :, h * NSA_HD:(h + 1) * NSA_HD] = (gates[:, 3 * h:3 * h + 1] * o_c[r]
                                                    + gates[:, 3 * h + 1:3 * h + 2] * o_s[r]
                                                    + gates[:, 3 * h + 2:3 * h + 3] * o_w[r])


def nsa_attention(q, kc, vc, ks, vs, kw, vw, f3, overlap, key_tile=256):
    b, h, t, d = q.shape
    g = NSA_GROUPS
    key_tile = min(key_tile, t)
    n_cmp = kc.shape[2]
    seq = lambda: pl.BlockSpec((1, 1, t, d), lambda bi, gi, qi: (bi, gi, 0, 0))
    cmp_spec = lambda: pl.BlockSpec((1, 1, n_cmp, d), lambda bi, gi, qi: (bi, gi, 0, 0))
    rows = NSA_HPG * NSA_QBLOCK
    return pl.pallas_call(
        functools.partial(_nsa_attn_body, key_tile=key_tile),
        grid=(b, g, t // NSA_QBLOCK),
        in_specs=[pl.BlockSpec((1, NSA_HPG, NSA_QBLOCK, d), lambda bi, gi, qi: (bi, gi, qi, 0)),
                  cmp_spec(), cmp_spec(), seq(), seq(), seq(), seq(),
                  pl.BlockSpec((1, NSA_QBLOCK, LANES), lambda bi, gi, qi: (bi, qi, NSA_GATE0 // LANES + gi)),
                  pl.BlockSpec(overlap.shape, lambda bi, gi, qi: (0, 0))],
        out_specs=pl.BlockSpec((1, NSA_QBLOCK, NSA_HPG * d), lambda bi, gi, qi: (bi, qi, gi)),
        out_shape=jax.ShapeDtypeStruct((b, t, h * d), f32),
        scratch_shapes=[pltpu.VMEM((rows, 1), f32), pltpu.VMEM((rows, 1), f32), pltpu.VMEM((rows, d), f32)],
        compiler_params=_cparams(("parallel", "parallel", "arbitrary")),
        name="nsa_attn",
    )(q, kc, vc, ks, vs, kw, vw, f3, overlap)


def _place(cols, total, pieces):
    out = jnp.zeros((cols, total), f32)
    for start, mat in pieces:
        out = lax.dynamic_update_slice(out, mat.astype(f32), (0, start))
    return out


def _layer0(h, b, t, g_pre, w_in, w_out, g_post, conv, a_log, dt_bias, gnorm, mu, w0, w2, a0, a2, g2, k_k, k_a,
            r_k, ln_w, ln_b):
    gdn_w = 4 * GDN_HEADS * GDN_D
    w_pad = _place(D_MODEL, AB_COLS, [
        (0, w_in[:, :gdn_w]),
        (AB_BA0, w_in[:, gdn_w:gdn_w + 2 * GDN_HEADS]),
        (AB_RWKV0, w_in[:, gdn_w + 2 * GDN_HEADS:gdn_w + 2 * GDN_HEADS + 3 * RWKV_W]),
        (AB_LORA0, w_in[:, gdn_w + 2 * GDN_HEADS + 3 * RWKV_W:]),
    ]).astype(bf16)
    f3 = norm_matmul(h, g_pre, w_pad).reshape(b, t, AB_COLS)
    arow = jnp.zeros((1, LANES), f32).at[0, GDN_HEADS:2 * GDN_HEADS].set(a_log)
    dtrow = jnp.zeros((1, LANES), f32).at[0, GDN_HEADS:2 * GDN_HEADS].set(dt_bias)
    conv3 = conv.reshape(GDN_CONV, 3, GDN_HEADS * GDN_D).transpose(1, 0, 2)
    o_a = gdn_mixer(f3, conv3, arow, dtrow, gnorm.reshape(1, GDN_D))
    prep = rwkv_prep(f3, mu, w0, w2, a0, a2, g2, k_k, k_a)
    o_b = rwkv_scan(*prep, r_k, ln_w, ln_b)
    m = b * t
    n_a = GDN_HEADS * GDN_D
    return out_res_norm([o_a.reshape(m, n_a), o_b.reshape(m, RWKV_W)],
                        [w_out[:n_a].astype(bf16), w_out[n_a:].astype(bf16)], h, g_post)


def _layer1(h, b, t, positions, g_pre, w_in, w_out, g_post, pe_k, w1_k, w2_k, pe_v, w1_v, w2_v):
    qw = NSA_HEADS * NSA_HD
    kvw = 6 * NSA_GROUPS * NSA_HD
    gates = w_in[:, qw + kvw:].reshape(D_MODEL, NSA_GROUPS, NSA_HPG * 3)
    w_pad = _place(D_MODEL, NSA_COLS, [(0, w_in[:, :qw + kvw])]
                   + [(NSA_GATE0 + gi * LANES, gates[:, gi]) for gi in range(NSA_GROUPS)]).astype(bf16)
    f3 = norm_matmul(h, g_pre, w_pad).reshape(b, t, NSA_COLS)
    inv_freq = ROPE_THETA ** (-jnp.arange(ROPE_HALF, dtype=f32) * (2.0 / ROPE_DIM))
    lane = jnp.arange(LANES)
    freq_row = jnp.where(lane % NSA_HD < ROPE_DIM, inv_freq[lane % ROPE_HALF], 0.0).reshape(1, LANES).astype(f32)
    posf = positions.astype(f32)
    q, kc, vc, ks, vs, kw, vw = nsa_prep(f3, posf.reshape(b, t, 1), freq_row)
    nrow = t // CMP_STRIDE
    cpos = jnp.concatenate([posf[:, CMP_LEN - 1::CMP_STRIDE], posf[:, -1:]], axis=1).reshape(b, nrow, 1)
    flat = lambda a: a.reshape(b, NSA_GROUPS, nrow, CMP_STRIDE * NSA_HD)
    kcc, vcc = nsa_compress(flat(kc), flat(vc), pe_k.reshape(1, -1), w1_k.astype(bf16), w2_k.astype(bf16),
                            pe_v.reshape(1, -1), w1_v.astype(bf16), w2_v.astype(bf16), cpos, freq_row)
    n_sel = t // SEL_BLOCK
    c_start = jnp.arange(nrow) * CMP_STRIDE
    s_start = jnp.arange(n_sel) * SEL_BLOCK
    overlap = jnp.clip(jnp.minimum(c_start[:, None] + CMP_LEN, s_start[None, :] + SEL_BLOCK)
                       - jnp.maximum(c_start[:, None], s_start[None, :]), 0, None).astype(f32) / CMP_LEN
    o = nsa_attention(q, kcc, vcc, ks, vs, kw, vw, f3, overlap)
    return out_res_norm([o.reshape(b * t, qw)], [w_out.astype(bf16)], h, g_post)


def kernel(x, positions, norm_mix_pre, norm_mix_post, norm_ffn_pre, norm_ffn_post, w_ffn_up, w_ffn_down, ab_w_in,
           ab_w_out, gdn_conv, gdn_a_log, gdn_dt_bias, gdn_norm, rwkv_mu, rwkv_w0, rwkv_w2, rwkv_a0, rwkv_a2,
           rwkv_g2, rwkv_k_k, rwkv_k_a, rwkv_r_k, rwkv_ln_w, rwkv_ln_b, nsa_w_in, nsa_w_out, nsa_pe_k, nsa_w1_k,
           nsa_w2_k, nsa_pe_v, nsa_w1_v, nsa_w2_v):
    b, t, d = x.shape
    h = x.reshape(b * t, d)
    h = _layer0(h, b, t, norm_mix_pre[0], ab_w_in[0], ab_w_out[0], norm_mix_post[0], gdn_conv[0], gdn_a_log[0],
                gdn_dt_bias[0], gdn_norm[0], rwkv_mu[0], rwkv_w0[0], rwkv_w2[0], rwkv_a0[0], rwkv_a2[0],
                rwkv_g2[0], rwkv_k_k[0], rwkv_k_a[0], rwkv_r_k[0].reshape(-1), rwkv_ln_w[0], rwkv_ln_b[0])
    h = ffn(h, norm_ffn_pre[0], w_ffn_up[0].astype(bf16), w_ffn_down[0].astype(bf16), norm_ffn_post[0])
    h = _layer1(h, b, t, positions, norm_mix_pre[1], nsa_w_in[0], nsa_w_out[0], norm_mix_post[1], nsa_pe_k[0],
                nsa_w1_k[0], nsa_w2_k[0], nsa_pe_v[0], nsa_w1_v[0], nsa_w2_v[0])
    h = ffn(h, norm_ffn_pre[1], w_ffn_up[1].astype(bf16), w_ffn_down[1].astype(bf16), norm_ffn_post[1])
    return h.reshape(b, t, d)
```

```python
import functools
import math

import jax
import jax.numpy as jnp
from jax import lax
from jax.experimental import pallas as pl
from jax.experimental.pallas import tpu as pltpu

f32 = jnp.float32
bf16 = jnp.bfloat16
HI = lax.Precision.HIGHEST

V7X_VMEM_LIMIT_BYTES = 56 * 1024 * 1024
LANES = 128
SUBLANES = 8

D_MODEL = 1024
D_FF = 4 * D_MODEL
NORM_EPS = 1e-6
GDN_HEADS = 4
GDN_D = 128
GDN_CONV = 4
GDN_CHUNK = 128
RWKV_HEADS = 8
RWKV_N = 64
RWKV_W = RWKV_HEADS * RWKV_N
RWKV_CHUNK = 64
RWKV_LN_EPS = 64e-5
NSA_HEADS = 16
NSA_GROUPS = 2
NSA_HPG = NSA_HEADS // NSA_GROUPS
NSA_HD = 64
CMP_LEN = 32
CMP_STRIDE = 16
CMP_HIDDEN = 256
SEL_BLOCK = 64
SEL_TOPN = 16
WINDOW = 512
NSA_QBLOCK = 64
ROPE_THETA = 500000.0
ROPE_DIM = NSA_HD // 4
ROPE_HALF = ROPE_DIM // 2

AB_COLS = 4096
AB_RWKV0 = 2048
AB_LORA0 = 3584
AB_BA0 = 3840
NSA_COLS = 2048
NSA_KV0 = 1024
NSA_GATE0 = 1792


def _cparams(sem):
    return pltpu.CompilerParams(dimension_semantics=sem, vmem_limit_bytes=V7X_VMEM_LIMIT_BYTES)


def _rms(x, g):
    return x * lax.rsqrt(jnp.mean(x * x, axis=-1, keepdims=True) + NORM_EPS) * g


def _dot(a, b, precision=None):
    return jnp.dot(a, b, precision=precision, preferred_element_type=f32)


def _dot_nt(a, b, precision=None):
    return lax.dot_general(a, b, (((1,), (1,)), ((), ())), precision=precision, preferred_element_type=f32)


def _dot_tn(a, b, precision=None):
    return lax.dot_general(a, b, (((0,), (0,)), ((), ())), precision=precision, preferred_element_type=f32)


def _bdot(a, b):
    return _dot(a.astype(bf16), b.astype(bf16))


def _bdot_nt(a, b):
    return _dot_nt(a.astype(bf16), b.astype(bf16))


def _bdot_tn(a, b):
    return _dot_tn(a.astype(bf16), b.astype(bf16))


def _dot01(m01, x):
    m = m01.astype(bf16)
    hi = x.astype(bf16)
    rest = x - hi.astype(f32)
    mid = rest.astype(bf16)
    lo = (rest - mid.astype(f32)).astype(bf16)
    return _dot(m, hi) + _dot(m, mid) + _dot(m, lo)


def _iota2(shape, axis):
    return lax.broadcasted_iota(jnp.int32, shape, axis)


def _norm_matmul_body(x_ref, g_ref, w_ref, o_ref, u_ref):
    @pl.when(pl.program_id(1) == 0)
    def _():
        u_ref[...] = _rms(x_ref[...], g_ref[...]).astype(bf16)

    o_ref[...] = _dot(u_ref[...], w_ref[...])


def norm_matmul(x, g, w, tm=1024, tn=512):
    m, d = x.shape
    n = w.shape[1]
    tm = min(tm, m)
    return pl.pallas_call(
        _norm_matmul_body,
        grid=(m // tm, n // tn),
        in_specs=[pl.BlockSpec((tm, d), lambda i, j: (i, 0)),
                  pl.BlockSpec((1, d), lambda i, j: (0, 0)),
                  pl.BlockSpec((d, tn), lambda i, j: (0, j))],
        out_specs=pl.BlockSpec((tm, tn), lambda i, j: (i, j)),
        out_shape=jax.ShapeDtypeStruct((m, n), f32),
        scratch_shapes=[pltpu.VMEM((tm, d), bf16)],
        compiler_params=_cparams(("parallel", "arbitrary")),
        name="norm_matmul",
    )(x, g.reshape(1, d), w)


def _out_res_norm_body(*refs, n_parts):
    a_refs = refs[:n_parts]
    w_refs = refs[n_parts:2 * n_parts]
    h_ref, g_ref, o_ref = refs[2 * n_parts:]
    y = _dot(a_refs[0][...].astype(bf16), w_refs[0][...])
    for a_ref, w_ref in zip(a_refs[1:], w_refs[1:]):
        y = y + _dot(a_ref[...].astype(bf16), w_ref[...])
    o_ref[...] = h_ref[...] + _rms(y, g_ref[...])


def out_res_norm(parts, w_parts, h, g, tm=512):
    m, d = h.shape
    tm = min(tm, m)
    n_parts = len(parts)
    in_specs = ([pl.BlockSpec((tm, p.shape[1]), lambda i: (i, 0)) for p in parts]
                + [pl.BlockSpec(w.shape, lambda i: (0, 0)) for w in w_parts]
                + [pl.BlockSpec((tm, d), lambda i: (i, 0)), pl.BlockSpec((1, d), lambda i: (0, 0))])
    return pl.pallas_call(
        functools.partial(_out_res_norm_body, n_parts=n_parts),
        grid=(m // tm,),
        in_specs=in_specs,
        out_specs=pl.BlockSpec((tm, d), lambda i: (i, 0)),
        out_shape=jax.ShapeDtypeStruct((m, d), f32),
        compiler_params=_cparams(("parallel",)),
        name="out_res_norm",
    )(*parts, *w_parts, h, g.reshape(1, d))


def _ffn_body(h_ref, g1_ref, wup_ref, wdn_ref, g2_ref, o_ref, u_ref, acc_ref):
    k = pl.program_id(1)

    @pl.when(k == 0)
    def _():
        u_ref[...] = _rms(h_ref[...], g1_ref[...]).astype(bf16)
        acc_ref[...] = jnp.zeros_like(acc_ref)

    a = _dot(u_ref[...], wup_ref[...])
    a = jnp.square(jnp.maximum(a, 0.0))
    acc_ref[...] += _dot(a.astype(bf16), wdn_ref[...])

    @pl.when(k == pl.num_programs(1) - 1)
    def _():
        o_ref[...] = h_ref[...] + _rms(acc_ref[...], g2_ref[...])


def ffn(h, g1, wup, wdn, g2, tm=1024, tf=512):
    m, d = h.shape
    ff = wup.shape[1]
    tm = min(tm, m)
    return pl.pallas_call(
        _ffn_body,
        grid=(m // tm, ff // tf),
        in_specs=[pl.BlockSpec((tm, d), lambda i, k: (i, 0)),
                  pl.BlockSpec((1, d), lambda i, k: (0, 0)),
                  pl.BlockSpec((d, tf), lambda i, k: (0, k)),
                  pl.BlockSpec((tf, d), lambda i, k: (k, 0)),
                  pl.BlockSpec((1, d), lambda i, k: (0, 0))],
        out_specs=pl.BlockSpec((tm, d), lambda i, k: (i, 0)),
        out_shape=jax.ShapeDtypeStruct((m, d), f32),
        scratch_shapes=[pltpu.VMEM((tm, d), bf16), pltpu.VMEM((tm, d), f32)],
        compiler_params=_cparams(("parallel", "arbitrary")),
        name="ffn",
    )(h, g1.reshape(1, d), wup, wdn, g2.reshape(1, d))


def _neumann_inverses(n_mats, size):
    eye = (_iota2((size, size), 0) == _iota2((size, size), 1)).astype(f32)
    ts = [eye + n for n in n_mats]
    ps = list(n_mats)
    for _ in range(int(math.log2(size)) - 1):
        ps = [_bdot(p, p) for p in ps]
        ts = [t + _bdot(t, p) for t, p in zip(ts, ps)]
    return ts


def _gdn_body(q_ref, k_ref, v_ref, z_ref, ba_ref, cw_ref, arow_ref, dtrow_ref, nw_ref, o_ref,
              xp_ref, qkv_ref, s_ref, *, tt):
    c = GDN_CHUNK
    d = GDN_D
    heads = range(GDN_HEADS)
    width = GDN_HEADS * d

    @pl.when(pl.program_id(1) == 0)
    def _():
        xp_ref[:, pl.ds(0, SUBLANES), :] = jnp.zeros((3, SUBLANES, width), f32)
        s_ref[...] = jnp.zeros_like(s_ref)

    for idx, ref in enumerate((q_ref, k_ref, v_ref)):
        xp_ref[idx, pl.ds(SUBLANES, tt), :] = ref[0]
        w = cw_ref[idx]
        y = xp_ref[idx, pl.ds(SUBLANES, tt), :] * w[GDN_CONV - 1:GDN_CONV, :]
        for j in range(GDN_CONV - 1):
            y = y + xp_ref[idx, pl.ds(SUBLANES - (GDN_CONV - 1) + j, tt), :] * w[j:j + 1, :]
        y = y * jax.nn.sigmoid(y)
        for h in heads:
            yh = y[:, h * d:(h + 1) * d]
            if idx < 2:
                yh = yh * lax.rsqrt(jnp.sum(yh * yh, axis=-1, keepdims=True) + 1e-6)
            if idx == 0:
                yh = yh * (d ** -0.5)
            qkv_ref[idx, :, h * d:(h + 1) * d] = yh
        xp_ref[idx, pl.ds(0, SUBLANES), :] = xp_ref[idx, pl.ds(tt, SUBLANES), :]

    row = _iota2((c, c), 0)
    col = _iota2((c, c), 1)
    tril = row >= col
    strict = row > col
    eye = row == col
    cum_l = tril.astype(f32)
    last_row = _iota2((c, 1), 0) == c - 1

    def chunk(ci, carry):
        off = pl.multiple_of(ci * c, c)
        rows = pl.ds(off, c)
        ba = ba_ref[0, rows, :]
        sig = jax.nn.sigmoid(ba)
        gcum = _dot01(cum_l, -jnp.exp(arow_ref[...]) * jax.nn.softplus(ba + dtrow_ref[...]))
        qn = [qkv_ref[0, rows, h * d:(h + 1) * d] for h in heads]
        kn = [qkv_ref[1, rows, h * d:(h + 1) * d] for h in heads]
        vv = [qkv_ref[2, rows, h * d:(h + 1) * d] for h in heads]
        beta = [sig[:, h:h + 1] for h in heads]
        gc = [gcum[:, GDN_HEADS + h:GDN_HEADS + h + 1] for h in heads]
        gc_row = [jnp.sum(jnp.where(eye, jnp.broadcast_to(g, (c, c)), 0.0), axis=0, keepdims=True) for g in gc]
        gc_last = [jnp.sum(jnp.where(last_row, g, 0.0), axis=0, keepdims=True) for g in gc]
        decay = [jnp.exp(jnp.where(tril, g - gr, -jnp.inf)) for g, gr in zip(gc, gc_row)]
        knb = [k.astype(bf16) for k in kn]
        kk = [_dot_nt(k, k) for k in knb]
        qk = [_dot_nt(q.astype(bf16), k) for q, k in zip(qn, knb)]
        t_inv = _neumann_inverses([-jnp.where(strict, b * x * dc, 0.0) for b, x, dc in zip(beta, kk, decay)], c)
        egc = [jnp.exp(g) for g in gc]
        u = [_bdot(t, v * b) for t, v, b in zip(t_inv, vv, beta)]
        w = [_bdot(t, k * (b * e)) for t, k, b, e in zip(t_inv, kn, beta, egc)]
        intra = [jnp.where(tril, x * dc, 0.0).astype(bf16) for x, dc in zip(qk, decay)]
        q_g = [(q * e).astype(bf16) for q, e in zip(qn, egc)]
        k_g = [(k * jnp.exp(gl - g)).astype(bf16) for k, gl, g in zip(kn, gc_last, gc)]
        s = [s_ref[h] for h in heads]
        sb = [x.astype(bf16) for x in s]
        v_new = [(x - _dot(y.astype(bf16), z)).astype(bf16) for x, y, z in zip(u, w, sb)]
        o = [_dot(q, z) + _dot(a, vn) for q, z, a, vn in zip(q_g, sb, intra, v_new)]
        for h in heads:
            s_ref[h] = s[h] * jnp.exp(gc_last[h]) + _dot_tn(k_g[h], v_new[h])
            z = z_ref[0, rows, h * d:(h + 1) * d]
            o_ref[0, rows, h * d:(h + 1) * d] = _rms(o[h], nw_ref[...]) * (z * jax.nn.sigmoid(z))
        return carry

    lax.fori_loop(0, tt // c, chunk, 0)


def gdn_mixer(f3, conv_w, arow, dtrow, norm_w, tt=512):
    b, t, _ = f3.shape
    tt = min(tt, t)
    width = GDN_HEADS * GDN_D
    col = lambda j: pl.BlockSpec((1, tt, width), lambda bi, ti, j=j: (bi, ti, j))
    return pl.pallas_call(
        functools.partial(_gdn_body, tt=tt),
        grid=(b, t // tt),
        in_specs=[col(0), col(1), col(2), col(3),
                  pl.BlockSpec((1, tt, LANES), lambda bi, ti: (bi, ti, AB_BA0 // LANES)),
                  pl.BlockSpec((3, GDN_CONV, width), lambda bi, ti: (0, 0, 0)),
                  pl.BlockSpec((1, LANES), lambda bi, ti: (0, 0)),
                  pl.BlockSpec((1, LANES), lambda bi, ti: (0, 0)),
                  pl.BlockSpec((1, GDN_D), lambda bi, ti: (0, 0))],
        out_specs=pl.BlockSpec((1, tt, width), lambda bi, ti: (bi, ti, 0)),
        out_shape=jax.ShapeDtypeStruct((b, t, width), f32),
        scratch_shapes=[pltpu.VMEM((3, tt + SUBLANES, width), f32),
                        pltpu.VMEM((3, tt, width), f32),
                        pltpu.VMEM((GDN_HEADS, GDN_D, GDN_D), f32)],
        compiler_params=_cparams(("parallel", "arbitrary")),
        name="gdn",
    )(f3, f3, f3, f3, f3, conv_w, arow, dtrow, norm_w)


def _rwkv_prep_body(r_ref, k_ref, v_ref, l_ref, mu_ref, w0_ref, w2_ref, a0_ref, a2_ref, g2_ref, kk_ref, ka_ref,
                    ro_ref, lw_ref, k2_ref, vo_ref, kko_ref, ao_ref, go_ref, xp_ref, *, tt):
    @pl.when(pl.program_id(1) == 0)
    def _():
        xp_ref[pl.ds(0, 8), :] = jnp.zeros((8, xp_ref.shape[1]), f32)

    w = RWKV_W
    xp_ref[pl.ds(8, tt), 0:w] = r_ref[0]
    xp_ref[pl.ds(8, tt), w:2 * w] = k_ref[0]
    xp_ref[pl.ds(8, tt), 2 * w:3 * w] = v_ref[0]
    xp_ref[pl.ds(8, tt), 3 * w:] = l_ref[0]
    x = xp_ref[pl.ds(8, tt), :]
    x = x + (xp_ref[pl.ds(7, tt), :] - x) * mu_ref[...]
    xp_ref[pl.ds(0, 8), :] = xp_ref[pl.ds(tt, 8), :]

    r, k, v = x[:, 0:w], x[:, w:2 * w], x[:, 2 * w:3 * w]
    wd, ad, gd = x[:, 3 * w:3 * w + 64], x[:, 3 * w + 64:3 * w + 128], x[:, 3 * w + 128:]
    w_log = -jax.nn.softplus(-(w0_ref[...] + _dot(jnp.tanh(wd), w2_ref[...], HI))) - 0.5
    a = jax.nn.sigmoid(a0_ref[...] + _dot(ad, a2_ref[...], HI))
    ro_ref[0] = r
    lw_ref[0] = -jnp.exp(w_log)
    k2_ref[0] = k * (1.0 + (a - 1.0) * ka_ref[...])
    vo_ref[0] = v
    kko_ref[0] = k * kk_ref[...]
    ao_ref[0] = a
    go_ref[0] = _dot(jax.nn.sigmoid(gd), g2_ref[...], HI)


def rwkv_prep(f3, mu, w0, w2, a0, a2, g2, k_k, k_a, tt=256):
    b, t, _ = f3.shape
    tt = min(tt, t)
    w = RWKV_W
    wide = 3 * w + 256
    row = lambda n: pl.BlockSpec((1, n), lambda bi, ti: (0, 0))
    full = lambda a: pl.BlockSpec(a.shape, lambda bi, ti: (0, 0))
    out = pl.BlockSpec((1, tt, w), lambda bi, ti: (bi, ti, 0))
    return pl.pallas_call(
        functools.partial(_rwkv_prep_body, tt=tt),
        grid=(b, t // tt),
        in_specs=[pl.BlockSpec((1, tt, w), lambda bi, ti: (bi, ti, AB_RWKV0 // w)),
                  pl.BlockSpec((1, tt, w), lambda bi, ti: (bi, ti, AB_RWKV0 // w + 1)),
                  pl.BlockSpec((1, tt, w), lambda bi, ti: (bi, ti, AB_RWKV0 // w + 2)),
                  pl.BlockSpec((1, tt, 256), lambda bi, ti: (bi, ti, AB_LORA0 // 256)),
                  row(wide), row(w), full(w2), row(w), full(a2), full(g2), row(w), row(w)],
        out_specs=[out] * 7,
        out_shape=[jax.ShapeDtypeStruct((b, t, w), f32)] * 7,
        scratch_shapes=[pltpu.VMEM((tt + 8, wide), f32)],
        compiler_params=_cparams(("parallel", "arbitrary")),
        name="rwkv_prep",
    )(f3, f3, f3, f3, mu.reshape(1, wide), w0.reshape(1, w), w2, a0.reshape(1, w), a2, g2,
      k_k.reshape(1, w), k_a.reshape(1, w))


def _rwkv_scan_body(r_ref, lw_ref, k2_ref, v_ref, kk_ref, a_ref, g_ref, rk_ref, lnw_ref, lnb_ref, o_ref, s_ref,
                    *, tt):
    c = RWKV_CHUNK
    n = RWKV_N

    @pl.when(pl.program_id(1) == 0)
    def _():
        s_ref[...] = jnp.zeros_like(s_ref)

    row = _iota2((c, c), 0)
    col = _iota2((c, c), 1)
    tril = row >= col
    strict = row > col
    cum_l = tril.astype(f32)
    last = _iota2((c, 1), 0) == c - 1

    heads = range(RWKV_HEADS)
    per_head = lambda x: [x[:, h * n:(h + 1) * n] for h in heads]

    def chunk(ci, carry):
        rows = pl.ds(pl.multiple_of(ci * c, c), c)
        lw_all = lw_ref[0, rows, :]
        p_all = _dot01(cum_l, lw_all)
        em_all = jnp.exp(-p_all)
        r_all, k2_all, v_all = r_ref[0, rows, :], k2_ref[0, rows, :], v_ref[0, rows, :]
        rh, k2, vh = per_head(r_all), per_head(k2_all), per_head(v_all)
        r_t = per_head(r_all * jnp.exp(p_all))
        k_t = per_head(k2_all * em_all)
        e_prev = per_head(jnp.exp(p_all - lw_all))
        lr_em = per_head(a_ref[0, rows, :] * em_all)
        kk = [x * lax.rsqrt(jnp.sum(x * x, axis=-1, keepdims=True) + 1e-6) for x in per_head(kk_ref[0, rows, :])]
        a_t = [-x * e for x, e in zip(kk, e_prev)]
        b_t = [x * e for x, e in zip(kk, lr_em)]
        m_ab = [jnp.where(strict, _bdot_nt(a, b), 0.0) for a, b in zip(a_t, b_t)]
        m_ak = [jnp.where(strict, _bdot_nt(a, k), 0.0) for a, k in zip(a_t, k_t)]
        a_rb = [jnp.where(tril, _bdot_nt(r, b), 0.0) for r, b in zip(r_t, b_t)]
        a_rk = [jnp.where(tril, _bdot_nt(r, k), 0.0) for r, k in zip(r_t, k_t)]
        t_inv = _neumann_inverses(m_ab, c)
        w1 = [_bdot(t, a) for t, a in zip(t_inv, a_t)]
        mv = [_bdot(m, v) for m, v in zip(m_ak, vh)]
        u2 = [_bdot(t, x) for t, x in zip(t_inv, mv)]
        y_v = [_bdot(a, v) for a, v in zip(a_rk, vh)]
        ds_v = [_bdot_tn(v, k) for v, k in zip(vh, k_t)]
        s0 = [s_ref[h] for h in heads]
        u = [_bdot_nt(w, s) + x for w, s, x in zip(w1, s0, u2)]
        y_s = [_bdot_nt(r, s) for r, s in zip(r_t, s0)]
        y = [ys + _bdot(a, x) + yv for ys, a, x, yv in zip(y_s, a_rb, u, y_v)]
        ds_u = [_bdot_tn(x, b) for x, b in zip(u, b_t)]
        decay_last = per_head(jnp.exp(jnp.sum(jnp.where(last, p_all, 0.0), axis=0, keepdims=True)))
        g_h = per_head(g_ref[0, rows, :])
        for h in heads:
            sl = slice(h * n, (h + 1) * n)
            s_ref[h] = (s0[h] + ds_u[h] + ds_v[h]) * decay_last[h]
            mean = jnp.mean(y[h], axis=-1, keepdims=True)
            yc = y[h] - mean
            var = jnp.mean(yc * yc, axis=-1, keepdims=True)
            yn = yc * lax.rsqrt(var + RWKV_LN_EPS) * lnw_ref[:, sl] + lnb_ref[:, sl]
            bonus = jnp.sum(rh[h] * k2[h] * rk_ref[:, sl], axis=-1, keepdims=True) * vh[h]
            o_ref[0, rows, sl] = (yn + bonus) * g_h[h]
        return carry

    lax.fori_loop(0, tt // c, chunk, 0)


def rwkv_scan(r, lw, k2, v, kk, a, g, r_k, ln_w, ln_b, tt=256):
    b, t, w = r.shape
    tt = min(tt, t)
    blk = pl.BlockSpec((1, tt, w), lambda bi, ti: (bi, ti, 0))
    row = pl.BlockSpec((1, w), lambda bi, ti: (0, 0))
    return pl.pallas_call(
        functools.partial(_rwkv_scan_body, tt=tt),
        grid=(b, t // tt),
        in_specs=[blk] * 7 + [row] * 3,
        out_specs=blk,
        out_shape=jax.ShapeDtypeStruct((b, t, w), f32),
        scratch_shapes=[pltpu.VMEM((RWKV_HEADS, RWKV_N, RWKV_N), f32)],
        compiler_params=_cparams(("parallel", "arbitrary")),
        name="rwkv_scan",
    )(r, lw, k2, v, kk, a, g, r_k.reshape(1, w), ln_w.reshape(1, w), ln_b.reshape(1, w))


def _rope128(x, cos, sin, sign_lo, sign_hi):
    r_hi = pltpu.roll(x, ROPE_HALF, 1)
    r_lo = pltpu.roll(x, LANES - ROPE_HALF, 1)
    return x * cos + (r_lo * sign_lo + r_hi * sign_hi) * sin


def _rope_tables(pos, freq_row):
    ang = pos * freq_row
    m = _iota2((1, LANES), 1) % NSA_HD
    sign_lo = jnp.where(m < ROPE_HALF, -1.0, 0.0).astype(f32)
    sign_hi = jnp.where((m >= ROPE_HALF) & (m < ROPE_DIM), 1.0, 0.0).astype(f32)
    return jnp.cos(ang), jnp.sin(ang), sign_lo, sign_hi


def _nsa_prep_body(q_ref, kc_i, vc_i, ks_i, vs_i, kw_i, vw_i, pos_ref, freq_ref,
                   qo_ref, kc_ref, vc_ref, ks_ref, vs_ref, kw_ref, vw_ref):
    cos, sin, s_lo, s_hi = _rope_tables(pos_ref[0], freq_ref[...])
    scale = NSA_HD ** -0.5 * math.log2(math.e)
    n_blk = q_ref.shape[1] // NSA_QBLOCK
    for c in range(NSA_HEADS // 2):
        xt = (_rope128(q_ref[0, :, c * LANES:(c + 1) * LANES], cos, sin, s_lo, s_hi) * scale).T
        for r in range(2):
            head = 2 * c + r
            g, hl = head // NSA_HPG, head % NSA_HPG
            for i in range(n_blk):
                qo_ref[0, g, i, :, hl * NSA_QBLOCK:(hl + 1) * NSA_QBLOCK] = (
                    xt[r * NSA_HD:(r + 1) * NSA_HD, i * NSA_QBLOCK:(i + 1) * NSA_QBLOCK].astype(bf16))

    def split(src, ref, rope, dtype):
        x = src[0]
        if rope:
            x = _rope128(x, cos, sin, s_lo, s_hi)
        ref[0, 0] = x[:, :NSA_HD].astype(dtype)
        ref[0, 1] = x[:, NSA_HD:].astype(dtype)

    def split_t(src, ref):
        xt = src[0].T
        ref[0, 0] = xt[:NSA_HD].astype(bf16)
        ref[0, 1] = xt[NSA_HD:].astype(bf16)

    split(kc_i, kc_ref, False, f32)
    split(vc_i, vc_ref, False, f32)
    split(ks_i, ks_ref, True, bf16)
    split_t(vs_i, vs_ref)
    split(kw_i, kw_ref, True, bf16)
    split_t(vw_i, vw_ref)


def nsa_prep(f3, pos3, freq_row, tt=512):
    b, t, _ = f3.shape
    tt = min(tt, t)
    g = NSA_GROUPS
    n_blk = tt // NSA_QBLOCK
    kv_in = [pl.BlockSpec((1, tt, LANES), lambda bi, ti, c=NSA_KV0 // LANES + i: (bi, ti, c)) for i in range(6)]
    kv_spec = pl.BlockSpec((1, g, tt, NSA_HD), lambda bi, ti: (bi, 0, ti, 0))
    kvt_spec = pl.BlockSpec((1, g, NSA_HD, tt), lambda bi, ti: (bi, 0, 0, ti))
    kv32 = jax.ShapeDtypeStruct((b, g, t, NSA_HD), f32)
    kv16 = jax.ShapeDtypeStruct((b, g, t, NSA_HD), bf16)
    kvt16 = jax.ShapeDtypeStruct((b, g, NSA_HD, t), bf16)
    q_lanes = NSA_HPG * NSA_QBLOCK
    return pl.pallas_call(
        _nsa_prep_body,
        grid=(b, t // tt),
        in_specs=[pl.BlockSpec((1, tt, NSA_HEADS * NSA_HD), lambda bi, ti: (bi, ti, 0))] + kv_in
                 + [pl.BlockSpec((1, tt, 1), lambda bi, ti: (bi, ti, 0)),
                    pl.BlockSpec((1, LANES), lambda bi, ti: (0, 0))],
        out_specs=[pl.BlockSpec((1, g, n_blk, NSA_HD, q_lanes), lambda bi, ti: (bi, 0, ti, 0, 0)),
                   kv_spec, kv_spec, kv_spec, kvt_spec, kv_spec, kvt_spec],
        out_shape=[jax.ShapeDtypeStruct((b, g, t // NSA_QBLOCK, NSA_HD, q_lanes), bf16),
                   kv32, kv32, kv16, kvt16, kv16, kvt16],
        compiler_params=_cparams(("parallel", "parallel")),
        name="nsa_prep",
    )(f3, f3, f3, f3, f3, f3, f3, pos3, freq_row)


def _nsa_compress_body(kc_ref, vc_ref, pek_ref, w1k_ref, w2k_ref, pev_ref, w1v_ref, w2v_ref, pos_ref, freq_ref,
                       ko_ref, vo_ref):
    half = CMP_STRIDE * NSA_HD
    nrow = kc_ref.shape[2]
    last_row = _iota2((nrow, 1), 0) == nrow - 1

    def hidden(x, pe_ref, w1_ref):
        lo = _dot((x + pe_ref[:, :half]).astype(bf16), w1_ref[:half, :])
        hi = _dot((x + pe_ref[:, half:]).astype(bf16), w1_ref[half:, :])
        hi = jnp.where(last_row, 0.0, pltpu.roll(hi, nrow - 1, 0))
        hid = lo + hi
        return (hid * jax.nn.sigmoid(hid)).astype(bf16)

    ks = [_dot(hidden(kc_ref[0, g], pek_ref, w1k_ref), w2k_ref[...]) for g in range(NSA_GROUPS)]
    cos, sin, s_lo, s_hi = _rope_tables(pos_ref[0], freq_ref[...])
    kr = _rope128(jnp.concatenate(ks, axis=-1), cos, sin, s_lo, s_hi)
    for g in range(NSA_GROUPS):
        ko_ref[0, g] = kr[:, g * NSA_HD:(g + 1) * NSA_HD].astype(bf16)
        vo_ref[0, g] = _dot_nt(w2v_ref[...], hidden(vc_ref[0, g], pev_ref, w1v_ref)).astype(bf16)


def nsa_compress(kc4, vc4, pe_k, w1_k, w2_k, pe_v, w1_v, w2_vt, cpos3, freq_row):
    b, g, nrow, wide = kc4.shape
    full = lambda a: pl.BlockSpec(a.shape, lambda bi: (0,) * a.ndim)
    blk = pl.BlockSpec((1, g, nrow, wide), lambda bi: (bi, 0, 0, 0))
    out = pl.BlockSpec((1, g, nrow, NSA_HD), lambda bi: (bi, 0, 0, 0))
    out_t = pl.BlockSpec((1, g, NSA_HD, nrow), lambda bi: (bi, 0, 0, 0))
    return pl.pallas_call(
        _nsa_compress_body,
        grid=(b,),
        in_specs=[blk, blk, full(pe_k), full(w1_k), full(w2_k), full(pe_v), full(w1_v), full(w2_vt),
                  pl.BlockSpec((1, nrow, 1), lambda bi: (bi, 0, 0)), pl.BlockSpec((1, LANES), lambda bi: (0, 0))],
        out_specs=[out, out_t],
        out_shape=[jax.ShapeDtypeStruct((b, g, nrow, NSA_HD), bf16),
                   jax.ShapeDtypeStruct((b, g, NSA_HD, nrow), bf16)],
        compiler_params=_cparams(("parallel",)),
        name="nsa_compress",
    )(kc4, vc4, pe_k, w1_k, w2_k, pe_v, w1_v, w2_vt, cpos3, freq_row)


def _nsa_attn_body(qt_ref, kc_ref, vct_ref, ks_ref, vst_ref, kw_ref, vwt_ref, gate_ref, ovt_ref, o_ref,
                   score_ref, sel_ref, s_ref, *, key_tile):
    qb_n = NSA_QBLOCK
    lanes = NSA_HPG * qb_n
    i = pl.program_id(2)
    s0 = i * qb_n
    qt = qt_ref[0, 0, 0]
    q_lane = _iota2((1, lanes), 1) % qb_n
    t_lane = s0 + q_lane

    def softmax_t(s):
        mx = jnp.max(s, axis=0, keepdims=True)
        mx = jnp.where(mx > -jnp.inf, mx, 0.0)
        e = jnp.exp2(s - mx)
        return e, jnp.sum(e, axis=0, keepdims=True), mx

    diag0 = (i // 2) * 2
    blocks_per_tile = key_tile // SEL_BLOCK
    n_tiles = (diag0 * SEL_BLOCK + key_tile - 1) // key_tile

    def score_tile(kt):
        kk0 = pl.multiple_of(kt * key_tile, key_tile)
        return _dot(ks_ref[0, 0, pl.ds(kk0, key_tile), :], qt)

    s_ref[0] = score_tile(0)

    n_cmp = kc_ref.shape[2]
    cmp_end = _iota2((n_cmp, 1), 0) * CMP_STRIDE + (CMP_LEN - 1)
    e_c, den_c, _ = softmax_t(jnp.where(cmp_end <= t_lane, _dot(kc_ref[0, 0], qt), -jnp.inf))
    p_c = e_c * (1.0 / jnp.maximum(den_c, 1e-30))
    o_c = _dot(vct_ref[0, 0], p_c.astype(bf16))

    n_sel = ovt_ref.shape[0]
    p_heads = p_c[:, 0:LANES]
    for c in range(1, lanes // LANES):
        p_heads = p_heads + p_c[:, c * LANES:(c + 1) * LANES]
    imp = _dot(ovt_ref[...], p_heads[:, :qb_n] + p_heads[:, qb_n:], HI)
    blk = _iota2((n_sel, qb_n), 0)
    valid = blk <= i
    forced = (blk == 0) | (blk == i) | (blk == i - 1)
    score = jnp.where(valid, jnp.where(forced, jnp.inf, imp), -jnp.inf)
    score_ref[...] = score

    span = WINDOW + 2 * qb_n
    w0 = pl.multiple_of((jnp.maximum(s0 - WINDOW, 0) // LANES) * LANES, LANES)
    kp = w0 + _iota2((span, 1), 0)
    e_w, l_w, _ = softmax_t(jnp.where((kp <= t_lane) & (kp > t_lane - WINDOW),
                                      _dot(kw_ref[0, 0, pl.ds(w0, span), :], qt), -jnp.inf))
    o_w = _dot(vwt_ref[0, 0, :, pl.ds(w0, span)], e_w.astype(bf16))

    k0 = pl.multiple_of(diag0 * SEL_BLOCK, 2 * SEL_BLOCK)
    key_pos = k0 + _iota2((2 * SEL_BLOCK, 1), 0)
    e_d, l_s, m_s = softmax_t(jnp.where(key_pos <= t_lane,
                                        _dot(ks_ref[0, 0, pl.ds(k0, 2 * SEL_BLOCK), :], qt), -jnp.inf))
    acc_s = _dot(vst_ref[0, 0, :, pl.ds(k0, 2 * SEL_BLOCK)], e_d.astype(bf16))

    def rank_step(jp, cnt):
        other = score_ref[pl.ds(jp, 1), :]
        ahead = (other > score) | ((other == score) & (blk > jp))
        return cnt + ahead.astype(jnp.int32)

    cnt = lax.fori_loop(0, jnp.where(i >= SEL_TOPN, i + 1, 0), rank_step, jnp.zeros((n_sel, qb_n), jnp.int32))
    sel = ((cnt < SEL_TOPN) & (blk < diag0)).astype(f32)
    sel_ref[...] = jnp.concatenate([sel] * NSA_HPG, axis=1)

    def half_step(kt, slot, carry):
        m_old, l_old, acc = carry
        s_ref[1 - slot] = score_tile(jnp.minimum(kt + 1, n_tiles - 1))
        live = kt < n_tiles
        kt = jnp.minimum(kt, n_tiles - 1)
        s = s_ref[slot]
        parts = []
        for bi in range(blocks_per_tile):
            picked = (sel_ref[pl.ds(kt * blocks_per_tile + bi, 1), :] > 0.5) & live
            parts.append(jnp.where(picked, s[bi * SEL_BLOCK:(bi + 1) * SEL_BLOCK], -jnp.inf))
        sm = jnp.concatenate(parts, axis=0)
        mx = jnp.maximum(m_old, jnp.max(sm, axis=0, keepdims=True))
        alpha = jnp.exp2(m_old - mx)
        p = jnp.exp2(sm - mx)
        kk0 = pl.multiple_of(kt * key_tile, key_tile)
        pv = _dot(vst_ref[0, 0, :, pl.ds(kk0, key_tile)], p.astype(bf16))
        return mx, alpha * l_old + jnp.sum(p, axis=0, keepdims=True), alpha * acc + pv

    def pair_step(pi, carry):
        return half_step(2 * pi + 1, 1, half_step(2 * pi, 0, carry))

    m_s, l_s, acc_s = lax.fori_loop(0, (n_tiles + 1) // 2, pair_step, (m_s, l_s, acc_s))

    gt = jax.nn.sigmoid(gate_ref[0]).T
    gate = lambda br: jnp.concatenate([gt[3 * h + br:3 * h + br + 1, :] for h in range(NSA_HPG)], axis=1)
    o_t = gate(0) * o_c + (gate(1) * (1.0 / l_s)) * acc_s + (gate(2) * (1.0 / l_w)) * o_w
    for h in range(NSA_HPG):
        o_ref[0, :, h * NSA_HD:(h + 1) * NSA_HD] = o_t[:, h * qb_n:(h + 1) * qb_n].T


def nsa_attention(qt, kc, vct, ks, vst, kw, vwt, f3, overlap_t, key_tile=256):
    b, g, n_q, d, lanes = qt.shape
    t = n_q * NSA_QBLOCK
    key_tile = min(key_tile, t)
    n_cmp = kc.shape[2]
    n_sel = overlap_t.shape[0]
    seq = pl.BlockSpec((1, 1, t, d), lambda bi, gi, qi: (bi, gi, 0, 0))
    seq_t = pl.BlockSpec((1, 1, d, t), lambda bi, gi, qi: (bi, gi, 0, 0))
    return pl.pallas_call(
        functools.partial(_nsa_attn_body, key_tile=key_tile),
        grid=(b, g, n_q),
        in_specs=[pl.BlockSpec((1, 1, 1, d, lanes), lambda bi, gi, qi: (bi, gi, qi, 0, 0)),
                  pl.BlockSpec((1, 1, n_cmp, d), lambda bi, gi, qi: (bi, gi, 0, 0)),
                  pl.BlockSpec((1, 1, d, n_cmp), lambda bi, gi, qi: (bi, gi, 0, 0)),
                  seq, seq_t, seq, seq_t,
                  pl.BlockSpec((1, NSA_QBLOCK, LANES), lambda bi, gi, qi: (bi, qi, NSA_GATE0 // LANES + gi)),
                  pl.BlockSpec(overlap_t.shape, lambda bi, gi, qi: (0, 0))],
        out_specs=pl.BlockSpec((1, NSA_QBLOCK, NSA_HPG * d), lambda bi, gi, qi: (bi, qi, gi)),
        out_shape=jax.ShapeDtypeStruct((b, t, g * NSA_HPG * d), f32),
        scratch_shapes=[pltpu.VMEM((n_sel, NSA_QBLOCK), f32), pltpu.VMEM((n_sel, lanes), f32),
                        pltpu.VMEM((2, key_tile, lanes), f32)],
        compiler_params=_cparams(("parallel", "parallel", "arbitrary")),
        name="nsa_attn",
    )(qt, kc, vct, ks, vst, kw, vwt, f3, overlap_t)


def _place(cols, total, pieces):
    out = jnp.zeros((cols, total), f32)
    for start, mat in pieces:
        out = lax.dynamic_update_slice(out, mat.astype(f32), (0, start))
    return out


def _layer0(h, b, t, g_pre, w_in, w_out, g_post, conv, a_log, dt_bias, gnorm, mu, w0, w2, a0, a2, g2, k_k, k_a,
            r_k, ln_w, ln_b):
    gdn_w = 4 * GDN_HEADS * GDN_D
    w_pad = _place(D_MODEL, AB_COLS, [
        (0, w_in[:, :gdn_w]),
        (AB_BA0, w_in[:, gdn_w:gdn_w + 2 * GDN_HEADS]),
        (AB_RWKV0, w_in[:, gdn_w + 2 * GDN_HEADS:gdn_w + 2 * GDN_HEADS + 3 * RWKV_W]),
        (AB_LORA0, w_in[:, gdn_w + 2 * GDN_HEADS + 3 * RWKV_W:]),
    ]).astype(bf16)
    f3 = norm_matmul(h, g_pre, w_pad).reshape(b, t, AB_COLS)
    arow = jnp.zeros((1, LANES), f32).at[0, GDN_HEADS:2 * GDN_HEADS].set(a_log)
    dtrow = jnp.zeros((1, LANES), f32).at[0, GDN_HEADS:2 * GDN_HEADS].set(dt_bias)
    conv3 = conv.reshape(GDN_CONV, 3, GDN_HEADS * GDN_D).transpose(1, 0, 2)
    o_a = gdn_mixer(f3, conv3, arow, dtrow, gnorm.reshape(1, GDN_D))
    prep = rwkv_prep(f3, mu, w0, w2, a0, a2, g2, k_k, k_a)
    o_b = rwkv_scan(*prep, r_k, ln_w, ln_b)
    m = b * t
    n_a = GDN_HEADS * GDN_D
    return out_res_norm([o_a.reshape(m, n_a), o_b.reshape(m, RWKV_W)],
                        [w_out[:n_a].astype(bf16), w_out[n_a:].astype(bf16)], h, g_post)


def _layer1(h, b, t, positions, g_pre, w_in, w_out, g_post, pe_k, w1_k, w2_k, pe_v, w1_v, w2_v):
    qw = NSA_HEADS * NSA_HD
    kvw = 6 * NSA_GROUPS * NSA_HD
    gates = w_in[:, qw + kvw:].reshape(D_MODEL, NSA_GROUPS, NSA_HPG * 3)
    w_pad = _place(D_MODEL, NSA_COLS, [(0, w_in[:, :qw + kvw])]
                   + [(NSA_GATE0 + gi * LANES, gates[:, gi]) for gi in range(NSA_GROUPS)]).astype(bf16)
    f3 = norm_matmul(h, g_pre, w_pad).reshape(b, t, NSA_COLS)
    inv_freq = ROPE_THETA ** (-jnp.arange(ROPE_HALF, dtype=f32) * (2.0 / ROPE_DIM))
    lane = jnp.arange(LANES)
    freq_row = jnp.where(lane % NSA_HD < ROPE_DIM, inv_freq[lane % ROPE_HALF], 0.0).reshape(1, LANES).astype(f32)
    posf = positions.astype(f32)
    qt, kc, vc, ks, vst, kw, vwt = nsa_prep(f3, posf.reshape(b, t, 1), freq_row)
    nrow = t // CMP_STRIDE
    cpos = jnp.concatenate([posf[:, CMP_LEN - 1::CMP_STRIDE], posf[:, -1:]], axis=1).reshape(b, nrow, 1)
    flat = lambda a: a.reshape(b, NSA_GROUPS, nrow, CMP_STRIDE * NSA_HD)
    kcc, vcct = nsa_compress(flat(kc), flat(vc), pe_k.reshape(1, -1), w1_k.astype(bf16), w2_k.astype(bf16),
                             pe_v.reshape(1, -1), w1_v.astype(bf16), w2_v.T.astype(bf16), cpos, freq_row)
    n_sel = t // SEL_BLOCK
    c_start = jnp.arange(nrow) * CMP_STRIDE
    s_start = jnp.arange(n_sel) * SEL_BLOCK
    overlap_t = jnp.clip(jnp.minimum(c_start[None, :] + CMP_LEN, s_start[:, None] + SEL_BLOCK)
                         - jnp.maximum(c_start[None, :], s_start[:, None]), 0, None).astype(f32) / CMP_LEN
    o = nsa_attention(qt, kcc, vcct, ks, vst, kw, vwt, f3, overlap_t)
    return out_res_norm([o.reshape(b * t, qw)], [w_out.astype(bf16)], h, g_post)


def kernel(x, positions, norm_mix_pre, norm_mix_post, norm_ffn_pre, norm_ffn_post, w_ffn_up, w_ffn_down, ab_w_in,
           ab_w_out, gdn_conv, gdn_a_log, gdn_dt_bias, gdn_norm, rwkv_mu, rwkv_w0, rwkv_w2, rwkv_a0, rwkv_a2,
           rwkv_g2, rwkv_k_k, rwkv_k_a, rwkv_r_k, rwkv_ln_w, rwkv_ln_b, nsa_w_in, nsa_w_out, nsa_pe_k, nsa_w1_k,
           nsa_w2_k, nsa_pe_v, nsa_w1_v, nsa_w2_v):
    b, t, d = x.shape
    h = x.reshape(b * t, d)
    h = _layer0(h, b, t, norm_mix_pre[0], ab_w_in[0], ab_w_out[0], norm_mix_post[0], gdn_conv[0], gdn_a_log[0],
                gdn_dt_bias[0], gdn_norm[0], rwkv_mu[0], rwkv_w0[0], rwkv_w2[0], rwkv_a0[0], rwkv_a2[0],
                rwkv_g2[0], rwkv_k_k[0], rwkv_k_a[0], rwkv_r_k[0].reshape(-1), rwkv_ln_w[0], rwkv_ln_b[0])
    h = ffn(h, norm_ffn_pre[0], w_ffn_up[0].astype(bf16), w_ffn_down[0].astype(bf16), norm_ffn_post[0])
    h = _layer1(h, b, t, positions, norm_mix_pre[1], nsa_w_in[0], nsa_w_out[0], norm_mix_post[1], nsa_pe_k[0],
                nsa_w1_k[0], nsa_w2_k[0], nsa_pe_v[0], nsa_w1_v[0], nsa_w2_v[0])
    h = ffn(h, norm_ffn_pre[1], w_ffn_up[1].astype(bf16), w_ffn_down[1].astype(bf16), norm_ffn_post[1])
    return h.reshape(b, t, d)
```

```python
import functools
import math

import jax
import jax.numpy as jnp
from jax import lax
from jax.experimental import pallas as pl
from jax.experimental.pallas import tpu as pltpu

f32 = jnp.float32
bf16 = jnp.bfloat16
HI = lax.Precision.HIGHEST

V7X_VMEM_LIMIT_BYTES = 56 * 1024 * 1024
LANES = 128
SUBLANES = 8

D_MODEL = 1024
D_FF = 4 * D_MODEL
NORM_EPS = 1e-6
GDN_HEADS = 4
GDN_D = 128
GDN_CONV = 4
GDN_CHUNK = 128
GDN_CHUNK_GROUP = 2
RWKV_HEADS = 8
RWKV_N = 64
RWKV_W = RWKV_HEADS * RWKV_N
RWKV_CHUNK = 64
RWKV_CHUNK_GROUP = 2
RWKV_LN_EPS = 64e-5
NSA_HEADS = 16
NSA_GROUPS = 2
NSA_HPG = NSA_HEADS // NSA_GROUPS
NSA_HD = 64
CMP_LEN = 32
CMP_STRIDE = 16
CMP_HIDDEN = 256
SEL_BLOCK = 64
SEL_TOPN = 16
WINDOW = 512
NSA_QBLOCK = 64
NSA_QPAIR = 2
ROPE_THETA = 500000.0
ROPE_DIM = NSA_HD // 4
ROPE_HALF = ROPE_DIM // 2

AB_COLS = 4096
AB_RWKV0 = 2048
AB_LORA0 = 3584
AB_BA0 = 3840
NSA_COLS = 2048
NSA_KV0 = 1024
NSA_GATE0 = 1792


def _cparams(sem):
    return pltpu.CompilerParams(dimension_semantics=sem, vmem_limit_bytes=V7X_VMEM_LIMIT_BYTES)


def _rms(x, g):
    return x * lax.rsqrt(jnp.mean(x * x, axis=-1, keepdims=True) + NORM_EPS) * g


def _dot(a, b, precision=None):
    return jnp.dot(a, b, precision=precision, preferred_element_type=f32)


def _dot_nt(a, b, precision=None):
    return lax.dot_general(a, b, (((1,), (1,)), ((), ())), precision=precision, preferred_element_type=f32)


def _dot_tn(a, b, precision=None):
    return lax.dot_general(a, b, (((0,), (0,)), ((), ())), precision=precision, preferred_element_type=f32)


def _bdot(a, b):
    return _dot(a.astype(bf16), b.astype(bf16))


def _bdot_nt(a, b):
    return _dot_nt(a.astype(bf16), b.astype(bf16))


def _bdot_tn(a, b):
    return _dot_tn(a.astype(bf16), b.astype(bf16))


def _dot01(m01, x):
    m = m01.astype(bf16)
    hi = x.astype(bf16)
    rest = x - hi.astype(f32)
    mid = rest.astype(bf16)
    lo = (rest - mid.astype(f32)).astype(bf16)
    return _dot(m, hi) + _dot(m, mid) + _dot(m, lo)


def _iota2(shape, axis):
    return lax.broadcasted_iota(jnp.int32, shape, axis)


def _norm_matmul_body(x_ref, g_ref, w_ref, o_ref, u_ref):
    @pl.when(pl.program_id(1) == 0)
    def _():
        u_ref[...] = _rms(x_ref[...], g_ref[...]).astype(bf16)

    o_ref[...] = _dot(u_ref[...], w_ref[...])


def norm_matmul(x, g, w, tm=1024, tn=1024):
    m, d = x.shape
    n = w.shape[1]
    tm = min(tm, m)
    return pl.pallas_call(
        _norm_matmul_body,
        grid=(m // tm, n // tn),
        in_specs=[pl.BlockSpec((tm, d), lambda i, j: (i, 0)),
                  pl.BlockSpec((1, d), lambda i, j: (0, 0)),
                  pl.BlockSpec((d, tn), lambda i, j: (0, j))],
        out_specs=pl.BlockSpec((tm, tn), lambda i, j: (i, j)),
        out_shape=jax.ShapeDtypeStruct((m, n), f32),
        scratch_shapes=[pltpu.VMEM((tm, d), bf16)],
        compiler_params=_cparams(("parallel", "arbitrary")),
        name="norm_matmul",
    )(x, g.reshape(1, d), w)


def _out_res_norm_body(*refs, n_parts):
    a_refs = refs[:n_parts]
    w_refs = refs[n_parts:2 * n_parts]
    h_ref, g_ref, o_ref = refs[2 * n_parts:]
    y = _dot(a_refs[0][...].astype(bf16), w_refs[0][...])
    for a_ref, w_ref in zip(a_refs[1:], w_refs[1:]):
        y = y + _dot(a_ref[...].astype(bf16), w_ref[...])
    o_ref[...] = h_ref[...] + _rms(y, g_ref[...])


def out_res_norm(parts, w_parts, h, g, tm=512):
    m, d = h.shape
    tm = min(tm, m)
    n_parts = len(parts)
    in_specs = ([pl.BlockSpec((tm, p.shape[1]), lambda i: (i, 0)) for p in parts]
                + [pl.BlockSpec(w.shape, lambda i: (0, 0)) for w in w_parts]
                + [pl.BlockSpec((tm, d), lambda i: (i, 0)), pl.BlockSpec((1, d), lambda i: (0, 0))])
    return pl.pallas_call(
        functools.partial(_out_res_norm_body, n_parts=n_parts),
        grid=(m // tm,),
        in_specs=in_specs,
        out_specs=pl.BlockSpec((tm, d), lambda i: (i, 0)),
        out_shape=jax.ShapeDtypeStruct((m, d), f32),
        compiler_params=_cparams(("parallel",)),
        name="out_res_norm",
    )(*parts, *w_parts, h, g.reshape(1, d))


def _ffn_body(h_ref, g1_ref, wup_ref, wdn_ref, g2_ref, o_ref, u_ref, acc_ref):
    k = pl.program_id(1)

    @pl.when(k == 0)
    def _():
        u_ref[...] = _rms(h_ref[...], g1_ref[...]).astype(bf16)
        acc_ref[...] = jnp.zeros_like(acc_ref)

    a = _dot(u_ref[...], wup_ref[...])
    a = jnp.square(jnp.maximum(a, 0.0))
    acc_ref[...] += _dot(a.astype(bf16), wdn_ref[...])

    @pl.when(k == pl.num_programs(1) - 1)
    def _():
        o_ref[...] = h_ref[...] + _rms(acc_ref[...], g2_ref[...])


def ffn(h, g1, wup, wdn, g2, tm=1024, tf=512):
    m, d = h.shape
    ff = wup.shape[1]
    tm = min(tm, m)
    return pl.pallas_call(
        _ffn_body,
        grid=(m // tm, ff // tf),
        in_specs=[pl.BlockSpec((tm, d), lambda i, k: (i, 0)),
                  pl.BlockSpec((1, d), lambda i, k: (0, 0)),
                  pl.BlockSpec((d, tf), lambda i, k: (0, k)),
                  pl.BlockSpec((tf, d), lambda i, k: (k, 0)),
                  pl.BlockSpec((1, d), lambda i, k: (0, 0))],
        out_specs=pl.BlockSpec((tm, d), lambda i, k: (i, 0)),
        out_shape=jax.ShapeDtypeStruct((m, d), f32),
        scratch_shapes=[pltpu.VMEM((tm, d), bf16), pltpu.VMEM((tm, d), f32)],
        compiler_params=_cparams(("parallel", "arbitrary")),
        name="ffn",
    )(h, g1.reshape(1, d), wup, wdn, g2.reshape(1, d))


def _neumann_inverses(n_mats, size):
    eye = (_iota2((size, size), 0) == _iota2((size, size), 1)).astype(f32)
    ts = [eye + n for n in n_mats]
    ps = list(n_mats)
    for _ in range(int(math.log2(size)) - 1):
        ps = [_bdot(p, p) for p in ps]
        ts = [t + _bdot(t, p) for t, p in zip(ts, ps)]
    return ts


def _gdn_body(q_ref, k_ref, v_ref, z_ref, ba_ref, cw_ref, arow_ref, dtrow_ref, nw_ref, o_ref,
              xp_ref, qkv_ref, s_ref, *, tt):
    c = GDN_CHUNK
    d = GDN_D
    heads = range(GDN_HEADS)
    width = GDN_HEADS * d

    @pl.when(pl.program_id(1) == 0)
    def _():
        xp_ref[:, pl.ds(0, SUBLANES), :] = jnp.zeros((3, SUBLANES, width), f32)
        s_ref[...] = jnp.zeros_like(s_ref)

    for idx, ref in enumerate((q_ref, k_ref, v_ref)):
        xp_ref[idx, pl.ds(SUBLANES, tt), :] = ref[0]
        w = cw_ref[idx]
        y = xp_ref[idx, pl.ds(SUBLANES, tt), :] * w[GDN_CONV - 1:GDN_CONV, :]
        for j in range(GDN_CONV - 1):
            y = y + xp_ref[idx, pl.ds(SUBLANES - (GDN_CONV - 1) + j, tt), :] * w[j:j + 1, :]
        y = y * jax.nn.sigmoid(y)
        for h in heads:
            yh = y[:, h * d:(h + 1) * d]
            if idx < 2:
                yh = yh * lax.rsqrt(jnp.sum(yh * yh, axis=-1, keepdims=True) + 1e-6)
            if idx == 0:
                yh = yh * (d ** -0.5)
            qkv_ref[idx, :, h * d:(h + 1) * d] = yh
        xp_ref[idx, pl.ds(0, SUBLANES), :] = xp_ref[idx, pl.ds(tt, SUBLANES), :]

    row = _iota2((c, c), 0)
    col = _iota2((c, c), 1)
    tril = row >= col
    strict = row > col
    eye = row == col
    cum_l = tril.astype(f32)
    last_row = _iota2((c, 1), 0) == c - 1

    group = GDN_CHUNK_GROUP
    nh = GDN_HEADS

    def chunk_group(gi, carry):
        rows = [pl.ds(pl.multiple_of((gi * group + j) * c, c), c) for j in range(group)]
        sig, gcum = [], []
        for r in rows:
            ba = ba_ref[0, r, :]
            sig.append(jax.nn.sigmoid(ba))
            gcum.append(_dot01(cum_l, -jnp.exp(arow_ref[...]) * jax.nn.softplus(ba + dtrow_ref[...])))
        units = lambda f: [f(j, h) for j in range(group) for h in heads]
        qn = units(lambda j, h: qkv_ref[0, rows[j], h * d:(h + 1) * d])
        kn = units(lambda j, h: qkv_ref[1, rows[j], h * d:(h + 1) * d])
        vv = units(lambda j, h: qkv_ref[2, rows[j], h * d:(h + 1) * d])
        beta = units(lambda j, h: sig[j][:, h:h + 1])
        gc = units(lambda j, h: gcum[j][:, nh + h:nh + h + 1])
        gc_row = [jnp.sum(jnp.where(eye, jnp.broadcast_to(g, (c, c)), 0.0), axis=0, keepdims=True) for g in gc]
        gc_last = [jnp.sum(jnp.where(last_row, g, 0.0), axis=0, keepdims=True) for g in gc]
        decay = [jnp.exp(jnp.where(tril, g - gr, -jnp.inf)) for g, gr in zip(gc, gc_row)]
        knb = [k.astype(bf16) for k in kn]
        kk = [_dot_nt(k, k) for k in knb]
        qk = [_dot_nt(q.astype(bf16), k) for q, k in zip(qn, knb)]
        t_inv = _neumann_inverses([-jnp.where(strict, b * x * dc, 0.0) for b, x, dc in zip(beta, kk, decay)], c)
        egc = [jnp.exp(g) for g in gc]
        u = [_bdot(t, v * b) for t, v, b in zip(t_inv, vv, beta)]
        w = [_bdot(t, k * (b * e)).astype(bf16) for t, k, b, e in zip(t_inv, kn, beta, egc)]
        intra = [jnp.where(tril, x * dc, 0.0).astype(bf16) for x, dc in zip(qk, decay)]
        q_g = [(q * e).astype(bf16) for q, e in zip(qn, egc)]
        k_g = [(k * jnp.exp(gl - g)).astype(bf16) for k, gl, g in zip(kn, gc_last, gc)]
        state = [s_ref[h] for h in heads]
        for j in range(group):
            sl = slice(j * nh, (j + 1) * nh)
            sb = [x.astype(bf16) for x in state]
            v_new = [(x - _dot(y, z)).astype(bf16) for x, y, z in zip(u[sl], w[sl], sb)]
            o = [_dot(q, z) + _dot(a, vn) for q, z, a, vn in zip(q_g[sl], sb, intra[sl], v_new)]
            state = [s * jnp.exp(gl) + _dot_tn(k, vn) for s, gl, k, vn in zip(state, gc_last[sl], k_g[sl], v_new)]
            for h in heads:
                z = z_ref[0, rows[j], h * d:(h + 1) * d]
                o_ref[0, rows[j], h * d:(h + 1) * d] = _rms(o[h], nw_ref[...]) * (z * jax.nn.sigmoid(z))
        for h in heads:
            s_ref[h] = state[h]
        return carry

    lax.fori_loop(0, tt // (c * group), chunk_group, 0)


def gdn_mixer(f3, conv_w, arow, dtrow, norm_w, tt=512):
    b, t, _ = f3.shape
    tt = min(tt, t)
    width = GDN_HEADS * GDN_D
    col = lambda j: pl.BlockSpec((1, tt, width), lambda bi, ti, j=j: (bi, ti, j))
    return pl.pallas_call(
        functools.partial(_gdn_body, tt=tt),
        grid=(b, t // tt),
        in_specs=[col(0), col(1), col(2), col(3),
                  pl.BlockSpec((1, tt, LANES), lambda bi, ti: (bi, ti, AB_BA0 // LANES)),
                  pl.BlockSpec((3, GDN_CONV, width), lambda bi, ti: (0, 0, 0)),
                  pl.BlockSpec((1, LANES), lambda bi, ti: (0, 0)),
                  pl.BlockSpec((1, LANES), lambda bi, ti: (0, 0)),
                  pl.BlockSpec((1, GDN_D), lambda bi, ti: (0, 0))],
        out_specs=pl.BlockSpec((1, tt, width), lambda bi, ti: (bi, ti, 0)),
        out_shape=jax.ShapeDtypeStruct((b, t, width), f32),
        scratch_shapes=[pltpu.VMEM((3, tt + SUBLANES, width), f32),
                        pltpu.VMEM((3, tt, width), f32),
                        pltpu.VMEM((GDN_HEADS, GDN_D, GDN_D), f32)],
        compiler_params=_cparams(("parallel", "arbitrary")),
        name="gdn",
    )(f3, f3, f3, f3, f3, conv_w, arow, dtrow, norm_w)


def _rwkv_prep_body(r_ref, k_ref, v_ref, l_ref, mu_ref, w0_ref, w2_ref, a0_ref, a2_ref, g2_ref, kk_ref, ka_ref,
                    ro_ref, lw_ref, k2_ref, vo_ref, kko_ref, ao_ref, go_ref, xp_ref, *, tt):
    @pl.when(pl.program_id(1) == 0)
    def _():
        xp_ref[pl.ds(0, 8), :] = jnp.zeros((8, xp_ref.shape[1]), f32)

    w = RWKV_W
    xp_ref[pl.ds(8, tt), 0:w] = r_ref[0]
    xp_ref[pl.ds(8, tt), w:2 * w] = k_ref[0]
    xp_ref[pl.ds(8, tt), 2 * w:3 * w] = v_ref[0]
    xp_ref[pl.ds(8, tt), 3 * w:] = l_ref[0]
    x = xp_ref[pl.ds(8, tt), :]
    x = x + (xp_ref[pl.ds(7, tt), :] - x) * mu_ref[...]
    xp_ref[pl.ds(0, 8), :] = xp_ref[pl.ds(tt, 8), :]

    r, k, v = x[:, 0:w], x[:, w:2 * w], x[:, 2 * w:3 * w]
    wd, ad, gd = x[:, 3 * w:3 * w + 64], x[:, 3 * w + 64:3 * w + 128], x[:, 3 * w + 128:]
    w_log = -jax.nn.softplus(-(w0_ref[...] + _dot(jnp.tanh(wd), w2_ref[...], HI))) - 0.5
    a = jax.nn.sigmoid(a0_ref[...] + _dot(ad, a2_ref[...], HI))
    ro_ref[0] = r
    lw_ref[0] = -jnp.exp(w_log)
    k2_ref[0] = k * (1.0 + (a - 1.0) * ka_ref[...])
    vo_ref[0] = v
    kko_ref[0] = k * kk_ref[...]
    ao_ref[0] = a
    go_ref[0] = _dot(jax.nn.sigmoid(gd), g2_ref[...], HI)


def rwkv_prep(f3, mu, w0, w2, a0, a2, g2, k_k, k_a, tt=256):
    b, t, _ = f3.shape
    tt = min(tt, t)
    w = RWKV_W
    wide = 3 * w + 256
    row = lambda n: pl.BlockSpec((1, n), lambda bi, ti: (0, 0))
    full = lambda a: pl.BlockSpec(a.shape, lambda bi, ti: (0, 0))
    out = pl.BlockSpec((1, tt, w), lambda bi, ti: (bi, ti, 0))
    return pl.pallas_call(
        functools.partial(_rwkv_prep_body, tt=tt),
        grid=(b, t // tt),
        in_specs=[pl.BlockSpec((1, tt, w), lambda bi, ti: (bi, ti, AB_RWKV0 // w)),
                  pl.BlockSpec((1, tt, w), lambda bi, ti: (bi, ti, AB_RWKV0 // w + 1)),
                  pl.BlockSpec((1, tt, w), lambda bi, ti: (bi, ti, AB_RWKV0 // w + 2)),
                  pl.BlockSpec((1, tt, 256), lambda bi, ti: (bi, ti, AB_LORA0 // 256)),
                  row(wide), row(w), full(w2), row(w), full(a2), full(g2), row(w), row(w)],
        out_specs=[out] * 7,
        out_shape=[jax.ShapeDtypeStruct((b, t, w), f32)] * 7,
        scratch_shapes=[pltpu.VMEM((tt + 8, wide), f32)],
        compiler_params=_cparams(("parallel", "arbitrary")),
        name="rwkv_prep",
    )(f3, f3, f3, f3, mu.reshape(1, wide), w0.reshape(1, w), w2, a0.reshape(1, w), a2, g2,
      k_k.reshape(1, w), k_a.reshape(1, w))


def _rwkv_scan_body(r_ref, lw_ref, k2_ref, v_ref, kk_ref, a_ref, g_ref, rk_ref, lnw_ref, lnb_ref, o_ref, s_ref,
                    *, tt):
    c = RWKV_CHUNK
    n = RWKV_N

    @pl.when(pl.program_id(1) == 0)
    def _():
        s_ref[...] = jnp.zeros_like(s_ref)

    row = _iota2((c, c), 0)
    col = _iota2((c, c), 1)
    tril = row >= col
    strict = row > col
    cum_l = tril.astype(f32)
    last = _iota2((c, 1), 0) == c - 1

    heads = range(RWKV_HEADS)
    per_head = lambda x: [x[:, h * n:(h + 1) * n] for h in heads]

    group = RWKV_CHUNK_GROUP

    def chunk_group(gi, carry):
        rows = [pl.ds(pl.multiple_of((gi * group + j) * c, c), c) for j in range(group)]
        units = lambda f: [x for r in rows for x in f(r)]
        lw_all = [lw_ref[0, r, :] for r in rows]
        p_all = [_dot01(cum_l, x) for x in lw_all]
        em_all = [jnp.exp(-p) for p in p_all]
        rh = units(lambda r: per_head(r_ref[0, r, :]))
        k2 = units(lambda r: per_head(k2_ref[0, r, :]))
        vh = units(lambda r: per_head(v_ref[0, r, :]))
        r_t = [x for r, p in zip(rows, p_all) for x in per_head(r_ref[0, r, :] * jnp.exp(p))]
        k_t = [x for r, e in zip(rows, em_all) for x in per_head(k2_ref[0, r, :] * e)]
        e_prev = [x for p, lw in zip(p_all, lw_all) for x in per_head(jnp.exp(p - lw))]
        lr_em = [x for r, e in zip(rows, em_all) for x in per_head(a_ref[0, r, :] * e)]
        kk = [x * lax.rsqrt(jnp.sum(x * x, axis=-1, keepdims=True) + 1e-6)
              for x in units(lambda r: per_head(kk_ref[0, r, :]))]
        a_t = [-x * e for x, e in zip(kk, e_prev)]
        b_t = [x * e for x, e in zip(kk, lr_em)]
        m_ab = [jnp.where(strict, _bdot_nt(a, b), 0.0) for a, b in zip(a_t, b_t)]
        m_ak = [jnp.where(strict, _bdot_nt(a, k), 0.0) for a, k in zip(a_t, k_t)]
        a_rb = [jnp.where(tril, _bdot_nt(r, b), 0.0) for r, b in zip(r_t, b_t)]
        a_rk = [jnp.where(tril, _bdot_nt(r, k), 0.0) for r, k in zip(r_t, k_t)]
        t_inv = _neumann_inverses(m_ab, c)
        w1 = [_bdot(t, a) for t, a in zip(t_inv, a_t)]
        mv = [_bdot(m, v) for m, v in zip(m_ak, vh)]
        u2 = [_bdot(t, x) for t, x in zip(t_inv, mv)]
        y_v = [_bdot(a, v) for a, v in zip(a_rk, vh)]
        ds_v = [_bdot_tn(v, k) for v, k in zip(vh, k_t)]
        decay_last = [x for p in p_all
                      for x in per_head(jnp.exp(jnp.sum(jnp.where(last, p, 0.0), axis=0, keepdims=True)))]
        g_h = units(lambda r: per_head(g_ref[0, r, :]))
        bonus = [jnp.sum(r * k * rk_ref[:, (i % RWKV_HEADS) * n:(i % RWKV_HEADS + 1) * n], axis=-1, keepdims=True) * v
                 for i, (r, k, v) in enumerate(zip(rh, k2, vh))]
        state = [s_ref[h] for h in heads]
        for j in range(group):
            sl_u = slice(j * RWKV_HEADS, (j + 1) * RWKV_HEADS)
            u = [_bdot_nt(w, s) + x for w, s, x in zip(w1[sl_u], state, u2[sl_u])]
            y_s = [_bdot_nt(r, s) for r, s in zip(r_t[sl_u], state)]
            y = [ys + _bdot(a, x) + yv for ys, a, x, yv in zip(y_s, a_rb[sl_u], u, y_v[sl_u])]
            ds_u = [_bdot_tn(x, b) for x, b in zip(u, b_t[sl_u])]
            state = [(s + du + dv) * dl for s, du, dv, dl in zip(state, ds_u, ds_v[sl_u], decay_last[sl_u])]
            for h in heads:
                sl = slice(h * n, (h + 1) * n)
                mean = jnp.mean(y[h], axis=-1, keepdims=True)
                yc = y[h] - mean
                var = jnp.mean(yc * yc, axis=-1, keepdims=True)
                yn = yc * lax.rsqrt(var + RWKV_LN_EPS) * lnw_ref[:, sl] + lnb_ref[:, sl]
                o_ref[0, rows[j], sl] = (yn + bonus[j * RWKV_HEADS + h]) * g_h[j * RWKV_HEADS + h]
        for h in heads:
            s_ref[h] = state[h]
        return carry

    lax.fori_loop(0, tt // (c * group), chunk_group, 0)


def rwkv_scan(r, lw, k2, v, kk, a, g, r_k, ln_w, ln_b, tt=256):
    b, t, w = r.shape
    tt = min(tt, t)
    blk = pl.BlockSpec((1, tt, w), lambda bi, ti: (bi, ti, 0))
    row = pl.BlockSpec((1, w), lambda bi, ti: (0, 0))
    return pl.pallas_call(
        functools.partial(_rwkv_scan_body, tt=tt),
        grid=(b, t // tt),
        in_specs=[blk] * 7 + [row] * 3,
        out_specs=blk,
        out_shape=jax.ShapeDtypeStruct((b, t, w), f32),
        scratch_shapes=[pltpu.VMEM((RWKV_HEADS, RWKV_N, RWKV_N), f32)],
        compiler_params=_cparams(("parallel", "arbitrary")),
        name="rwkv_scan",
    )(r, lw, k2, v, kk, a, g, r_k.reshape(1, w), ln_w.reshape(1, w), ln_b.reshape(1, w))


def _rope128(x, cos, sin, sign_lo, sign_hi):
    r_hi = pltpu.roll(x, ROPE_HALF, 1)
    r_lo = pltpu.roll(x, LANES - ROPE_HALF, 1)
    return x * cos + (r_lo * sign_lo + r_hi * sign_hi) * sin


def _rope_tables(pos, freq_row):
    ang = pos * freq_row
    m = _iota2((1, LANES), 1) % NSA_HD
    sign_lo = jnp.where(m < ROPE_HALF, -1.0, 0.0).astype(f32)
    sign_hi = jnp.where((m >= ROPE_HALF) & (m < ROPE_DIM), 1.0, 0.0).astype(f32)
    return jnp.cos(ang), jnp.sin(ang), sign_lo, sign_hi


def _nsa_prep_body(q_ref, kc_i, vc_i, ks_i, vs_i, kw_i, vw_i, pos_ref, freq_ref,
                   qo_ref, kc_ref, vc_ref, ks_ref, vs_ref, kw_ref, vw_ref):
    cos, sin, s_lo, s_hi = _rope_tables(pos_ref[0], freq_ref[...])
    scale = NSA_HD ** -0.5 * math.log2(math.e)
    n_blk = q_ref.shape[1] // NSA_QBLOCK
    for c in range(NSA_HEADS // 2):
        xt = (_rope128(q_ref[0, :, c * LANES:(c + 1) * LANES], cos, sin, s_lo, s_hi) * scale).T
        for r in range(2):
            head = 2 * c + r
            g, hl = head // NSA_HPG, head % NSA_HPG
            for i in range(n_blk):
                qo_ref[0, g, i, :, hl * NSA_QBLOCK:(hl + 1) * NSA_QBLOCK] = (
                    xt[r * NSA_HD:(r + 1) * NSA_HD, i * NSA_QBLOCK:(i + 1) * NSA_QBLOCK].astype(bf16))

    def split(src, ref, rope, dtype):
        x = src[0]
        if rope:
            x = _rope128(x, cos, sin, s_lo, s_hi)
        ref[0, 0] = x[:, :NSA_HD].astype(dtype)
        ref[0, 1] = x[:, NSA_HD:].astype(dtype)

    def split_t(src, ref):
        xt = src[0].T
        ref[0, 0] = xt[:NSA_HD].astype(bf16)
        ref[0, 1] = xt[NSA_HD:].astype(bf16)

    split(kc_i, kc_ref, False, f32)
    split(vc_i, vc_ref, False, f32)
    split(ks_i, ks_ref, True, bf16)
    split_t(vs_i, vs_ref)
    split(kw_i, kw_ref, True, bf16)
    split_t(vw_i, vw_ref)


def nsa_prep(f3, pos3, freq_row, tt=512):
    b, t, _ = f3.shape
    tt = min(tt, t)
    g = NSA_GROUPS
    n_blk = tt // NSA_QBLOCK
    kv_in = [pl.BlockSpec((1, tt, LANES), lambda bi, ti, c=NSA_KV0 // LANES + i: (bi, ti, c)) for i in range(6)]
    kv_spec = pl.BlockSpec((1, g, tt, NSA_HD), lambda bi, ti: (bi, 0, ti, 0))
    kvt_spec = pl.BlockSpec((1, g, NSA_HD, tt), lambda bi, ti: (bi, 0, 0, ti))
    kv32 = jax.ShapeDtypeStruct((b, g, t, NSA_HD), f32)
    kv16 = jax.ShapeDtypeStruct((b, g, t, NSA_HD), bf16)
    kvt16 = jax.ShapeDtypeStruct((b, g, NSA_HD, t), bf16)
    q_lanes = NSA_HPG * NSA_QBLOCK
    return pl.pallas_call(
        _nsa_prep_body,
        grid=(b, t // tt),
        in_specs=[pl.BlockSpec((1, tt, NSA_HEADS * NSA_HD), lambda bi, ti: (bi, ti, 0))] + kv_in
                 + [pl.BlockSpec((1, tt, 1), lambda bi, ti: (bi, ti, 0)),
                    pl.BlockSpec((1, LANES), lambda bi, ti: (0, 0))],
        out_specs=[pl.BlockSpec((1, g, n_blk, NSA_HD, q_lanes), lambda bi, ti: (bi, 0, ti, 0, 0)),
                   kv_spec, kv_spec, kv_spec, kvt_spec, kv_spec, kvt_spec],
        out_shape=[jax.ShapeDtypeStruct((b, g, t // NSA_QBLOCK, NSA_HD, q_lanes), bf16),
                   kv32, kv32, kv16, kvt16, kv16, kvt16],
        compiler_params=_cparams(("parallel", "parallel")),
        name="nsa_prep",
    )(f3, f3, f3, f3, f3, f3, f3, pos3, freq_row)


def _nsa_compress_body(kc_ref, vc_ref, pek_ref, w1k_ref, w2k_ref, pev_ref, w1v_ref, w2v_ref, pos_ref, freq_ref,
                       ko_ref, vo_ref):
    half = CMP_STRIDE * NSA_HD
    nrow = kc_ref.shape[2]
    last_row = _iota2((nrow, 1), 0) == nrow - 1

    def hidden(x, pe_ref, w1_ref):
        lo = _dot((x + pe_ref[:, :half]).astype(bf16), w1_ref[:half, :])
        hi = _dot((x + pe_ref[:, half:]).astype(bf16), w1_ref[half:, :])
        hi = jnp.where(last_row, 0.0, pltpu.roll(hi, nrow - 1, 0))
        hid = lo + hi
        return (hid * jax.nn.sigmoid(hid)).astype(bf16)

    ks = [_dot(hidden(kc_ref[0, g], pek_ref, w1k_ref), w2k_ref[...]) for g in range(NSA_GROUPS)]
    cos, sin, s_lo, s_hi = _rope_tables(pos_ref[0], freq_ref[...])
    kr = _rope128(jnp.concatenate(ks, axis=-1), cos, sin, s_lo, s_hi)
    for g in range(NSA_GROUPS):
        ko_ref[0, g] = kr[:, g * NSA_HD:(g + 1) * NSA_HD].astype(bf16)
        vo_ref[0, g] = _dot_nt(w2v_ref[...], hidden(vc_ref[0, g], pev_ref, w1v_ref)).astype(bf16)


def nsa_compress(kc4, vc4, pe_k, w1_k, w2_k, pe_v, w1_v, w2_vt, cpos3, freq_row):
    b, g, nrow, wide = kc4.shape
    full = lambda a: pl.BlockSpec(a.shape, lambda bi: (0,) * a.ndim)
    blk = pl.BlockSpec((1, g, nrow, wide), lambda bi: (bi, 0, 0, 0))
    out = pl.BlockSpec((1, g, nrow, NSA_HD), lambda bi: (bi, 0, 0, 0))
    out_t = pl.BlockSpec((1, g, NSA_HD, nrow), lambda bi: (bi, 0, 0, 0))
    return pl.pallas_call(
        _nsa_compress_body,
        grid=(b,),
        in_specs=[blk, blk, full(pe_k), full(w1_k), full(w2_k), full(pe_v), full(w1_v), full(w2_vt),
                  pl.BlockSpec((1, nrow, 1), lambda bi: (bi, 0, 0)), pl.BlockSpec((1, LANES), lambda bi: (0, 0))],
        out_specs=[out, out_t],
        out_shape=[jax.ShapeDtypeStruct((b, g, nrow, NSA_HD), bf16),
                   jax.ShapeDtypeStruct((b, g, NSA_HD, nrow), bf16)],
        compiler_params=_cparams(("parallel",)),
        name="nsa_compress",
    )(kc4, vc4, pe_k, w1_k, w2_k, pe_v, w1_v, w2_vt, cpos3, freq_row)


def _nsa_attn_body(qt_ref, kc_ref, vct_ref, ks_ref, vst_ref, kw_ref, vwt_ref, gate_ref, ovt_ref, dbias_ref,
                   wbias_ref, o_ref, score_ref, sel_ref, s_ref, acc_ref, sw_ref, *, key_tile):
    qb_n = NSA_QBLOCK
    blk_lanes = NSA_HPG * qb_n
    lanes = NSA_QPAIR * blk_lanes
    pair = pl.program_id(2)
    qt = jnp.concatenate([qt_ref[0, 0, a] for a in range(NSA_QPAIR)], axis=1)
    lane = _iota2((1, lanes), 1)
    t_lane = (pair * NSA_QPAIR + lane // blk_lanes) * qb_n + lane % qb_n

    def softmax_t(s):
        mx = jnp.max(s, axis=0, keepdims=True)
        mx = jnp.where(mx > -jnp.inf, mx, 0.0)
        e = jnp.exp2(s - mx)
        return e, jnp.sum(e, axis=0, keepdims=True), mx

    diag0 = pair * NSA_QPAIR
    blocks_per_tile = key_tile // SEL_BLOCK
    n_tiles = (diag0 * SEL_BLOCK + key_tile - 1) // key_tile

    def score_tile(kt):
        kk0 = pl.multiple_of(kt * key_tile, key_tile)
        return _dot(ks_ref[0, 0, pl.ds(kk0, key_tile), :], qt)

    s_ref[0] = score_tile(0)

    gt = jax.nn.sigmoid(gate_ref[0]).T
    gate = lambda br: jnp.concatenate([gt[3 * h + br:3 * h + br + 1, a * qb_n:(a + 1) * qb_n]
                                       for a in range(NSA_QPAIR) for h in range(NSA_HPG)], axis=1)
    gates = [gate(br) for br in range(3)]

    n_cmp = kc_ref.shape[2]
    cmp_end = _iota2((n_cmp, 1), 0) * CMP_STRIDE + (CMP_LEN - 1)
    e_c, den_c, _ = softmax_t(jnp.where(cmp_end <= t_lane, _dot(kc_ref[0, 0], qt), -jnp.inf))
    p_c = e_c * (1.0 / jnp.maximum(den_c, 1e-30))
    o_c = _dot(vct_ref[0, 0], p_c.astype(bf16))

    n_sel = ovt_ref.shape[0]
    p_pair = []
    for a in range(NSA_QPAIR):
        acc = p_c[:, a * blk_lanes:a * blk_lanes + LANES]
        for c in range(1, blk_lanes // LANES):
            acc = acc + p_c[:, a * blk_lanes + c * LANES:a * blk_lanes + (c + 1) * LANES]
        p_pair.append(acc[:, :qb_n] + acc[:, qb_n:])
    imp = _dot(ovt_ref[...], jnp.concatenate(p_pair, axis=1), HI)
    blk = _iota2((n_sel, NSA_QPAIR * qb_n), 0)
    cur = pair * NSA_QPAIR + _iota2((n_sel, NSA_QPAIR * qb_n), 1) // qb_n
    valid = blk <= cur
    forced = (blk == 0) | (blk == cur) | (blk == cur - 1)
    score = jnp.where(valid, jnp.where(forced, jnp.inf, imp), -jnp.inf)
    score_ref[...] = score

    span = WINDOW + NSA_QPAIR * qb_n
    w0 = pl.multiple_of(pair * NSA_QPAIR * qb_n, LANES)
    sw_ref[...] = _dot(kw_ref[0, 0, pl.ds(w0, span), :], qt) + wbias_ref[...]

    @pl.when(w0 < WINDOW)
    def _():
        is_pad = _iota2((WINDOW, 1), 0) < WINDOW - w0
        sw_ref[pl.ds(0, WINDOW), :] = jnp.where(is_pad, -jnp.inf, sw_ref[pl.ds(0, WINDOW), :])

    e_w, l_w, _ = softmax_t(sw_ref[...])
    o_w = _dot(vwt_ref[0, 0, :, pl.ds(w0, span)], e_w.astype(bf16))

    k0 = pl.multiple_of(diag0 * SEL_BLOCK, NSA_QPAIR * SEL_BLOCK)
    e_d, l_s, m_s = softmax_t(_dot(ks_ref[0, 0, pl.ds(k0, NSA_QPAIR * SEL_BLOCK), :], qt) + dbias_ref[...])
    acc_ref[...] = _dot(vst_ref[0, 0, :, pl.ds(k0, NSA_QPAIR * SEL_BLOCK)], e_d.astype(bf16))

    def rank_step(jp, cnt):
        other = score_ref[pl.ds(jp, 1), :]
        ahead = (other > score) | ((other == score) & (blk > jp))
        return cnt + ahead.astype(jnp.int32)

    last_cur = pair * NSA_QPAIR + NSA_QPAIR - 1
    cnt = lax.fori_loop(0, jnp.where(last_cur >= SEL_TOPN, last_cur + 1, 0), rank_step,
                        jnp.zeros((n_sel, NSA_QPAIR * qb_n), jnp.int32))
    sel = ((cnt < SEL_TOPN) & (blk < diag0)).astype(f32)
    sel_ref[...] = jnp.concatenate([sel[:, a * qb_n:(a + 1) * qb_n] for a in range(NSA_QPAIR)
                                    for _ in range(NSA_HPG)], axis=1)

    def half_step(kt, slot, carry):
        m_old, l_old = carry
        s_ref[1 - slot] = score_tile(jnp.minimum(kt + 1, n_tiles - 1))
        live = kt < n_tiles
        kt = jnp.minimum(kt, n_tiles - 1)
        s = s_ref[slot]
        parts = []
        for bi in range(blocks_per_tile):
            picked = (sel_ref[pl.ds(kt * blocks_per_tile + bi, 1), :] > 0.5) & live
            parts.append(jnp.where(picked, s[bi * SEL_BLOCK:(bi + 1) * SEL_BLOCK], -jnp.inf))
        sm = jnp.concatenate(parts, axis=0)
        mx = jnp.maximum(m_old, jnp.max(sm, axis=0, keepdims=True))
        alpha = jnp.exp2(m_old - mx)
        p = jnp.exp2(sm - mx)
        kk0 = pl.multiple_of(kt * key_tile, key_tile)
        acc_ref[...] = alpha * acc_ref[...] + _dot(vst_ref[0, 0, :, pl.ds(kk0, key_tile)], p.astype(bf16))
        return mx, alpha * l_old + jnp.sum(p, axis=0, keepdims=True)

    def pair_step(pi, carry):
        return half_step(2 * pi + 1, 1, half_step(2 * pi, 0, carry))

    m_s, l_s = lax.fori_loop(0, (n_tiles + 1) // 2, pair_step, (m_s, l_s))

    o_t = gates[0] * o_c + (gates[1] * (1.0 / l_s)) * acc_ref[...] + (gates[2] * (1.0 / l_w)) * o_w
    o_b = o_t.astype(bf16)
    eye = (_iota2((qb_n, qb_n), 0) == _iota2((qb_n, qb_n), 1)).astype(bf16)
    for a in range(NSA_QPAIR):
        for h in range(NSA_HPG):
            piece = o_b[:, a * blk_lanes + h * qb_n:a * blk_lanes + (h + 1) * qb_n]
            o_ref[0, a * qb_n:(a + 1) * qb_n, h * NSA_HD:(h + 1) * NSA_HD] = _dot_nt(eye, piece).astype(bf16)


def nsa_attention(qt, kc, vct, ks, vst, kw, vwt, f3, overlap_t, key_tile=256):
    b, g, n_q, d, blk_lanes = qt.shape
    t = n_q * NSA_QBLOCK
    key_tile = min(key_tile, t)
    n_cmp = kc.shape[2]
    n_sel = overlap_t.shape[0]
    lanes = NSA_QPAIR * blk_lanes
    q_rows = NSA_QPAIR * NSA_QBLOCK
    off = (jnp.arange(lanes) // blk_lanes) * NSA_QBLOCK + jnp.arange(lanes) % NSA_QBLOCK
    r_d = jnp.arange(NSA_QPAIR * SEL_BLOCK)[:, None]
    diag_bias = jnp.where(r_d <= off[None, :], 0.0, -jnp.inf).astype(f32)
    r_w = jnp.arange(WINDOW + q_rows)[:, None]
    win_bias = jnp.where((r_w > off[None, :]) & (r_w <= WINDOW + off[None, :]), 0.0, -jnp.inf).astype(f32)
    seq = pl.BlockSpec((1, 1, t, d), lambda bi, gi, qi: (bi, gi, 0, 0))
    seq_t = pl.BlockSpec((1, 1, d, t), lambda bi, gi, qi: (bi, gi, 0, 0))
    const = lambda a: pl.BlockSpec(a.shape, lambda bi, gi, qi: (0, 0))
    return pl.pallas_call(
        functools.partial(_nsa_attn_body, key_tile=key_tile),
        grid=(b, g, n_q // NSA_QPAIR),
        in_specs=[pl.BlockSpec((1, 1, NSA_QPAIR, d, blk_lanes), lambda bi, gi, qi: (bi, gi, qi, 0, 0)),
                  pl.BlockSpec((1, 1, n_cmp, d), lambda bi, gi, qi: (bi, gi, 0, 0)),
                  pl.BlockSpec((1, 1, d, n_cmp), lambda bi, gi, qi: (bi, gi, 0, 0)),
                  seq, seq_t,
                  pl.BlockSpec((1, 1, t + WINDOW, d), lambda bi, gi, qi: (bi, gi, 0, 0)),
                  pl.BlockSpec((1, 1, d, t + WINDOW), lambda bi, gi, qi: (bi, gi, 0, 0)),
                  pl.BlockSpec((1, q_rows, LANES), lambda bi, gi, qi: (bi, qi, NSA_GATE0 // LANES + gi)),
                  const(overlap_t), const(diag_bias), const(win_bias)],
        out_specs=pl.BlockSpec((1, q_rows, NSA_HPG * d), lambda bi, gi, qi: (bi, qi, gi)),
        out_shape=jax.ShapeDtypeStruct((b, t, g * NSA_HPG * d), bf16),
        scratch_shapes=[pltpu.VMEM((n_sel, q_rows), f32), pltpu.VMEM((n_sel, lanes), f32),
                        pltpu.VMEM((2, key_tile, lanes), f32), pltpu.VMEM((d, lanes), f32),
                        pltpu.VMEM((WINDOW + q_rows, lanes), f32)],
        compiler_params=_cparams(("parallel", "parallel", "arbitrary")),
        name="nsa_attn",
    )(qt, kc, vct, ks, vst, kw, vwt, f3, overlap_t, diag_bias, win_bias)


def _place(cols, total, pieces):
    out = jnp.zeros((cols, total), f32)
    for start, mat in pieces:
        out = lax.dynamic_update_slice(out, mat.astype(f32), (0, start))
    return out


def _layer0(h, b, t, g_pre, w_in, w_out, g_post, conv, a_log, dt_bias, gnorm, mu, w0, w2, a0, a2, g2, k_k, k_a,
            r_k, ln_w, ln_b):
    gdn_w = 4 * GDN_HEADS * GDN_D
    w_pad = _place(D_MODEL, AB_COLS, [
        (0, w_in[:, :gdn_w]),
        (AB_BA0, w_in[:, gdn_w:gdn_w + 2 * GDN_HEADS]),
        (AB_RWKV0, w_in[:, gdn_w + 2 * GDN_HEADS:gdn_w + 2 * GDN_HEADS + 3 * RWKV_W]),
        (AB_LORA0, w_in[:, gdn_w + 2 * GDN_HEADS + 3 * RWKV_W:]),
    ]).astype(bf16)
    f3 = norm_matmul(h, g_pre, w_pad).reshape(b, t, AB_COLS)
    arow = jnp.zeros((1, LANES), f32).at[0, GDN_HEADS:2 * GDN_HEADS].set(a_log)
    dtrow = jnp.zeros((1, LANES), f32).at[0, GDN_HEADS:2 * GDN_HEADS].set(dt_bias)
    conv3 = conv.reshape(GDN_CONV, 3, GDN_HEADS * GDN_D).transpose(1, 0, 2)
    o_a = gdn_mixer(f3, conv3, arow, dtrow, gnorm.reshape(1, GDN_D))
    prep = rwkv_prep(f3, mu, w0, w2, a0, a2, g2, k_k, k_a)
    o_b = rwkv_scan(*prep, r_k, ln_w, ln_b)
    m = b * t
    n_a = GDN_HEADS * GDN_D
    return out_res_norm([o_a.reshape(m, n_a), o_b.reshape(m, RWKV_W)],
                        [w_out[:n_a].astype(bf16), w_out[n_a:].astype(bf16)], h, g_post)


def _layer1(h, b, t, positions, g_pre, w_in, w_out, g_post, pe_k, w1_k, w2_k, pe_v, w1_v, w2_v):
    qw = NSA_HEADS * NSA_HD
    kvw = 6 * NSA_GROUPS * NSA_HD
    gates = w_in[:, qw + kvw:].reshape(D_MODEL, NSA_GROUPS, NSA_HPG * 3)
    w_pad = _place(D_MODEL, NSA_COLS, [(0, w_in[:, :qw + kvw])]
                   + [(NSA_GATE0 + gi * LANES, gates[:, gi]) for gi in range(NSA_GROUPS)]).astype(bf16)
    f3 = norm_matmul(h, g_pre, w_pad).reshape(b, t, NSA_COLS)
    inv_freq = ROPE_THETA ** (-jnp.arange(ROPE_HALF, dtype=f32) * (2.0 / ROPE_DIM))
    lane = jnp.arange(LANES)
    freq_row = jnp.where(lane % NSA_HD < ROPE_DIM, inv_freq[lane % ROPE_HALF], 0.0).reshape(1, LANES).astype(f32)
    posf = positions.astype(f32)
    qt, kc, vc, ks, vst, kw, vwt = nsa_prep(f3, posf.reshape(b, t, 1), freq_row)
    nrow = t // CMP_STRIDE
    cpos = jnp.concatenate([posf[:, CMP_LEN - 1::CMP_STRIDE], posf[:, -1:]], axis=1).reshape(b, nrow, 1)
    flat = lambda a: a.reshape(b, NSA_GROUPS, nrow, CMP_STRIDE * NSA_HD)
    kcc, vcct = nsa_compress(flat(kc), flat(vc), pe_k.reshape(1, -1), w1_k.astype(bf16), w2_k.astype(bf16),
                             pe_v.reshape(1, -1), w1_v.astype(bf16), w2_v.T.astype(bf16), cpos, freq_row)
    n_sel = t // SEL_BLOCK
    c_start = jnp.arange(nrow) * CMP_STRIDE
    s_start = jnp.arange(n_sel) * SEL_BLOCK
    overlap_t = jnp.clip(jnp.minimum(c_start[None, :] + CMP_LEN, s_start[:, None] + SEL_BLOCK)
                         - jnp.maximum(c_start[None, :], s_start[:, None]), 0, None).astype(f32) / CMP_LEN
    kw_pad = jnp.pad(kw, ((0, 0), (0, 0), (WINDOW, 0), (0, 0)))
    vwt_pad = jnp.pad(vwt, ((0, 0), (0, 0), (0, 0), (WINDOW, 0)))
    o = nsa_attention(qt, kcc, vcct, ks, vst, kw_pad, vwt_pad, f3, overlap_t)
    return out_res_norm([o.reshape(b * t, qw)], [w_out.astype(bf16)], h, g_post)


def kernel(x, positions, norm_mix_pre, norm_mix_post, norm_ffn_pre, norm_ffn_post, w_ffn_up, w_ffn_down, ab_w_in,
           ab_w_out, gdn_conv, gdn_a_log, gdn_dt_bias, gdn_norm, rwkv_mu, rwkv_w0, rwkv_w2, rwkv_a0, rwkv_a2,
           rwkv_g2, rwkv_k_k, rwkv_k_a, rwkv_r_k, rwkv_ln_w, rwkv_ln_b, nsa_w_in, nsa_w_out, nsa_pe_k, nsa_w1_k,
           nsa_w2_k, nsa_pe_v, nsa_w1_v, nsa_w2_v):
    b, t, d = x.shape
    h = x.reshape(b * t, d)
    h = _layer0(h, b, t, norm_mix_pre[0], ab_w_in[0], ab_w_out[0], norm_mix_post[0], gdn_conv[0], gdn_a_log[0],
                gdn_dt_bias[0], gdn_norm[0], rwkv_mu[0], rwkv_w0[0], rwkv_w2[0], rwkv_a0[0], rwkv_a2[0],
                rwkv_g2[0], rwkv_k_k[0], rwkv_k_a[0], rwkv_r_k[0].reshape(-1), rwkv_ln_w[0], rwkv_ln_b[0])
    h = ffn(h, norm_ffn_pre[0], w_ffn_up[0].astype(bf16), w_ffn_down[0].astype(bf16), norm_ffn_post[0])
    h = _layer1(h, b, t, positions, norm_mix_pre[1], nsa_w_in[0], nsa_w_out[0], norm_mix_post[1], nsa_pe_k[0],
                nsa_w1_k[0], nsa_w2_k[0], nsa_pe_v[0], nsa_w1_v[0], nsa_w2_v[0])
    h = ffn(h, norm_ffn_pre[1], w_ffn_up[1].astype(bf16), w_ffn_down[1].astype(bf16), norm_ffn_post[1])
    return h.reshape(b, t, d)
```

```python
import functools
import math

import jax
import jax.numpy as jnp
from jax import lax
from jax.experimental import pallas as pl
from jax.experimental.pallas import tpu as pltpu

f32 = jnp.float32
bf16 = jnp.bfloat16
HI = lax.Precision.HIGHEST

V7X_VMEM_LIMIT_BYTES = 56 * 1024 * 1024
LANES = 128
SUBLANES = 8

D_MODEL = 1024
D_FF = 4 * D_MODEL
NORM_EPS = 1e-6
GDN_HEADS = 4
GDN_D = 128
GDN_CONV = 4
GDN_CHUNK = 128
GDN_CHUNK_GROUP = 2
RWKV_HEADS = 8
RWKV_N = 64
RWKV_W = RWKV_HEADS * RWKV_N
RWKV_CHUNK = 64
RWKV_CHUNK_GROUP = 2
RWKV_LN_EPS = 64e-5
NSA_HEADS = 16
NSA_GROUPS = 2
NSA_HPG = NSA_HEADS // NSA_GROUPS
NSA_HD = 64
CMP_LEN = 32
CMP_STRIDE = 16
CMP_HIDDEN = 256
SEL_BLOCK = 64
SEL_TOPN = 16
WINDOW = 512
NSA_QBLOCK = 64
NSA_QPAIR = 2
NSA_MASKED = -(2.0 ** 126)
ROPE_THETA = 500000.0
ROPE_DIM = NSA_HD // 4
ROPE_HALF = ROPE_DIM // 2

AB_COLS = 4096
AB_RWKV0 = 2048
AB_LORA0 = 3584
AB_BA0 = 3840
NSA_COLS = 2048
NSA_KV0 = 1024
NSA_GATE0 = 1792


def _cparams(sem):
    return pltpu.CompilerParams(dimension_semantics=sem, vmem_limit_bytes=V7X_VMEM_LIMIT_BYTES)


def _rms(x, g):
    return x * lax.rsqrt(jnp.mean(x * x, axis=-1, keepdims=True) + NORM_EPS) * g


def _dot(a, b, precision=None):
    return jnp.dot(a, b, precision=precision, preferred_element_type=f32)


def _dot_nt(a, b, precision=None):
    return lax.dot_general(a, b, (((1,), (1,)), ((), ())), precision=precision, preferred_element_type=f32)


def _dot_tn(a, b, precision=None):
    return lax.dot_general(a, b, (((0,), (0,)), ((), ())), precision=precision, preferred_element_type=f32)


def _bdot(a, b):
    return _dot(a.astype(bf16), b.astype(bf16))


def _bdot_nt(a, b):
    return _dot_nt(a.astype(bf16), b.astype(bf16))


def _bdot_tn(a, b):
    return _dot_tn(a.astype(bf16), b.astype(bf16))


def _dot01(m01, x):
    m = m01.astype(bf16)
    hi = x.astype(bf16)
    rest = x - hi.astype(f32)
    mid = rest.astype(bf16)
    lo = (rest - mid.astype(f32)).astype(bf16)
    return _dot(m, hi) + _dot(m, mid) + _dot(m, lo)


def _iota2(shape, axis):
    return lax.broadcasted_iota(jnp.int32, shape, axis)


def _norm_matmul_body(x_ref, g_ref, w_ref, o_ref, u_ref):
    @pl.when(pl.program_id(1) == 0)
    def _():
        u_ref[...] = _rms(x_ref[...], g_ref[...]).astype(bf16)

    o_ref[...] = _dot(u_ref[...], w_ref[...])


def norm_matmul(x, g, w, tm=1024, tn=1024):
    m, d = x.shape
    n = w.shape[1]
    tm = min(tm, m)
    return pl.pallas_call(
        _norm_matmul_body,
        grid=(m // tm, n // tn),
        in_specs=[pl.BlockSpec((tm, d), lambda i, j: (i, 0)),
                  pl.BlockSpec((1, d), lambda i, j: (0, 0)),
                  pl.BlockSpec((d, tn), lambda i, j: (0, j))],
        out_specs=pl.BlockSpec((tm, tn), lambda i, j: (i, j)),
        out_shape=jax.ShapeDtypeStruct((m, n), f32),
        scratch_shapes=[pltpu.VMEM((tm, d), bf16)],
        compiler_params=_cparams(("parallel", "arbitrary")),
        name="norm_matmul",
    )(x, g.reshape(1, d), w)


def _out_res_norm_body(*refs, n_parts):
    a_refs = refs[:n_parts]
    w_refs = refs[n_parts:2 * n_parts]
    h_ref, g_ref, o_ref = refs[2 * n_parts:]
    y = _dot(a_refs[0][...].astype(bf16), w_refs[0][...])
    for a_ref, w_ref in zip(a_refs[1:], w_refs[1:]):
        y = y + _dot(a_ref[...].astype(bf16), w_ref[...])
    o_ref[...] = h_ref[...] + _rms(y, g_ref[...])


def out_res_norm(parts, w_parts, h, g, tm=512):
    m, d = h.shape
    tm = min(tm, m)
    n_parts = len(parts)
    in_specs = ([pl.BlockSpec((tm, p.shape[1]), lambda i: (i, 0)) for p in parts]
                + [pl.BlockSpec(w.shape, lambda i: (0, 0)) for w in w_parts]
                + [pl.BlockSpec((tm, d), lambda i: (i, 0)), pl.BlockSpec((1, d), lambda i: (0, 0))])
    return pl.pallas_call(
        functools.partial(_out_res_norm_body, n_parts=n_parts),
        grid=(m // tm,),
        in_specs=in_specs,
        out_specs=pl.BlockSpec((tm, d), lambda i: (i, 0)),
        out_shape=jax.ShapeDtypeStruct((m, d), f32),
        compiler_params=_cparams(("parallel",)),
        name="out_res_norm",
    )(*parts, *w_parts, h, g.reshape(1, d))


def _ffn_body(h_ref, g1_ref, wup_ref, wdn_ref, g2_ref, o_ref, u_ref, acc_ref):
    k = pl.program_id(1)

    @pl.when(k == 0)
    def _():
        u_ref[...] = _rms(h_ref[...], g1_ref[...]).astype(bf16)
        acc_ref[...] = jnp.zeros_like(acc_ref)

    a = _dot(u_ref[...], wup_ref[...])
    a = jnp.square(jnp.maximum(a, 0.0))
    acc_ref[...] += _dot(a.astype(bf16), wdn_ref[...])

    @pl.when(k == pl.num_programs(1) - 1)
    def _():
        o_ref[...] = h_ref[...] + _rms(acc_ref[...], g2_ref[...])


def ffn(h, g1, wup, wdn, g2, tm=1024, tf=1024):
    m, d = h.shape
    ff = wup.shape[1]
    tm = min(tm, m)
    return pl.pallas_call(
        _ffn_body,
        grid=(m // tm, ff // tf),
        in_specs=[pl.BlockSpec((tm, d), lambda i, k: (i, 0)),
                  pl.BlockSpec((1, d), lambda i, k: (0, 0)),
                  pl.BlockSpec((d, tf), lambda i, k: (0, k)),
                  pl.BlockSpec((tf, d), lambda i, k: (k, 0)),
                  pl.BlockSpec((1, d), lambda i, k: (0, 0))],
        out_specs=pl.BlockSpec((tm, d), lambda i, k: (i, 0)),
        out_shape=jax.ShapeDtypeStruct((m, d), f32),
        scratch_shapes=[pltpu.VMEM((tm, d), bf16), pltpu.VMEM((tm, d), f32)],
        compiler_params=_cparams(("parallel", "arbitrary")),
        name="ffn",
    )(h, g1.reshape(1, d), wup, wdn, g2.reshape(1, d))


def _neumann_inverses(n_mats, size):
    eye = (_iota2((size, size), 0) == _iota2((size, size), 1)).astype(f32)
    ts = [eye + n for n in n_mats]
    ps = list(n_mats)
    for _ in range(int(math.log2(size)) - 1):
        ps = [_bdot(p, p) for p in ps]
        ts = [t + _bdot(t, p) for t, p in zip(ts, ps)]
    return ts


def _gdn_body(q_ref, k_ref, v_ref, z_ref, ba_ref, cw_ref, arow_ref, dtrow_ref, nw_ref, o_ref,
              xp_ref, qkv_ref, s_ref, *, tt):
    c = GDN_CHUNK
    d = GDN_D
    heads = range(GDN_HEADS)
    width = GDN_HEADS * d

    @pl.when(pl.program_id(1) == 0)
    def _():
        xp_ref[:, pl.ds(0, SUBLANES), :] = jnp.zeros((3, SUBLANES, width), f32)
        s_ref[...] = jnp.zeros_like(s_ref)

    conv_rows = min(tt, 64)
    for idx, ref in enumerate((q_ref, k_ref, v_ref)):
        xp_ref[idx, pl.ds(SUBLANES, tt), :] = ref[0]
        for h in heads:
            cols = slice(h * d, (h + 1) * d)
            w = cw_ref[idx, :, cols]
            for r0 in range(0, tt, conv_rows):
                y = xp_ref[idx, pl.ds(SUBLANES + r0, conv_rows), cols] * w[GDN_CONV - 1:GDN_CONV, :]
                for j in range(GDN_CONV - 1):
                    y = y + xp_ref[idx, pl.ds(SUBLANES - (GDN_CONV - 1) + j + r0, conv_rows), cols] * w[j:j + 1, :]
                y = y * jax.nn.sigmoid(y)
                if idx < 2:
                    y = y * lax.rsqrt(jnp.sum(y * y, axis=-1, keepdims=True) + 1e-6)
                if idx == 0:
                    y = y * (d ** -0.5)
                qkv_ref[idx, pl.ds(r0, conv_rows), cols] = y
        xp_ref[idx, pl.ds(0, SUBLANES), :] = xp_ref[idx, pl.ds(tt, SUBLANES), :]

    row = _iota2((c, c), 0)
    col = _iota2((c, c), 1)
    tril = row >= col
    strict = row > col
    eye = row == col
    cum_l = tril.astype(f32)
    last_row = _iota2((c, 1), 0) == c - 1

    group = GDN_CHUNK_GROUP
    nh = GDN_HEADS

    def chunk_group(gi, carry):
        rows = [pl.ds(pl.multiple_of((gi * group + j) * c, c), c) for j in range(group)]
        sig, gcum = [], []
        for r in rows:
            ba = ba_ref[0, r, :]
            sig.append(jax.nn.sigmoid(ba))
            gcum.append(_dot01(cum_l, -jnp.exp(arow_ref[...]) * jax.nn.softplus(ba + dtrow_ref[...])))
        units = lambda f: [f(j, h) for j in range(group) for h in heads]
        qn = units(lambda j, h: qkv_ref[0, rows[j], h * d:(h + 1) * d])
        kn = units(lambda j, h: qkv_ref[1, rows[j], h * d:(h + 1) * d])
        vv = units(lambda j, h: qkv_ref[2, rows[j], h * d:(h + 1) * d])
        beta = units(lambda j, h: sig[j][:, h:h + 1])
        gc = units(lambda j, h: gcum[j][:, nh + h:nh + h + 1])
        gc_row = [jnp.sum(jnp.where(eye, jnp.broadcast_to(g, (c, c)), 0.0), axis=0, keepdims=True) for g in gc]
        gc_last = [jnp.sum(jnp.where(last_row, g, 0.0), axis=0, keepdims=True) for g in gc]
        decay = [jnp.exp(jnp.where(tril, g - gr, -jnp.inf)) for g, gr in zip(gc, gc_row)]
        knb = [k.astype(bf16) for k in kn]
        kk = [_dot_nt(k, k) for k in knb]
        qk = [_dot_nt(q.astype(bf16), k) for q, k in zip(qn, knb)]
        t_inv = _neumann_inverses([-jnp.where(strict, b * x * dc, 0.0) for b, x, dc in zip(beta, kk, decay)], c)
        egc = [jnp.exp(g) for g in gc]
        u = [_bdot(t, v * b) for t, v, b in zip(t_inv, vv, beta)]
        w = [_bdot(t, k * (b * e)).astype(bf16) for t, k, b, e in zip(t_inv, kn, beta, egc)]
        intra = [jnp.where(tril, x * dc, 0.0).astype(bf16) for x, dc in zip(qk, decay)]
        q_g = [(q * e).astype(bf16) for q, e in zip(qn, egc)]
        k_g = [(k * jnp.exp(gl - g)).astype(bf16) for k, gl, g in zip(kn, gc_last, gc)]
        state = [s_ref[h] for h in heads]
        for j in range(group):
            sl = slice(j * nh, (j + 1) * nh)
            sb = [x.astype(bf16) for x in state]
            v_new = [(x - _dot(y, z)).astype(bf16) for x, y, z in zip(u[sl], w[sl], sb)]
            o = [_dot(q, z) + _dot(a, vn) for q, z, a, vn in zip(q_g[sl], sb, intra[sl], v_new)]
            state = [s * jnp.exp(gl) + _dot_tn(k, vn) for s, gl, k, vn in zip(state, gc_last[sl], k_g[sl], v_new)]
            for h in heads:
                z = z_ref[0, rows[j], h * d:(h + 1) * d]
                o_ref[0, rows[j], h * d:(h + 1) * d] = _rms(o[h], nw_ref[...]) * (z * jax.nn.sigmoid(z))
        for h in heads:
            s_ref[h] = state[h]
        return carry

    lax.fori_loop(0, tt // (c * group), chunk_group, 0)


def gdn_mixer(f3, conv_w, arow, dtrow, norm_w, tt=512):
    b, t, _ = f3.shape
    tt = min(tt, t)
    width = GDN_HEADS * GDN_D
    col = lambda j: pl.BlockSpec((1, tt, width), lambda bi, ti, j=j: (bi, ti, j))
    return pl.pallas_call(
        functools.partial(_gdn_body, tt=tt),
        grid=(b, t // tt),
        in_specs=[col(0), col(1), col(2), col(3),
                  pl.BlockSpec((1, tt, LANES), lambda bi, ti: (bi, ti, AB_BA0 // LANES)),
                  pl.BlockSpec((3, GDN_CONV, width), lambda bi, ti: (0, 0, 0)),
                  pl.BlockSpec((1, LANES), lambda bi, ti: (0, 0)),
                  pl.BlockSpec((1, LANES), lambda bi, ti: (0, 0)),
                  pl.BlockSpec((1, GDN_D), lambda bi, ti: (0, 0))],
        out_specs=pl.BlockSpec((1, tt, width), lambda bi, ti: (bi, ti, 0)),
        out_shape=jax.ShapeDtypeStruct((b, t, width), f32),
        scratch_shapes=[pltpu.VMEM((3, tt + SUBLANES, width), f32),
                        pltpu.VMEM((3, tt, width), f32),
                        pltpu.VMEM((GDN_HEADS, GDN_D, GDN_D), f32)],
        compiler_params=_cparams(("parallel", "arbitrary")),
        name="gdn",
    )(f3, f3, f3, f3, f3, conv_w, arow, dtrow, norm_w)


def _rwkv_prep_body(r_ref, k_ref, v_ref, l_ref, mu_ref, w0_ref, w2_ref, a0_ref, a2_ref, g2_ref, kk_ref, ka_ref,
                    ro_ref, lw_ref, k2_ref, vo_ref, kko_ref, ao_ref, go_ref, xp_ref, *, tt):
    @pl.when(pl.program_id(1) == 0)
    def _():
        xp_ref[pl.ds(0, 8), :] = jnp.zeros((8, xp_ref.shape[1]), f32)

    w = RWKV_W
    xp_ref[pl.ds(8, tt), 0:w] = r_ref[0]
    xp_ref[pl.ds(8, tt), w:2 * w] = k_ref[0]
    xp_ref[pl.ds(8, tt), 2 * w:3 * w] = v_ref[0]
    xp_ref[pl.ds(8, tt), 3 * w:] = l_ref[0]
    x = xp_ref[pl.ds(8, tt), :]
    x = x + (xp_ref[pl.ds(7, tt), :] - x) * mu_ref[...]
    xp_ref[pl.ds(0, 8), :] = xp_ref[pl.ds(tt, 8), :]

    r, k, v = x[:, 0:w], x[:, w:2 * w], x[:, 2 * w:3 * w]
    wd, ad, gd = x[:, 3 * w:3 * w + 64], x[:, 3 * w + 64:3 * w + 128], x[:, 3 * w + 128:]
    w_log = -jax.nn.softplus(-(w0_ref[...] + _dot(jnp.tanh(wd), w2_ref[...], HI))) - 0.5
    a = jax.nn.sigmoid(a0_ref[...] + _dot(ad, a2_ref[...], HI))
    ro_ref[0] = r
    lw_ref[0] = -jnp.exp(w_log)
    k2_ref[0] = k * (1.0 + (a - 1.0) * ka_ref[...])
    vo_ref[0] = v
    kko_ref[0] = k * kk_ref[...]
    ao_ref[0] = a
    go_ref[0] = _dot(jax.nn.sigmoid(gd), g2_ref[...], HI)


def rwkv_prep(f3, mu, w0, w2, a0, a2, g2, k_k, k_a, tt=256):
    b, t, _ = f3.shape
    tt = min(tt, t)
    w = RWKV_W
    wide = 3 * w + 256
    row = lambda n: pl.BlockSpec((1, n), lambda bi, ti: (0, 0))
    full = lambda a: pl.BlockSpec(a.shape, lambda bi, ti: (0, 0))
    out = pl.BlockSpec((1, tt, w), lambda bi, ti: (bi, ti, 0))
    return pl.pallas_call(
        functools.partial(_rwkv_prep_body, tt=tt),
        grid=(b, t // tt),
        in_specs=[pl.BlockSpec((1, tt, w), lambda bi, ti: (bi, ti, AB_RWKV0 // w)),
                  pl.BlockSpec((1, tt, w), lambda bi, ti: (bi, ti, AB_RWKV0 // w + 1)),
                  pl.BlockSpec((1, tt, w), lambda bi, ti: (bi, ti, AB_RWKV0 // w + 2)),
                  pl.BlockSpec((1, tt, 256), lambda bi, ti: (bi, ti, AB_LORA0 // 256)),
                  row(wide), row(w), full(w2), row(w), full(a2), full(g2), row(w), row(w)],
        out_specs=[out] * 7,
        out_shape=[jax.ShapeDtypeStruct((b, t, w), f32)] * 7,
        scratch_shapes=[pltpu.VMEM((tt + 8, wide), f32)],
        compiler_params=_cparams(("parallel", "arbitrary")),
        name="rwkv_prep",
    )(f3, f3, f3, f3, mu.reshape(1, wide), w0.reshape(1, w), w2, a0.reshape(1, w), a2, g2,
      k_k.reshape(1, w), k_a.reshape(1, w))


def _rwkv_scan_body(r_ref, lw_ref, k2_ref, v_ref, kk_ref, a_ref, g_ref, rk_ref, lnw_ref, lnb_ref, o_ref, s_ref,
                    *, tt):
    c = RWKV_CHUNK
    n = RWKV_N

    @pl.when(pl.program_id(1) == 0)
    def _():
        s_ref[...] = jnp.zeros_like(s_ref)

    row = _iota2((c, c), 0)
    col = _iota2((c, c), 1)
    tril = row >= col
    strict = row > col
    cum_l = tril.astype(f32)
    last = _iota2((c, 1), 0) == c - 1

    heads = range(RWKV_HEADS)
    per_head = lambda x: [x[:, h * n:(h + 1) * n] for h in heads]

    group = RWKV_CHUNK_GROUP

    def chunk_group(gi, carry):
        rows = [pl.ds(pl.multiple_of((gi * group + j) * c, c), c) for j in range(group)]
        units = lambda f: [x for r in rows for x in f(r)]
        lw_all = [lw_ref[0, r, :] for r in rows]
        p_all = [_dot01(cum_l, x) for x in lw_all]
        em_all = [jnp.exp(-p) for p in p_all]
        rh = units(lambda r: per_head(r_ref[0, r, :]))
        k2 = units(lambda r: per_head(k2_ref[0, r, :]))
        vh = units(lambda r: per_head(v_ref[0, r, :]))
        r_t = [x for r, p in zip(rows, p_all) for x in per_head(r_ref[0, r, :] * jnp.exp(p))]
        k_t = [x for r, e in zip(rows, em_all) for x in per_head(k2_ref[0, r, :] * e)]
        e_prev = [x for p, lw in zip(p_all, lw_all) for x in per_head(jnp.exp(p - lw))]
        lr_em = [x for r, e in zip(rows, em_all) for x in per_head(a_ref[0, r, :] * e)]
        kk = [x * lax.rsqrt(jnp.sum(x * x, axis=-1, keepdims=True) + 1e-6)
              for x in units(lambda r: per_head(kk_ref[0, r, :]))]
        a_t = [-x * e for x, e in zip(kk, e_prev)]
        b_t = [x * e for x, e in zip(kk, lr_em)]
        m_ab = [jnp.where(strict, _bdot_nt(a, b), 0.0) for a, b in zip(a_t, b_t)]
        m_ak = [jnp.where(strict, _bdot_nt(a, k), 0.0) for a, k in zip(a_t, k_t)]
        a_rb = [jnp.where(tril, _bdot_nt(r, b), 0.0) for r, b in zip(r_t, b_t)]
        a_rk = [jnp.where(tril, _bdot_nt(r, k), 0.0) for r, k in zip(r_t, k_t)]
        t_inv = _neumann_inverses(m_ab, c)
        w1 = [_bdot(t, a) for t, a in zip(t_inv, a_t)]
        mv = [_bdot(m, v) for m, v in zip(m_ak, vh)]
        u2 = [_bdot(t, x) for t, x in zip(t_inv, mv)]
        y_v = [_bdot(a, v) for a, v in zip(a_rk, vh)]
        ds_v = [_bdot_tn(v, k) for v, k in zip(vh, k_t)]
        decay_last = [x for p in p_all
                      for x in per_head(jnp.exp(jnp.sum(jnp.where(last, p, 0.0), axis=0, keepdims=True)))]
        g_h = units(lambda r: per_head(g_ref[0, r, :]))
        bonus = [jnp.sum(r * k * rk_ref[:, (i % RWKV_HEADS) * n:(i % RWKV_HEADS + 1) * n], axis=-1, keepdims=True) * v
                 for i, (r, k, v) in enumerate(zip(rh, k2, vh))]
        state = [s_ref[h] for h in heads]
        for j in range(group):
            sl_u = slice(j * RWKV_HEADS, (j + 1) * RWKV_HEADS)
            u = [_bdot_nt(w, s) + x for w, s, x in zip(w1[sl_u], state, u2[sl_u])]
            y_s = [_bdot_nt(r, s) for r, s in zip(r_t[sl_u], state)]
            y = [ys + _bdot(a, x) + yv for ys, a, x, yv in zip(y_s, a_rb[sl_u], u, y_v[sl_u])]
            ds_u = [_bdot_tn(x, b) for x, b in zip(u, b_t[sl_u])]
            state = [(s + du + dv) * dl for s, du, dv, dl in zip(state, ds_u, ds_v[sl_u], decay_last[sl_u])]
            for h in heads:
                sl = slice(h * n, (h + 1) * n)
                mean = jnp.mean(y[h], axis=-1, keepdims=True)
                yc = y[h] - mean
                var = jnp.mean(yc * yc, axis=-1, keepdims=True)
                yn = yc * lax.rsqrt(var + RWKV_LN_EPS) * lnw_ref[:, sl] + lnb_ref[:, sl]
                o_ref[0, rows[j], sl] = (yn + bonus[j * RWKV_HEADS + h]) * g_h[j * RWKV_HEADS + h]
        for h in heads:
            s_ref[h] = state[h]
        return carry

    lax.fori_loop(0, tt // (c * group), chunk_group, 0)


def rwkv_scan(r, lw, k2, v, kk, a, g, r_k, ln_w, ln_b, tt=256):
    b, t, w = r.shape
    tt = min(tt, t)
    blk = pl.BlockSpec((1, tt, w), lambda bi, ti: (bi, ti, 0))
    row = pl.BlockSpec((1, w), lambda bi, ti: (0, 0))
    return pl.pallas_call(
        functools.partial(_rwkv_scan_body, tt=tt),
        grid=(b, t // tt),
        in_specs=[blk] * 7 + [row] * 3,
        out_specs=blk,
        out_shape=jax.ShapeDtypeStruct((b, t, w), f32),
        scratch_shapes=[pltpu.VMEM((RWKV_HEADS, RWKV_N, RWKV_N), f32)],
        compiler_params=_cparams(("parallel", "arbitrary")),
        name="rwkv_scan",
    )(r, lw, k2, v, kk, a, g, r_k.reshape(1, w), ln_w.reshape(1, w), ln_b.reshape(1, w))


def _rope128(x, cos, sin, sign_lo, sign_hi):
    r_hi = pltpu.roll(x, ROPE_HALF, 1)
    r_lo = pltpu.roll(x, LANES - ROPE_HALF, 1)
    return x * cos + (r_lo * sign_lo + r_hi * sign_hi) * sin


def _rope_tables(pos, freq_row):
    ang = pos * freq_row
    m = _iota2((1, LANES), 1) % NSA_HD
    sign_lo = jnp.where(m < ROPE_HALF, -1.0, 0.0).astype(f32)
    sign_hi = jnp.where((m >= ROPE_HALF) & (m < ROPE_DIM), 1.0, 0.0).astype(f32)
    return jnp.cos(ang), jnp.sin(ang), sign_lo, sign_hi


def _nsa_prep_body(q_ref, kc_i, vc_i, ks_i, vs_i, kw_i, vw_i, pos_ref, freq_ref,
                   qo_ref, kc_ref, vc_ref, ks_ref, vs_ref, kw_ref, vw_ref):
    cos, sin, s_lo, s_hi = _rope_tables(pos_ref[0], freq_ref[...])
    scale = NSA_HD ** -0.5 * math.log2(math.e)
    n_blk = q_ref.shape[1] // NSA_QBLOCK
    for c in range(NSA_HEADS // 2):
        xt = (_rope128(q_ref[0, :, c * LANES:(c + 1) * LANES], cos, sin, s_lo, s_hi) * scale).T
        for r in range(2):
            head = 2 * c + r
            g, hl = head // NSA_HPG, head % NSA_HPG
            for i in range(n_blk):
                qo_ref[0, g, i, :, hl * NSA_QBLOCK:(hl + 1) * NSA_QBLOCK] = (
                    xt[r * NSA_HD:(r + 1) * NSA_HD, i * NSA_QBLOCK:(i + 1) * NSA_QBLOCK].astype(bf16))

    def split(src, ref, rope, dtype):
        x = src[0]
        if rope:
            x = _rope128(x, cos, sin, s_lo, s_hi)
        ref[0, 0] = x[:, :NSA_HD].astype(dtype)
        ref[0, 1] = x[:, NSA_HD:].astype(dtype)

    def split_t(src, ref):
        xt = src[0].T
        ref[0, 0] = xt[:NSA_HD].astype(bf16)
        ref[0, 1] = xt[NSA_HD:].astype(bf16)

    split(kc_i, kc_ref, False, f32)
    split(vc_i, vc_ref, False, f32)
    tt = ks_i.shape[1]
    tok = pl.program_id(1) * tt + _iota2((tt, NSA_HD), 0)
    block_onehot = (tok // SEL_BLOCK == _iota2((tt, NSA_HD), 1)).astype(f32)
    ks = _rope128(ks_i[0], cos, sin, s_lo, s_hi)
    for g in range(NSA_GROUPS):
        ks_ref[0, g] = jnp.concatenate([ks[:, g * NSA_HD:(g + 1) * NSA_HD], block_onehot], axis=1).astype(bf16)
    split_t(vs_i, vs_ref)
    split(kw_i, kw_ref, True, bf16)
    split_t(vw_i, vw_ref)


def nsa_prep(f3, pos3, freq_row, tt=512):
    b, t, _ = f3.shape
    assert t // SEL_BLOCK <= NSA_HD, "the selection-block one-hot shares the key tile's second 64 lanes"
    tt = min(tt, t)
    g = NSA_GROUPS
    n_blk = tt // NSA_QBLOCK
    kv_in = [pl.BlockSpec((1, tt, LANES), lambda bi, ti, c=NSA_KV0 // LANES + i: (bi, ti, c)) for i in range(6)]
    kv_spec = pl.BlockSpec((1, g, tt, NSA_HD), lambda bi, ti: (bi, 0, ti, 0))
    kvt_spec = pl.BlockSpec((1, g, NSA_HD, tt), lambda bi, ti: (bi, 0, 0, ti))
    kv32 = jax.ShapeDtypeStruct((b, g, t, NSA_HD), f32)
    kv16 = jax.ShapeDtypeStruct((b, g, t, NSA_HD), bf16)
    kvt16 = jax.ShapeDtypeStruct((b, g, NSA_HD, t), bf16)
    q_lanes = NSA_HPG * NSA_QBLOCK
    return pl.pallas_call(
        _nsa_prep_body,
        grid=(b, t // tt),
        in_specs=[pl.BlockSpec((1, tt, NSA_HEADS * NSA_HD), lambda bi, ti: (bi, ti, 0))] + kv_in
                 + [pl.BlockSpec((1, tt, 1), lambda bi, ti: (bi, ti, 0)),
                    pl.BlockSpec((1, LANES), lambda bi, ti: (0, 0))],
        out_specs=[pl.BlockSpec((1, g, n_blk, NSA_HD, q_lanes), lambda bi, ti: (bi, 0, ti, 0, 0)),
                   kv_spec, kv_spec, pl.BlockSpec((1, g, tt, 2 * NSA_HD), lambda bi, ti: (bi, 0, ti, 0)),
                   kvt_spec, kv_spec, kvt_spec],
        out_shape=[jax.ShapeDtypeStruct((b, g, t // NSA_QBLOCK, NSA_HD, q_lanes), bf16),
                   kv32, kv32, jax.ShapeDtypeStruct((b, g, t, 2 * NSA_HD), bf16), kvt16, kv16, kvt16],
        compiler_params=_cparams(("parallel", "parallel")),
        name="nsa_prep",
    )(f3, f3, f3, f3, f3, f3, f3, pos3, freq_row)


def _nsa_compress_body(kc_ref, vc_ref, pek_ref, w1k_ref, w2k_ref, pev_ref, w1v_ref, w2v_ref, pos_ref, freq_ref,
                       ko_ref, vo_ref):
    half = CMP_STRIDE * NSA_HD
    nrow = kc_ref.shape[2]
    last_row = _iota2((nrow, 1), 0) == nrow - 1

    def hidden(x, pe_ref, w1_ref):
        lo = _dot((x + pe_ref[:, :half]).astype(bf16), w1_ref[:half, :])
        hi = _dot((x + pe_ref[:, half:]).astype(bf16), w1_ref[half:, :])
        hi = jnp.where(last_row, 0.0, pltpu.roll(hi, nrow - 1, 0))
        hid = lo + hi
        return (hid * jax.nn.sigmoid(hid)).astype(bf16)

    ks = [_dot(hidden(kc_ref[0, g], pek_ref, w1k_ref), w2k_ref[...]) for g in range(NSA_GROUPS)]
    cos, sin, s_lo, s_hi = _rope_tables(pos_ref[0], freq_ref[...])
    kr = _rope128(jnp.concatenate(ks, axis=-1), cos, sin, s_lo, s_hi)
    for g in range(NSA_GROUPS):
        ko_ref[0, g] = kr[:, g * NSA_HD:(g + 1) * NSA_HD].astype(bf16)
        vo_ref[0, g] = _dot_nt(w2v_ref[...], hidden(vc_ref[0, g], pev_ref, w1v_ref)).astype(bf16)


def nsa_compress(kc4, vc4, pe_k, w1_k, w2_k, pe_v, w1_v, w2_vt, cpos3, freq_row):
    b, g, nrow, wide = kc4.shape
    full = lambda a: pl.BlockSpec(a.shape, lambda bi: (0,) * a.ndim)
    blk = pl.BlockSpec((1, g, nrow, wide), lambda bi: (bi, 0, 0, 0))
    out = pl.BlockSpec((1, g, nrow, NSA_HD), lambda bi: (bi, 0, 0, 0))
    out_t = pl.BlockSpec((1, g, NSA_HD, nrow), lambda bi: (bi, 0, 0, 0))
    return pl.pallas_call(
        _nsa_compress_body,
        grid=(b,),
        in_specs=[blk, blk, full(pe_k), full(w1_k), full(w2_k), full(pe_v), full(w1_v), full(w2_vt),
                  pl.BlockSpec((1, nrow, 1), lambda bi: (bi, 0, 0)), pl.BlockSpec((1, LANES), lambda bi: (0, 0))],
        out_specs=[out, out_t],
        out_shape=[jax.ShapeDtypeStruct((b, g, nrow, NSA_HD), bf16),
                   jax.ShapeDtypeStruct((b, g, NSA_HD, nrow), bf16)],
        compiler_params=_cparams(("parallel",)),
        name="nsa_compress",
    )(kc4, vc4, pe_k, w1_k, w2_k, pe_v, w1_v, w2_vt, cpos3, freq_row)


def _nsa_attn_body(qt_ref, kc_ref, vct_ref, ks_ref, vst_ref, kw_ref, vwt_ref, gate_ref, ovt_ref, dbias_ref,
                   wbias_ref, rep_ref, o_ref, score_ref, s_ref, acc_ref, sw_ref, *, key_tile):
    qb_n = NSA_QBLOCK
    blk_lanes = NSA_HPG * qb_n
    lanes = NSA_QPAIR * blk_lanes
    pair = pl.program_id(2)
    qt = jnp.concatenate([qt_ref[0, 0, a] for a in range(NSA_QPAIR)], axis=1)
    lane = _iota2((1, lanes), 1)
    t_lane = (pair * NSA_QPAIR + lane // blk_lanes) * qb_n + lane % qb_n

    def softmax_t(s):
        mx = jnp.max(s, axis=0, keepdims=True)
        mx = jnp.where(mx > -jnp.inf, mx, 0.0)
        e = jnp.exp2(s - mx)
        return e, jnp.sum(e, axis=0, keepdims=True), mx

    diag0 = pair * NSA_QPAIR
    n_tiles = (diag0 * SEL_BLOCK + key_tile - 1) // key_tile
    last_tile = ks_ref.shape[2] // key_tile - 1

    gt = jax.nn.sigmoid(gate_ref[0]).T
    gate = lambda br: jnp.concatenate([gt[3 * h + br:3 * h + br + 1, a * qb_n:(a + 1) * qb_n]
                                       for a in range(NSA_QPAIR) for h in range(NSA_HPG)], axis=1)
    gates = [gate(br) for br in range(3)]

    s_cmp = _dot(kc_ref[0, 0], qt)
    span = WINDOW + NSA_QPAIR * qb_n
    w0 = pl.multiple_of(pair * NSA_QPAIR * qb_n, LANES)
    sw_ref[...] = _dot(kw_ref[0, 0, pl.ds(w0, span), :], qt) + wbias_ref[...]
    k0 = pl.multiple_of(diag0 * SEL_BLOCK, NSA_QPAIR * SEL_BLOCK)
    s_diag = _dot(ks_ref[0, 0, pl.ds(k0, NSA_QPAIR * SEL_BLOCK), 0:NSA_HD], qt) + dbias_ref[...]

    n_cmp = kc_ref.shape[2]
    cmp_end = _iota2((n_cmp, 1), 0) * CMP_STRIDE + (CMP_LEN - 1)
    e_c, den_c, _ = softmax_t(jnp.where(cmp_end <= t_lane, s_cmp, -jnp.inf))
    p_c = e_c * (1.0 / jnp.maximum(den_c, 1e-30))
    o_c = _dot(vct_ref[0, 0], p_c.astype(bf16))

    n_sel = ovt_ref.shape[0]
    p_pair = []
    for a in range(NSA_QPAIR):
        acc = p_c[:, a * blk_lanes:a * blk_lanes + LANES]
        for c in range(1, blk_lanes // LANES):
            acc = acc + p_c[:, a * blk_lanes + c * LANES:a * blk_lanes + (c + 1) * LANES]
        p_pair.append(acc[:, :qb_n] + acc[:, qb_n:])
    imp = _dot(ovt_ref[...], jnp.concatenate(p_pair, axis=1), HI)
    blk = _iota2((n_sel, NSA_QPAIR * qb_n), 0)
    cur = pair * NSA_QPAIR + _iota2((n_sel, NSA_QPAIR * qb_n), 1) // qb_n
    valid = blk <= cur
    forced = (blk == 0) | (blk == cur) | (blk == cur - 1)
    score = jnp.where(valid, jnp.where(forced, jnp.inf, imp), -jnp.inf)
    score_ref[...] = score

    e_d, l_s, m_s = softmax_t(s_diag)
    acc_ref[...] = _dot(vst_ref[0, 0, :, pl.ds(k0, NSA_QPAIR * SEL_BLOCK)], e_d.astype(bf16))

    @pl.when(w0 < WINDOW)
    def _():
        is_pad = _iota2((WINDOW, 1), 0) < WINDOW - w0
        sw_ref[pl.ds(0, WINDOW), :] = jnp.where(is_pad, -jnp.inf, sw_ref[pl.ds(0, WINDOW), :])

    def rank_step(jp, cnt):
        other = score_ref[pl.ds(jp, 1), :]
        ahead = (other > score) | ((other == score) & (blk > jp))
        return cnt + ahead.astype(jnp.int32)

    def rank_step2(jj, cnt):
        return rank_step(2 * jj + 1, rank_step(2 * jj, cnt))

    n_cand = pair * NSA_QPAIR + NSA_QPAIR
    cnt = lax.fori_loop(0, jnp.where(n_cand > SEL_TOPN, n_cand // 2, 0), rank_step2,
                        jnp.zeros((n_sel, NSA_QPAIR * qb_n), jnp.int32))
    bias = jnp.where((cnt < SEL_TOPN) & (blk < diag0), 0.0, NSA_MASKED).astype(bf16)
    bias = _dot(bias, rep_ref[...]).astype(bf16)
    if n_sel < NSA_HD:
        bias = jnp.concatenate([bias, jnp.zeros((NSA_HD - n_sel, lanes), bf16)], axis=0)
    qx = jnp.concatenate([qt, bias], axis=0)

    def score_tile(kt):
        kk0 = pl.multiple_of(kt * key_tile, key_tile)
        return _dot(ks_ref[0, 0, pl.ds(kk0, key_tile), :], qx)

    s_ref[0] = score_tile(0)

    e_w, l_w, _ = softmax_t(sw_ref[...])
    o_w = _dot(vwt_ref[0, 0, :, pl.ds(w0, span)], e_w.astype(bf16))

    def half_step(kt, slot, carry):
        m_old, l_old = carry
        s_ref[1 - slot] = score_tile(jnp.minimum(kt + 1, last_tile))
        sm = s_ref[slot]
        mx = jnp.maximum(m_old, jnp.max(sm, axis=0, keepdims=True))
        alpha = jnp.exp2(m_old - mx)
        p = jnp.exp2(sm - mx)
        kk0 = pl.multiple_of(kt * key_tile, key_tile)
        acc_ref[...] = alpha * acc_ref[...] + _dot(vst_ref[0, 0, :, pl.ds(kk0, key_tile)], p.astype(bf16))
        return mx, alpha * l_old + jnp.sum(p, axis=0, keepdims=True)

    def pair_step(pi, carry):
        return half_step(2 * pi + 1, 1, half_step(2 * pi, 0, carry))

    m_s, l_s = lax.fori_loop(0, (n_tiles + 1) // 2, pair_step, (m_s, l_s))

    o_t = gates[0] * o_c + (gates[1] * (1.0 / l_s)) * acc_ref[...] + (gates[2] * (1.0 / l_w)) * o_w
    o_b = o_t.astype(bf16)
    q_idx = _iota2((qb_n, LANES), 0)
    l_idx = _iota2((qb_n, LANES), 1)
    pick = [(l_idx == q_idx + r * qb_n).astype(bf16) for r in range(LANES // qb_n)]
    for a in range(NSA_QPAIR):
        for c in range(blk_lanes // LANES):
            piece = o_b[:, a * blk_lanes + c * LANES:a * blk_lanes + (c + 1) * LANES]
            for r in range(LANES // qb_n):
                h = c * (LANES // qb_n) + r
                o_ref[0, a * qb_n:(a + 1) * qb_n, h * NSA_HD:(h + 1) * NSA_HD] = (
                    _dot_nt(pick[r], piece).astype(bf16))


def nsa_attention(qt, kc, vct, ks, vst, kw, vwt, f3, overlap_t, key_tile=256):
    b, g, n_q, d, blk_lanes = qt.shape
    t = n_q * NSA_QBLOCK
    key_tile = min(key_tile, t)
    assert (t // key_tile) % 2 == 0, "an odd tile count borrows the (fully unselected) tile after the last one"
    n_cmp = kc.shape[2]
    n_sel = overlap_t.shape[0]
    lanes = NSA_QPAIR * blk_lanes
    q_rows = NSA_QPAIR * NSA_QBLOCK
    off = (jnp.arange(lanes) // blk_lanes) * NSA_QBLOCK + jnp.arange(lanes) % NSA_QBLOCK
    r_d = jnp.arange(NSA_QPAIR * SEL_BLOCK)[:, None]
    diag_bias = jnp.where(r_d <= off[None, :], 0.0, -jnp.inf).astype(f32)
    r_w = jnp.arange(WINDOW + q_rows)[:, None]
    win_bias = jnp.where((r_w > off[None, :]) & (r_w <= WINDOW + off[None, :]), 0.0, -jnp.inf).astype(f32)
    rep = (jnp.arange(q_rows)[:, None] == off[None, :]).astype(bf16)
    seq = pl.BlockSpec((1, 1, t, ks.shape[3]), lambda bi, gi, qi: (bi, gi, 0, 0))
    seq_t = pl.BlockSpec((1, 1, d, t), lambda bi, gi, qi: (bi, gi, 0, 0))
    const = lambda a: pl.BlockSpec(a.shape, lambda bi, gi, qi: (0, 0))
    return pl.pallas_call(
        functools.partial(_nsa_attn_body, key_tile=key_tile),
        grid=(b, g, n_q // NSA_QPAIR),
        in_specs=[pl.BlockSpec((1, 1, NSA_QPAIR, d, blk_lanes), lambda bi, gi, qi: (bi, gi, qi, 0, 0)),
                  pl.BlockSpec((1, 1, n_cmp, d), lambda bi, gi, qi: (bi, gi, 0, 0)),
                  pl.BlockSpec((1, 1, d, n_cmp), lambda bi, gi, qi: (bi, gi, 0, 0)),
                  seq, seq_t,
                  pl.BlockSpec((1, 1, t + WINDOW, d), lambda bi, gi, qi: (bi, gi, 0, 0)),
                  pl.BlockSpec((1, 1, d, t + WINDOW), lambda bi, gi, qi: (bi, gi, 0, 0)),
                  pl.BlockSpec((1, q_rows, LANES), lambda bi, gi, qi: (bi, qi, NSA_GATE0 // LANES + gi)),
                  const(overlap_t), const(diag_bias), const(win_bias), const(rep)],
        out_specs=pl.BlockSpec((1, q_rows, NSA_HPG * d), lambda bi, gi, qi: (bi, qi, gi)),
        out_shape=jax.ShapeDtypeStruct((b, t, g * NSA_HPG * d), bf16),
        scratch_shapes=[pltpu.VMEM((n_sel, q_rows), f32),
                        pltpu.VMEM((2, key_tile, lanes), f32), pltpu.VMEM((d, lanes), f32),
                        pltpu.VMEM((WINDOW + q_rows, lanes), f32)],
        compiler_params=_cparams(("parallel", "parallel", "arbitrary")),
        name="nsa_attn",
    )(qt, kc, vct, ks, vst, kw, vwt, f3, overlap_t, diag_bias, win_bias, rep)


def _place(cols, total, pieces):
    out = jnp.zeros((cols, total), f32)
    for start, mat in pieces:
        out = lax.dynamic_update_slice(out, mat.astype(f32), (0, start))
    return out


def _layer0(h, b, t, g_pre, w_in, w_out, g_post, conv, a_log, dt_bias, gnorm, mu, w0, w2, a0, a2, g2, k_k, k_a,
            r_k, ln_w, ln_b):
    gdn_w = 4 * GDN_HEADS * GDN_D
    w_pad = _place(D_MODEL, AB_COLS, [
        (0, w_in[:, :gdn_w]),
        (AB_BA0, w_in[:, gdn_w:gdn_w + 2 * GDN_HEADS]),
        (AB_RWKV0, w_in[:, gdn_w + 2 * GDN_HEADS:gdn_w + 2 * GDN_HEADS + 3 * RWKV_W]),
        (AB_LORA0, w_in[:, gdn_w + 2 * GDN_HEADS + 3 * RWKV_W:]),
    ]).astype(bf16)
    f3 = norm_matmul(h, g_pre, w_pad).reshape(b, t, AB_COLS)
    arow = jnp.zeros((1, LANES), f32).at[0, GDN_HEADS:2 * GDN_HEADS].set(a_log)
    dtrow = jnp.zeros((1, LANES), f32).at[0, GDN_HEADS:2 * GDN_HEADS].set(dt_bias)
    conv3 = conv.reshape(GDN_CONV, 3, GDN_HEADS * GDN_D).transpose(1, 0, 2)
    o_a = gdn_mixer(f3, conv3, arow, dtrow, gnorm.reshape(1, GDN_D))
    prep = rwkv_prep(f3, mu, w0, w2, a0, a2, g2, k_k, k_a)
    o_b = rwkv_scan(*prep, r_k, ln_w, ln_b)
    m = b * t
    n_a = GDN_HEADS * GDN_D
    return out_res_norm([o_a.reshape(m, n_a), o_b.reshape(m, RWKV_W)],
                        [w_out[:n_a].astype(bf16), w_out[n_a:].astype(bf16)], h, g_post)


def _layer1(h, b, t, positions, g_pre, w_in, w_out, g_post, pe_k, w1_k, w2_k, pe_v, w1_v, w2_v):
    qw = NSA_HEADS * NSA_HD
    kvw = 6 * NSA_GROUPS * NSA_HD
    gates = w_in[:, qw + kvw:].reshape(D_MODEL, NSA_GROUPS, NSA_HPG * 3)
    w_pad = _place(D_MODEL, NSA_COLS, [(0, w_in[:, :qw + kvw])]
                   + [(NSA_GATE0 + gi * LANES, gates[:, gi]) for gi in range(NSA_GROUPS)]).astype(bf16)
    f3 = norm_matmul(h, g_pre, w_pad).reshape(b, t, NSA_COLS)
    inv_freq = ROPE_THETA ** (-jnp.arange(ROPE_HALF, dtype=f32) * (2.0 / ROPE_DIM))
    lane = jnp.arange(LANES)
    freq_row = jnp.where(lane % NSA_HD < ROPE_DIM, inv_freq[lane % ROPE_HALF], 0.0).reshape(1, LANES).astype(f32)
    posf = positions.astype(f32)
    qt, kc, vc, ks, vst, kw, vwt = nsa_prep(f3, posf.reshape(b, t, 1), freq_row)
    nrow = t // CMP_STRIDE
    cpos = jnp.concatenate([posf[:, CMP_LEN - 1::CMP_STRIDE], posf[:, -1:]], axis=1).reshape(b, nrow, 1)
    flat = lambda a: a.reshape(b, NSA_GROUPS, nrow, CMP_STRIDE * NSA_HD)
    kcc, vcct = nsa_compress(flat(kc), flat(vc), pe_k.reshape(1, -1), w1_k.astype(bf16), w2_k.astype(bf16),
                             pe_v.reshape(1, -1), w1_v.astype(bf16), w2_v.T.astype(bf16), cpos, freq_row)
    n_sel = t // SEL_BLOCK
    c_start = jnp.arange(nrow) * CMP_STRIDE
    s_start = jnp.arange(n_sel) * SEL_BLOCK
    overlap_t = jnp.clip(jnp.minimum(c_start[None, :] + CMP_LEN, s_start[:, None] + SEL_BLOCK)
                         - jnp.maximum(c_start[None, :], s_start[:, None]), 0, None).astype(f32) / CMP_LEN
    kw_pad = jnp.pad(kw, ((0, 0), (0, 0), (WINDOW, 0), (0, 0)))
    vwt_pad = jnp.pad(vwt, ((0, 0), (0, 0), (0, 0), (WINDOW, 0)))
    o = nsa_attention(qt, kcc, vcct, ks, vst, kw_pad, vwt_pad, f3, overlap_t)
    return out_res_norm([o.reshape(b * t, qw)], [w_out.astype(bf16)], h, g_post)


def kernel(x, positions, norm_mix_pre, norm_mix_post, norm_ffn_pre, norm_ffn_post, w_ffn_up, w_ffn_down, ab_w_in,
           ab_w_out, gdn_conv, gdn_a_log, gdn_dt_bias, gdn_norm, rwkv_mu, rwkv_w0, rwkv_w2, rwkv_a0, rwkv_a2,
           rwkv_g2, rwkv_k_k, rwkv_k_a, rwkv_r_k, rwkv_ln_w, rwkv_ln_b, nsa_w_in, nsa_w_out, nsa_pe_k, nsa_w1_k,
           nsa_w2_k, nsa_pe_v, nsa_w1_v, nsa_w2_v):
    b, t, d = x.shape
    h = x.reshape(b * t, d)
    h = _layer0(h, b, t, norm_mix_pre[0], ab_w_in[0], ab_w_out[0], norm_mix_post[0], gdn_conv[0], gdn_a_log[0],
                gdn_dt_bias[0], gdn_norm[0], rwkv_mu[0], rwkv_w0[0], rwkv_w2[0], rwkv_a0[0], rwkv_a2[0],
                rwkv_g2[0], rwkv_k_k[0], rwkv_k_a[0], rwkv_r_k[0].reshape(-1), rwkv_ln_w[0], rwkv_ln_b[0])
    h = ffn(h, norm_ffn_pre[0], w_ffn_up[0].astype(bf16), w_ffn_down[0].astype(bf16), norm_ffn_post[0])
    h = _layer1(h, b, t, positions, norm_mix_pre[1], nsa_w_in[0], nsa_w_out[0], norm_mix_post[1], nsa_pe_k[0],
                nsa_w1_k[0], nsa_w2_k[0], nsa_pe_v[0], nsa_w1_v[0], nsa_w2_v[0])
    h = ffn(h, norm_ffn_pre[1], w_ffn_up[1].astype(bf16), w_ffn_down[1].astype(bf16), norm_ffn_post[1])
    return h.reshape(b, t, d)
```

```python
import functools
import math

import jax
import jax.numpy as jnp
from jax import lax
from jax.experimental import pallas as pl
from jax.experimental.pallas import tpu as pltpu

f32 = jnp.float32
bf16 = jnp.bfloat16
HI = lax.Precision.HIGHEST

V7X_VMEM_LIMIT_BYTES = 56 * 1024 * 1024
LANES = 128
SUBLANES = 8

D_MODEL = 1024
D_FF = 4 * D_MODEL
NORM_EPS = 1e-6
GDN_HEADS = 4
GDN_D = 128
GDN_CONV = 4
GDN_CHUNK = 128
GDN_CHUNK_GROUP = 4
RWKV_HEADS = 8
RWKV_N = 64
RWKV_W = RWKV_HEADS * RWKV_N
RWKV_CHUNK = 64
RWKV_CHUNK_GROUP = 4
RWKV_LN_EPS = 64e-5
RWKV_DECAY_LORA = 64
RWKV_LR_LORA = 64
RWKV_GATE_LORA = 128
NSA_HEADS = 16
NSA_GROUPS = 2
NSA_HPG = NSA_HEADS // NSA_GROUPS
NSA_HD = 64
CMP_LEN = 32
CMP_STRIDE = 16
CMP_HIDDEN = 256
SEL_BLOCK = 64
SEL_TOPN = 16
WINDOW = 512
NSA_QBLOCK = 64
NSA_QPAIR = 2
NSA_VT_ROWS = 80
NSA_MASKED = -(2.0 ** 126)
ROPE_THETA = 500000.0
ROPE_DIM = NSA_HD // 4
ROPE_HALF = ROPE_DIM // 2

AB_COLS = 4096
AB_RWKV0 = 2048
AB_LORA0 = 3584
AB_BA0 = 3840
NSA_COLS = 2048
NSA_KV0 = 1024
NSA_GATE0 = 1792


def _cparams(sem):
    return pltpu.CompilerParams(dimension_semantics=sem, vmem_limit_bytes=V7X_VMEM_LIMIT_BYTES)


def _rms(x, g):
    return x * lax.rsqrt(jnp.mean(x * x, axis=-1, keepdims=True) + NORM_EPS) * g


def _dot(a, b, precision=None):
    return jnp.dot(a, b, precision=precision, preferred_element_type=f32)


def _dot_nt(a, b, precision=None):
    return lax.dot_general(a, b, (((1,), (1,)), ((), ())), precision=precision, preferred_element_type=f32)


def _dot_tn(a, b, precision=None):
    return lax.dot_general(a, b, (((0,), (0,)), ((), ())), precision=precision, preferred_element_type=f32)


def _bdot(a, b):
    return _dot(a.astype(bf16), b.astype(bf16))


def _bdot_nt(a, b):
    return _dot_nt(a.astype(bf16), b.astype(bf16))


def _bdot_tn(a, b):
    return _dot_tn(a.astype(bf16), b.astype(bf16))


def _dot01(m01, x):
    m = m01.astype(bf16)
    hi = x.astype(bf16)
    rest = x - hi.astype(f32)
    mid = rest.astype(bf16)
    lo = (rest - mid.astype(f32)).astype(bf16)
    return _dot(m, hi) + _dot(m, mid) + _dot(m, lo)


def _iota2(shape, axis):
    return lax.broadcasted_iota(jnp.int32, shape, axis)


def _norm_matmul_body(x_ref, g_ref, w_ref, o_ref, u_ref):
    @pl.when(pl.program_id(1) == 0)
    def _():
        u_ref[...] = _rms(x_ref[...], g_ref[...]).astype(bf16)

    o_ref[...] = _dot(u_ref[...], w_ref[...])


def norm_matmul(x, g, w, tm=1024, tn=1024):
    m, d = x.shape
    n = w.shape[1]
    tm = min(tm, m)
    return pl.pallas_call(
        _norm_matmul_body,
        grid=(m // tm, n // tn),
        in_specs=[pl.BlockSpec((tm, d), lambda i, j: (i, 0)),
                  pl.BlockSpec((1, d), lambda i, j: (0, 0)),
                  pl.BlockSpec((d, tn), lambda i, j: (0, j))],
        out_specs=pl.BlockSpec((tm, tn), lambda i, j: (i, j)),
        out_shape=jax.ShapeDtypeStruct((m, n), f32),
        scratch_shapes=[pltpu.VMEM((tm, d), bf16)],
        compiler_params=_cparams(("parallel", "arbitrary")),
        name="norm_matmul",
    )(x, g.reshape(1, d), w)


def _out_res_norm_body(*refs, n_parts):
    a_refs = refs[:n_parts]
    w_refs = refs[n_parts:2 * n_parts]
    h_ref, g_ref, o_ref = refs[2 * n_parts:]
    y = _dot(a_refs[0][...].astype(bf16), w_refs[0][...])
    for a_ref, w_ref in zip(a_refs[1:], w_refs[1:]):
        y = y + _dot(a_ref[...].astype(bf16), w_ref[...])
    o_ref[...] = h_ref[...] + _rms(y, g_ref[...])


def out_res_norm(parts, w_parts, h, g, tm=512):
    m, d = h.shape
    tm = min(tm, m)
    n_parts = len(parts)
    in_specs = ([pl.BlockSpec((tm, p.shape[1]), lambda i: (i, 0)) for p in parts]
                + [pl.BlockSpec(w.shape, lambda i: (0, 0)) for w in w_parts]
                + [pl.BlockSpec((tm, d), lambda i: (i, 0)), pl.BlockSpec((1, d), lambda i: (0, 0))])
    return pl.pallas_call(
        functools.partial(_out_res_norm_body, n_parts=n_parts),
        grid=(m // tm,),
        in_specs=in_specs,
        out_specs=pl.BlockSpec((tm, d), lambda i: (i, 0)),
        out_shape=jax.ShapeDtypeStruct((m, d), f32),
        compiler_params=_cparams(("parallel",)),
        name="out_res_norm",
    )(*parts, *w_parts, h, g.reshape(1, d))


def _ffn_body(h_ref, g1_ref, wup_ref, wdn_ref, g2_ref, o_ref, u_ref, acc_ref):
    k = pl.program_id(1)

    @pl.when(k == 0)
    def _():
        u_ref[...] = _rms(h_ref[...], g1_ref[...]).astype(bf16)
        acc_ref[...] = jnp.zeros_like(acc_ref)

    a = _dot(u_ref[...], wup_ref[...])
    a = jnp.square(jnp.maximum(a, 0.0))
    acc_ref[...] += _dot(a.astype(bf16), wdn_ref[...])

    @pl.when(k == pl.num_programs(1) - 1)
    def _():
        o_ref[...] = h_ref[...] + _rms(acc_ref[...], g2_ref[...])


def ffn(h, g1, wup, wdn, g2, tm=1024, tf=1024):
    m, d = h.shape
    ff = wup.shape[1]
    tm = min(tm, m)
    return pl.pallas_call(
        _ffn_body,
        grid=(m // tm, ff // tf),
        in_specs=[pl.BlockSpec((tm, d), lambda i, k: (i, 0)),
                  pl.BlockSpec((1, d), lambda i, k: (0, 0)),
                  pl.BlockSpec((d, tf), lambda i, k: (0, k)),
                  pl.BlockSpec((tf, d), lambda i, k: (k, 0)),
                  pl.BlockSpec((1, d), lambda i, k: (0, 0))],
        out_specs=pl.BlockSpec((tm, d), lambda i, k: (i, 0)),
        out_shape=jax.ShapeDtypeStruct((m, d), f32),
        scratch_shapes=[pltpu.VMEM((tm, d), bf16), pltpu.VMEM((tm, d), f32)],
        compiler_params=_cparams(("parallel", "arbitrary")),
        name="ffn",
    )(h, g1.reshape(1, d), wup, wdn, g2.reshape(1, d))


def _neumann_inverses(n_mats, size):
    eye = (_iota2((size, size), 0) == _iota2((size, size), 1)).astype(f32)
    ts = [eye + n for n in n_mats]
    ps = list(n_mats)
    for _ in range(int(math.log2(size)) - 1):
        ps = [_bdot(p, p) for p in ps]
        ts = [t + _bdot(t, p) for t, p in zip(ts, ps)]
    return ts


def _gdn_body(q_ref, k_ref, v_ref, z_ref, ba_ref, cw_ref, arow_ref, dtrow_ref, nw_ref, o_ref,
              xp_ref, qkv_ref, s_ref, *, tt):
    c = GDN_CHUNK
    d = GDN_D
    heads = range(GDN_HEADS)
    width = GDN_HEADS * d

    @pl.when(pl.program_id(1) == 0)
    def _():
        xp_ref[:, pl.ds(0, SUBLANES), :] = jnp.zeros((3, SUBLANES, width), f32)
        s_ref[...] = jnp.zeros_like(s_ref)

    conv_rows = min(tt, 64)
    for idx, ref in enumerate((q_ref, k_ref, v_ref)):
        xp_ref[idx, pl.ds(SUBLANES, tt), :] = ref[0]
        for h in heads:
            cols = slice(h * d, (h + 1) * d)
            w = cw_ref[idx, :, cols]
            for r0 in range(0, tt, conv_rows):
                y = xp_ref[idx, pl.ds(SUBLANES + r0, conv_rows), cols] * w[GDN_CONV - 1:GDN_CONV, :]
                for j in range(GDN_CONV - 1):
                    y = y + xp_ref[idx, pl.ds(SUBLANES - (GDN_CONV - 1) + j + r0, conv_rows), cols] * w[j:j + 1, :]
                y = y * jax.nn.sigmoid(y)
                if idx < 2:
                    y = y * lax.rsqrt(jnp.sum(y * y, axis=-1, keepdims=True) + 1e-6)
                if idx == 0:
                    y = y * (d ** -0.5)
                qkv_ref[idx, pl.ds(r0, conv_rows), cols] = y
        xp_ref[idx, pl.ds(0, SUBLANES), :] = xp_ref[idx, pl.ds(tt, SUBLANES), :]

    row = _iota2((c, c), 0)
    col = _iota2((c, c), 1)
    tril = row >= col
    strict = row > col
    eye = row == col
    cum_l = tril.astype(f32)
    last_row = _iota2((c, 1), 0) == c - 1

    group = GDN_CHUNK_GROUP
    nh = GDN_HEADS

    def chunk_group(gi, carry):
        rows = [pl.ds(pl.multiple_of((gi * group + j) * c, c), c) for j in range(group)]
        sig, gcum = [], []
        for r in rows:
            ba = ba_ref[0, r, :]
            sig.append(jax.nn.sigmoid(ba))
            gcum.append(_dot01(cum_l, -jnp.exp(arow_ref[...]) * jax.nn.softplus(ba + dtrow_ref[...])))
        units = lambda f: [f(j, h) for j in range(group) for h in heads]
        qn = units(lambda j, h: qkv_ref[0, rows[j], h * d:(h + 1) * d])
        kn = units(lambda j, h: qkv_ref[1, rows[j], h * d:(h + 1) * d])
        vv = units(lambda j, h: qkv_ref[2, rows[j], h * d:(h + 1) * d])
        beta = units(lambda j, h: sig[j][:, h:h + 1])
        gc = units(lambda j, h: gcum[j][:, nh + h:nh + h + 1])
        gc_row = [jnp.sum(jnp.where(eye, jnp.broadcast_to(g, (c, c)), 0.0), axis=0, keepdims=True) for g in gc]
        gc_last = [jnp.sum(jnp.where(last_row, g, 0.0), axis=0, keepdims=True) for g in gc]
        decay = [jnp.exp(jnp.where(tril, g - gr, -jnp.inf)) for g, gr in zip(gc, gc_row)]
        knb = [k.astype(bf16) for k in kn]
        kk = [_dot_nt(k, k) for k in knb]
        qk = [_dot_nt(q.astype(bf16), k) for q, k in zip(qn, knb)]
        t_inv = _neumann_inverses([-jnp.where(strict, b * x * dc, 0.0) for b, x, dc in zip(beta, kk, decay)], c)
        egc = [jnp.exp(g) for g in gc]
        u = [_bdot(t, v * b) for t, v, b in zip(t_inv, vv, beta)]
        w = [_bdot(t, k * (b * e)).astype(bf16) for t, k, b, e in zip(t_inv, kn, beta, egc)]
        intra = [jnp.where(tril, x * dc, 0.0).astype(bf16) for x, dc in zip(qk, decay)]
        q_g = [(q * e).astype(bf16) for q, e in zip(qn, egc)]
        k_g = [(k * jnp.exp(gl - g)).astype(bf16) for k, gl, g in zip(kn, gc_last, gc)]
        state = [s_ref[h] for h in heads]
        for j in range(group):
            sl = slice(j * nh, (j + 1) * nh)
            sb = [x.astype(bf16) for x in state]
            v_new = [(x - _dot(y, z)).astype(bf16) for x, y, z in zip(u[sl], w[sl], sb)]
            o = [_dot(q, z) + _dot(a, vn) for q, z, a, vn in zip(q_g[sl], sb, intra[sl], v_new)]
            state = [s * jnp.exp(gl) + _dot_tn(k, vn) for s, gl, k, vn in zip(state, gc_last[sl], k_g[sl], v_new)]
            for h in heads:
                z = z_ref[0, rows[j], h * d:(h + 1) * d]
                o_ref[0, rows[j], h * d:(h + 1) * d] = _rms(o[h], nw_ref[...]) * (z * jax.nn.sigmoid(z))
        for h in heads:
            s_ref[h] = state[h]
        return carry

    lax.fori_loop(0, tt // (c * group), chunk_group, 0)


def gdn_mixer(f3, conv_w, arow, dtrow, norm_w, tt=512):
    b, t, _ = f3.shape
    tt = min(tt, t)
    assert t % tt == 0 and tt % (GDN_CHUNK * GDN_CHUNK_GROUP) == 0
    width = GDN_HEADS * GDN_D
    col = lambda j: pl.BlockSpec((1, tt, width), lambda bi, ti, j=j: (bi, ti, j))
    return pl.pallas_call(
        functools.partial(_gdn_body, tt=tt),
        grid=(b, t // tt),
        in_specs=[col(0), col(1), col(2), col(3),
                  pl.BlockSpec((1, tt, LANES), lambda bi, ti: (bi, ti, AB_BA0 // LANES)),
                  pl.BlockSpec((3, GDN_CONV, width), lambda bi, ti: (0, 0, 0)),
                  pl.BlockSpec((1, LANES), lambda bi, ti: (0, 0)),
                  pl.BlockSpec((1, LANES), lambda bi, ti: (0, 0)),
                  pl.BlockSpec((1, GDN_D), lambda bi, ti: (0, 0))],
        out_specs=pl.BlockSpec((1, tt, width), lambda bi, ti: (bi, ti, 0)),
        out_shape=jax.ShapeDtypeStruct((b, t, width), f32),
        scratch_shapes=[pltpu.VMEM((3, tt + SUBLANES, width), f32),
                        pltpu.VMEM((3, tt, width), f32),
                        pltpu.VMEM((GDN_HEADS, GDN_D, GDN_D), f32)],
        compiler_params=_cparams(("parallel", "arbitrary")),
        name="gdn",
    )(f3, f3, f3, f3, f3, conv_w, arow, dtrow, norm_w)


def _rwkv_body(r_ref, k_ref, v_ref, l_ref, mu_ref, w0_ref, w2_ref, a0_ref, a2_ref, g2_ref, kkw_ref, ka_ref,
               rk_ref, lnw_ref, lnb_ref, o_ref, xp_ref, r_s, lw_s, k2_s, v_s, kk_s, a_s, g_s, s_ref, *, tt):
    c = RWKV_CHUNK
    n = RWKV_N
    w = RWKV_W

    @pl.when(pl.program_id(1) == 0)
    def _():
        xp_ref[pl.ds(0, SUBLANES), :] = jnp.zeros((SUBLANES, xp_ref.shape[1]), f32)
        s_ref[...] = jnp.zeros_like(s_ref)

    xp_ref[pl.ds(SUBLANES, tt), 0:w] = r_ref[0]
    xp_ref[pl.ds(SUBLANES, tt), w:2 * w] = k_ref[0]
    xp_ref[pl.ds(SUBLANES, tt), 2 * w:3 * w] = v_ref[0]
    xp_ref[pl.ds(SUBLANES, tt), 3 * w:] = l_ref[0]
    x = xp_ref[pl.ds(SUBLANES, tt), :]
    x = x + (xp_ref[pl.ds(SUBLANES - 1, tt), :] - x) * mu_ref[...]
    xp_ref[pl.ds(0, SUBLANES), :] = xp_ref[pl.ds(tt, SUBLANES), :]

    k = x[:, w:2 * w]
    lora0 = 3 * w
    wd = x[:, lora0:lora0 + RWKV_DECAY_LORA]
    ad = x[:, lora0 + RWKV_DECAY_LORA:lora0 + RWKV_DECAY_LORA + RWKV_LR_LORA]
    gd = x[:, lora0 + RWKV_DECAY_LORA + RWKV_LR_LORA:]
    w_log = -jax.nn.softplus(-(w0_ref[...] + _dot(jnp.tanh(wd), w2_ref[...], HI))) - 0.5
    lr = jax.nn.sigmoid(a0_ref[...] + _dot(ad, a2_ref[...], HI))
    r_s[...] = x[:, 0:w]
    lw_s[...] = -jnp.exp(w_log)
    k2_s[...] = k * (1.0 + (lr - 1.0) * ka_ref[...])
    v_s[...] = x[:, 2 * w:3 * w]
    kk_s[...] = k * kkw_ref[...]
    a_s[...] = lr
    g_s[...] = _dot(jax.nn.sigmoid(gd), g2_ref[...], HI)

    row = _iota2((c, c), 0)
    col = _iota2((c, c), 1)
    tril = row >= col
    strict = row > col
    cum_l = tril.astype(f32)
    last = _iota2((c, 1), 0) == c - 1

    heads = range(RWKV_HEADS)
    per_head = lambda x: [x[:, h * n:(h + 1) * n] for h in heads]

    group = RWKV_CHUNK_GROUP

    def chunk_group(gi, carry):
        rows = [pl.ds(pl.multiple_of((gi * group + j) * c, c), c) for j in range(group)]
        units = lambda f: [x for r in rows for x in f(r)]
        stack = lambda xs, ys: [jnp.concatenate([x, y], axis=0) for x, y in zip(xs, ys)]
        lw_all = [lw_s[r, :] for r in rows]
        p_all = [_dot01(cum_l, x) for x in lw_all]
        em_all = [jnp.exp(-p) for p in p_all]
        rh = units(lambda r: per_head(r_s[r, :]))
        k2 = units(lambda r: per_head(k2_s[r, :]))
        vh = units(lambda r: per_head(v_s[r, :]))
        r_t = [x for r, p in zip(rows, p_all) for x in per_head(r_s[r, :] * jnp.exp(p))]
        k_t = [x for r, e in zip(rows, em_all) for x in per_head(k2_s[r, :] * e)]
        e_prev = [x for p, lw in zip(p_all, lw_all) for x in per_head(jnp.exp(p - lw))]
        lr_em = [x for r, e in zip(rows, em_all) for x in per_head(a_s[r, :] * e)]
        kk = [x * lax.rsqrt(jnp.sum(x * x, axis=-1, keepdims=True) + 1e-6)
              for x in units(lambda r: per_head(kk_s[r, :]))]
        a_t = [-x * e for x, e in zip(kk, e_prev)]
        b_t = [x * e for x, e in zip(kk, lr_em)]
        ar = [x.astype(bf16) for x in stack(a_t, r_t)]
        bk = [x.astype(bf16) for x in stack(b_t, k_t)]
        ar_b = [_dot_nt(x, y[:c]) for x, y in zip(ar, bk)]
        ar_k = [_dot_nt(x, y[c:]) for x, y in zip(ar, bk)]
        m_ab = [jnp.where(strict, x[:c], 0.0) for x in ar_b]
        a_rb = [jnp.where(tril, x[c:], 0.0) for x in ar_b]
        mk_rk = [jnp.where(jnp.concatenate([strict, tril], axis=0), x, 0.0) for x in ar_k]
        eye = (row == col).astype(f32)
        t_inv = [eye + x for x in m_ab]
        pw = [_bdot(x, x) for x in m_ab]
        for _ in range(int(math.log2(c)) - 1):
            z = [_bdot(x, p) for x, p in zip(stack(pw, t_inv), pw)]
            pw = [x[:c] for x in z]
            t_inv = [t + x[c:] for t, x in zip(t_inv, z)]
        mv_yv = [_bdot(x, v) for x, v in zip(mk_rk, vh)]
        w1 = [_bdot(t, a) for t, a in zip(t_inv, a_t)]
        u2 = [_bdot(t, x[:c]) for t, x in zip(t_inv, mv_yv)]
        y_v = [x[c:] for x in mv_yv]
        w1r = stack(w1, r_t)
        decay_last = [x for p in p_all
                      for x in per_head(jnp.exp(jnp.sum(jnp.where(last, p, 0.0), axis=0, keepdims=True)))]
        g_h = units(lambda r: per_head(g_s[r, :]))
        bonus = [jnp.sum(r * k * rk_ref[:, (i % RWKV_HEADS) * n:(i % RWKV_HEADS + 1) * n], axis=-1, keepdims=True) * v
                 for i, (r, k, v) in enumerate(zip(rh, k2, vh))]
        state = [s_ref[h] for h in heads]
        for j in range(group):
            sl_u = slice(j * RWKV_HEADS, (j + 1) * RWKV_HEADS)
            ws = [_bdot_nt(x, s) for x, s in zip(w1r[sl_u], state)]
            u = [x[:c] + y for x, y in zip(ws, u2[sl_u])]
            y = [x[c:] + _bdot(a, z) + yv for x, a, z, yv in zip(ws, a_rb[sl_u], u, y_v[sl_u])]
            ds = [_dot_tn(x.astype(bf16), y) for x, y in zip(stack(u, vh[sl_u]), bk[sl_u])]
            state = [(s + d) * dl for s, d, dl in zip(state, ds, decay_last[sl_u])]
            for h in heads:
                sl = slice(h * n, (h + 1) * n)
                mean = jnp.mean(y[h], axis=-1, keepdims=True)
                yc = y[h] - mean
                var = jnp.mean(yc * yc, axis=-1, keepdims=True)
                yn = yc * lax.rsqrt(var + RWKV_LN_EPS) * lnw_ref[:, sl] + lnb_ref[:, sl]
                o_ref[0, rows[j], sl] = (yn + bonus[j * RWKV_HEADS + h]) * g_h[j * RWKV_HEADS + h]
        for h in heads:
            s_ref[h] = state[h]
        return carry

    lax.fori_loop(0, tt // (c * group), chunk_group, 0)


def rwkv_mixer(f3, mu, w0, w2, a0, a2, g2, k_k, k_a, r_k, ln_w, ln_b, tt=256):
    b, t, _ = f3.shape
    tt = min(tt, t)
    assert t % tt == 0 and tt % (RWKV_CHUNK * RWKV_CHUNK_GROUP) == 0
    w = RWKV_W
    lora = RWKV_DECAY_LORA + RWKV_LR_LORA + RWKV_GATE_LORA
    wide = 3 * w + lora
    row = lambda n: pl.BlockSpec((1, n), lambda bi, ti: (0, 0))
    full = lambda a: pl.BlockSpec(a.shape, lambda bi, ti: (0, 0))
    col = lambda j: pl.BlockSpec((1, tt, w), lambda bi, ti, j=j: (bi, ti, AB_RWKV0 // w + j))
    return pl.pallas_call(
        functools.partial(_rwkv_body, tt=tt),
        grid=(b, t // tt),
        in_specs=[col(0), col(1), col(2),
                  pl.BlockSpec((1, tt, lora), lambda bi, ti: (bi, ti, AB_LORA0 // lora)),
                  row(wide), row(w), full(w2), row(w), full(a2), full(g2)] + [row(w)] * 5,
        out_specs=pl.BlockSpec((1, tt, w), lambda bi, ti: (bi, ti, 0)),
        out_shape=jax.ShapeDtypeStruct((b, t, w), f32),
        scratch_shapes=[pltpu.VMEM((tt + SUBLANES, wide), f32)] + [pltpu.VMEM((tt, w), f32)] * 7
                       + [pltpu.VMEM((RWKV_HEADS, RWKV_N, RWKV_N), f32)],
        compiler_params=_cparams(("parallel", "arbitrary")),
        name="rwkv",
    )(f3, f3, f3, f3, mu.reshape(1, wide), w0.reshape(1, w), w2, a0.reshape(1, w), a2, g2,
      k_k.reshape(1, w), k_a.reshape(1, w), r_k.reshape(1, w), ln_w.reshape(1, w), ln_b.reshape(1, w))


def _rope128(x, cos, sin, sign_lo, sign_hi):
    r_hi = pltpu.roll(x, ROPE_HALF, 1)
    r_lo = pltpu.roll(x, LANES - ROPE_HALF, 1)
    return x * cos + (r_lo * sign_lo + r_hi * sign_hi) * sin


def _rope_tables(pos, freq_row):
    ang = pos * freq_row
    m = _iota2((1, LANES), 1) % NSA_HD
    sign_lo = jnp.where(m < ROPE_HALF, -1.0, 0.0).astype(f32)
    sign_hi = jnp.where((m >= ROPE_HALF) & (m < ROPE_DIM), 1.0, 0.0).astype(f32)
    return jnp.cos(ang), jnp.sin(ang), sign_lo, sign_hi


def _nsa_prep_body(q_ref, kc_i, vc_i, ks_i, vs_i, kw_i, vw_i, pos_ref, freq_ref,
                   qo_ref, kc_ref, vc_ref, ks_ref, vs_ref, kw_ref, vw_ref):
    cos, sin, s_lo, s_hi = _rope_tables(pos_ref[0], freq_ref[...])
    scale = NSA_HD ** -0.5 * math.log2(math.e)
    n_blk = q_ref.shape[1] // NSA_QBLOCK
    for c in range(NSA_HEADS // 2):
        xt = (_rope128(q_ref[0, :, c * LANES:(c + 1) * LANES], cos, sin, s_lo, s_hi) * scale).T
        for r in range(2):
            head = 2 * c + r
            g, hl = head // NSA_HPG, head % NSA_HPG
            for i in range(n_blk):
                qo_ref[0, g, i, :, hl * NSA_QBLOCK:(hl + 1) * NSA_QBLOCK] = (
                    xt[r * NSA_HD:(r + 1) * NSA_HD, i * NSA_QBLOCK:(i + 1) * NSA_QBLOCK].astype(bf16))

    def split(src, ref, rope, dtype):
        x = src[0]
        if rope:
            x = _rope128(x, cos, sin, s_lo, s_hi)
        ref[0, 0] = x[:, :NSA_HD].astype(dtype)
        ref[0, 1] = x[:, NSA_HD:].astype(dtype)

    def split_t(src, ref):
        xt = src[0].T
        extra = (_iota2((NSA_VT_ROWS - NSA_HD, xt.shape[1]), 0) == 0).astype(f32)
        for g in range(NSA_GROUPS):
            ref[0, g] = jnp.concatenate([xt[g * NSA_HD:(g + 1) * NSA_HD], extra], axis=0).astype(bf16)

    split(kc_i, kc_ref, False, f32)
    split(vc_i, vc_ref, False, f32)
    tt = ks_i.shape[1]
    tok = pl.program_id(1) * tt + _iota2((tt, NSA_HD), 0)
    block_onehot = (tok // SEL_BLOCK == _iota2((tt, NSA_HD), 1)).astype(f32)
    ks = _rope128(ks_i[0], cos, sin, s_lo, s_hi)
    for g in range(NSA_GROUPS):
        ks_ref[0, g] = jnp.concatenate([ks[:, g * NSA_HD:(g + 1) * NSA_HD], block_onehot], axis=1).astype(bf16)
    split_t(vs_i, vs_ref)
    split(kw_i, kw_ref, True, bf16)
    split_t(vw_i, vw_ref)


def nsa_prep(f3, pos3, freq_row, tt=512):
    b, t, _ = f3.shape
    assert t // SEL_BLOCK <= NSA_HD, "the selection-block one-hot shares the key tile's second 64 lanes"
    tt = min(tt, t)
    g = NSA_GROUPS
    n_blk = tt // NSA_QBLOCK
    kv_in = [pl.BlockSpec((1, tt, LANES), lambda bi, ti, c=NSA_KV0 // LANES + i: (bi, ti, c)) for i in range(6)]
    kv_spec = pl.BlockSpec((1, g, tt, NSA_HD), lambda bi, ti: (bi, 0, ti, 0))
    kvt_spec = pl.BlockSpec((1, g, NSA_VT_ROWS, tt), lambda bi, ti: (bi, 0, 0, ti))
    kv32 = jax.ShapeDtypeStruct((b, g, t, NSA_HD), f32)
    kv16 = jax.ShapeDtypeStruct((b, g, t, NSA_HD), bf16)
    kvt16 = jax.ShapeDtypeStruct((b, g, NSA_VT_ROWS, t), bf16)
    q_lanes = NSA_HPG * NSA_QBLOCK
    return pl.pallas_call(
        _nsa_prep_body,
        grid=(b, t // tt),
        in_specs=[pl.BlockSpec((1, tt, NSA_HEADS * NSA_HD), lambda bi, ti: (bi, ti, 0))] + kv_in
                 + [pl.BlockSpec((1, tt, 1), lambda bi, ti: (bi, ti, 0)),
                    pl.BlockSpec((1, LANES), lambda bi, ti: (0, 0))],
        out_specs=[pl.BlockSpec((1, g, n_blk, NSA_HD, q_lanes), lambda bi, ti: (bi, 0, ti, 0, 0)),
                   kv_spec, kv_spec, pl.BlockSpec((1, g, tt, 2 * NSA_HD), lambda bi, ti: (bi, 0, ti, 0)),
                   kvt_spec, kv_spec, kvt_spec],
        out_shape=[jax.ShapeDtypeStruct((b, g, t // NSA_QBLOCK, NSA_HD, q_lanes), bf16),
                   kv32, kv32, jax.ShapeDtypeStruct((b, g, t, 2 * NSA_HD), bf16), kvt16, kv16, kvt16],
        compiler_params=_cparams(("parallel", "parallel")),
        name="nsa_prep",
    )(f3, f3, f3, f3, f3, f3, f3, pos3, freq_row)


def _nsa_compress_body(kc_ref, vc_ref, pek_ref, w1k_ref, w2k_ref, pev_ref, w1v_ref, w2v_ref, pos_ref, freq_ref,
                       ko_ref, vo_ref):
    half = CMP_STRIDE * NSA_HD
    nrow = kc_ref.shape[2]
    last_row = _iota2((nrow, 1), 0) == nrow - 1

    def hidden(x, pe_ref, w1_ref):
        lo = _dot((x + pe_ref[:, :half]).astype(bf16), w1_ref[:half, :])
        hi = _dot((x + pe_ref[:, half:]).astype(bf16), w1_ref[half:, :])
        hi = jnp.where(last_row, 0.0, pltpu.roll(hi, nrow - 1, 0))
        hid = lo + hi
        return (hid * jax.nn.sigmoid(hid)).astype(bf16)

    ks = [_dot(hidden(kc_ref[0, g], pek_ref, w1k_ref), w2k_ref[...]) for g in range(NSA_GROUPS)]
    cos, sin, s_lo, s_hi = _rope_tables(pos_ref[0], freq_ref[...])
    kr = _rope128(jnp.concatenate(ks, axis=-1), cos, sin, s_lo, s_hi)
    for g in range(NSA_GROUPS):
        ko_ref[0, g] = kr[:, g * NSA_HD:(g + 1) * NSA_HD].astype(bf16)
        vo_ref[0, g] = _dot_nt(w2v_ref[...], hidden(vc_ref[0, g], pev_ref, w1v_ref)).astype(bf16)


def nsa_compress(kc4, vc4, pe_k, w1_k, w2_k, pe_v, w1_v, w2_vt, cpos3, freq_row):
    b, g, nrow, wide = kc4.shape
    full = lambda a: pl.BlockSpec(a.shape, lambda bi: (0,) * a.ndim)
    blk = pl.BlockSpec((1, g, nrow, wide), lambda bi: (bi, 0, 0, 0))
    out = pl.BlockSpec((1, g, nrow, NSA_HD), lambda bi: (bi, 0, 0, 0))
    out_t = pl.BlockSpec((1, g, NSA_HD, nrow), lambda bi: (bi, 0, 0, 0))
    return pl.pallas_call(
        _nsa_compress_body,
        grid=(b,),
        in_specs=[blk, blk, full(pe_k), full(w1_k), full(w2_k), full(pe_v), full(w1_v), full(w2_vt),
                  pl.BlockSpec((1, nrow, 1), lambda bi: (bi, 0, 0)), pl.BlockSpec((1, LANES), lambda bi: (0, 0))],
        out_specs=[out, out_t],
        out_shape=[jax.ShapeDtypeStruct((b, g, nrow, NSA_HD), bf16),
                   jax.ShapeDtypeStruct((b, g, NSA_HD, nrow), bf16)],
        compiler_params=_cparams(("parallel",)),
        name="nsa_compress",
    )(kc4, vc4, pe_k, w1_k, w2_k, pe_v, w1_v, w2_vt, cpos3, freq_row)


def _nsa_attn_body(qt_ref, kc_ref, vct_ref, ks_ref, vst_ref, kw_ref, vwt_ref, gate_ref, ovt_ref, dbias_ref,
                   wbias_ref, rep_ref, o_ref, score_ref, s_ref, acc_ref, sw_ref, *, key_tile):
    qb_n = NSA_QBLOCK
    blk_lanes = NSA_HPG * qb_n
    lanes = NSA_QPAIR * blk_lanes
    pair = pl.program_id(2)
    qt = jnp.concatenate([qt_ref[0, 0, a] for a in range(NSA_QPAIR)], axis=1)
    lane = _iota2((1, lanes), 1)
    t_lane = (pair * NSA_QPAIR + lane // blk_lanes) * qb_n + lane % qb_n

    def softmax_t(s):
        mx = jnp.max(s, axis=0, keepdims=True)
        mx = jnp.where(mx > -jnp.inf, mx, 0.0)
        e = jnp.exp2(s - mx)
        return e, jnp.sum(e, axis=0, keepdims=True), mx

    diag0 = pair * NSA_QPAIR
    n_tiles = (diag0 * SEL_BLOCK + key_tile - 1) // key_tile
    last_tile = ks_ref.shape[2] // key_tile - 1

    gt = jax.nn.sigmoid(gate_ref[0]).T
    gate = lambda br: jnp.concatenate([gt[3 * h + br:3 * h + br + 1, a * qb_n:(a + 1) * qb_n]
                                       for a in range(NSA_QPAIR) for h in range(NSA_HPG)], axis=1)
    gates = [gate(br) for br in range(3)]

    s_cmp = _dot(kc_ref[0, 0], qt)
    span = WINDOW + NSA_QPAIR * qb_n
    w0 = pl.multiple_of(pair * NSA_QPAIR * qb_n, LANES)
    sw_ref[...] = _dot(kw_ref[0, 0, pl.ds(w0, span), :], qt) + wbias_ref[...]
    k0 = pl.multiple_of(diag0 * SEL_BLOCK, NSA_QPAIR * SEL_BLOCK)
    s_diag = _dot(ks_ref[0, 0, pl.ds(k0, NSA_QPAIR * SEL_BLOCK), 0:NSA_HD], qt) + dbias_ref[...]

    n_cmp = kc_ref.shape[2]
    cmp_end = _iota2((n_cmp, 1), 0) * CMP_STRIDE + (CMP_LEN - 1)
    e_c, den_c, _ = softmax_t(jnp.where(cmp_end <= t_lane, s_cmp, -jnp.inf))
    p_c = e_c * (1.0 / jnp.maximum(den_c, 1e-30))
    o_c = _dot(vct_ref[0, 0], p_c.astype(bf16))

    n_sel = ovt_ref.shape[0]
    p_pair = []
    for a in range(NSA_QPAIR):
        acc = p_c[:, a * blk_lanes:a * blk_lanes + LANES]
        for c in range(1, blk_lanes // LANES):
            acc = acc + p_c[:, a * blk_lanes + c * LANES:a * blk_lanes + (c + 1) * LANES]
        p_pair.append(acc[:, :qb_n] + acc[:, qb_n:])
    imp = _dot(ovt_ref[...], jnp.concatenate(p_pair, axis=1), HI)
    blk = _iota2((n_sel, NSA_QPAIR * qb_n), 0)
    cur = pair * NSA_QPAIR + _iota2((n_sel, NSA_QPAIR * qb_n), 1) // qb_n
    valid = blk <= cur
    forced = (blk == 0) | (blk == cur) | (blk == cur - 1)
    score = jnp.where(valid, jnp.where(forced, jnp.inf, imp), -jnp.inf)
    score_ref[...] = score

    e_d, _, m_s = softmax_t(s_diag)
    acc_ref[...] = _dot(vst_ref[0, 0, :, pl.ds(k0, NSA_QPAIR * SEL_BLOCK)], e_d.astype(bf16))

    @pl.when(w0 < WINDOW)
    def _():
        is_pad = _iota2((WINDOW, 1), 0) < WINDOW - w0
        sw_ref[pl.ds(0, WINDOW), :] = jnp.where(is_pad, -jnp.inf, sw_ref[pl.ds(0, WINDOW), :])

    def rank_step(jp, cnt):
        other = score_ref[pl.ds(jp, 1), :]
        ahead = (other > score) | ((other == score) & (blk > jp))
        return cnt + ahead.astype(jnp.int32)

    def rank_step2(jj, cnt):
        return rank_step(2 * jj + 1, rank_step(2 * jj, cnt))

    n_cand = pair * NSA_QPAIR + NSA_QPAIR
    cnt = lax.fori_loop(0, jnp.where(n_cand > SEL_TOPN, n_cand // 2, 0), rank_step2,
                        jnp.zeros((n_sel, NSA_QPAIR * qb_n), jnp.int32))
    bias = jnp.where((cnt < SEL_TOPN) & (blk < diag0), 0.0, NSA_MASKED).astype(bf16)
    bias = _dot(bias, rep_ref[...]).astype(bf16)
    if n_sel < NSA_HD:
        bias = jnp.concatenate([bias, jnp.zeros((NSA_HD - n_sel, lanes), bf16)], axis=0)
    qx = jnp.concatenate([qt, bias], axis=0)

    def score_tile(kt):
        kk0 = pl.multiple_of(kt * key_tile, key_tile)
        return _dot(ks_ref[0, 0, pl.ds(kk0, key_tile), :], qx)

    s_ref[0] = score_tile(0)

    e_w, _, _ = softmax_t(sw_ref[...])
    o_w = _dot(vwt_ref[0, 0, :, pl.ds(w0, span)], e_w.astype(bf16))

    def half_step(kt, slot, carry):
        m_old = carry
        s_ref[1 - slot] = score_tile(jnp.minimum(kt + 1, last_tile))
        sm = s_ref[slot]
        mx = jnp.maximum(m_old, jnp.max(sm, axis=0, keepdims=True))
        alpha = jnp.exp2(m_old - mx)
        p = jnp.exp2(sm - mx)
        kk0 = pl.multiple_of(kt * key_tile, key_tile)
        acc_ref[...] = alpha * acc_ref[...] + _dot(vst_ref[0, 0, :, pl.ds(kk0, key_tile)], p.astype(bf16))
        return mx

    def pair_step(pi, carry):
        return half_step(2 * pi + 1, 1, half_step(2 * pi, 0, carry))

    lax.fori_loop(0, (n_tiles + 1) // 2, pair_step, m_s)

    l_s = acc_ref[pl.ds(NSA_HD, 1), :]
    l_w = o_w[NSA_HD:NSA_HD + 1]
    o_t = (gates[0] * o_c + (gates[1] * (1.0 / l_s)) * acc_ref[pl.ds(0, NSA_HD), :]
           + (gates[2] * (1.0 / l_w)) * o_w[:NSA_HD])
    o_b = o_t.astype(bf16)
    q_idx = _iota2((qb_n, LANES), 0)
    l_idx = _iota2((qb_n, LANES), 1)
    pick = [(l_idx == q_idx + r * qb_n).astype(bf16) for r in range(LANES // qb_n)]
    for a in range(NSA_QPAIR):
        for c in range(blk_lanes // LANES):
            piece = o_b[:, a * blk_lanes + c * LANES:a * blk_lanes + (c + 1) * LANES]
            for r in range(LANES // qb_n):
                h = c * (LANES // qb_n) + r
                o_ref[0, a * qb_n:(a + 1) * qb_n, h * NSA_HD:(h + 1) * NSA_HD] = (
                    _dot_nt(pick[r], piece).astype(bf16))


def nsa_attention(qt, kc, vct, ks, vst, kw, vwt, f3, overlap_t, key_tile=256):
    b, g, n_q, d, blk_lanes = qt.shape
    t = n_q * NSA_QBLOCK
    key_tile = min(key_tile, t)
    assert (t // key_tile) % 2 == 0, "an odd tile count borrows the (fully unselected) tile after the last one"
    n_cmp = kc.shape[2]
    n_sel = overlap_t.shape[0]
    lanes = NSA_QPAIR * blk_lanes
    q_rows = NSA_QPAIR * NSA_QBLOCK
    off = (jnp.arange(lanes) // blk_lanes) * NSA_QBLOCK + jnp.arange(lanes) % NSA_QBLOCK
    r_d = jnp.arange(NSA_QPAIR * SEL_BLOCK)[:, None]
    diag_bias = jnp.where(r_d <= off[None, :], 0.0, -jnp.inf).astype(f32)
    r_w = jnp.arange(WINDOW + q_rows)[:, None]
    win_bias = jnp.where((r_w > off[None, :]) & (r_w <= WINDOW + off[None, :]), 0.0, -jnp.inf).astype(f32)
    rep = (jnp.arange(q_rows)[:, None] == off[None, :]).astype(bf16)
    seq = pl.BlockSpec((1, 1, t, ks.shape[3]), lambda bi, gi, qi: (bi, gi, 0, 0))
    seq_t = pl.BlockSpec((1, 1, NSA_VT_ROWS, t), lambda bi, gi, qi: (bi, gi, 0, 0))
    const = lambda a: pl.BlockSpec(a.shape, lambda bi, gi, qi: (0, 0))
    return pl.pallas_call(
        functools.partial(_nsa_attn_body, key_tile=key_tile),
        grid=(b, g, n_q // NSA_QPAIR),
        in_specs=[pl.BlockSpec((1, 1, NSA_QPAIR, d, blk_lanes), lambda bi, gi, qi: (bi, gi, qi, 0, 0)),
                  pl.BlockSpec((1, 1, n_cmp, d), lambda bi, gi, qi: (bi, gi, 0, 0)),
                  pl.BlockSpec((1, 1, d, n_cmp), lambda bi, gi, qi: (bi, gi, 0, 0)),
                  seq, seq_t,
                  pl.BlockSpec((1, 1, t + WINDOW, d), lambda bi, gi, qi: (bi, gi, 0, 0)),
                  pl.BlockSpec((1, 1, NSA_VT_ROWS, t + WINDOW), lambda bi, gi, qi: (bi, gi, 0, 0)),
                  pl.BlockSpec((1, q_rows, LANES), lambda bi, gi, qi: (bi, qi, NSA_GATE0 // LANES + gi)),
                  const(overlap_t), const(diag_bias), const(win_bias), const(rep)],
        out_specs=pl.BlockSpec((1, q_rows, NSA_HPG * d), lambda bi, gi, qi: (bi, qi, gi)),
        out_shape=jax.ShapeDtypeStruct((b, t, g * NSA_HPG * d), bf16),
        scratch_shapes=[pltpu.VMEM((n_sel, q_rows), f32),
                        pltpu.VMEM((2, key_tile, lanes), f32), pltpu.VMEM((NSA_VT_ROWS, lanes), f32),
                        pltpu.VMEM((WINDOW + q_rows, lanes), f32)],
        compiler_params=_cparams(("parallel", "parallel", "arbitrary")),
        name="nsa_attn",
    )(qt, kc, vct, ks, vst, kw, vwt, f3, overlap_t, diag_bias, win_bias, rep)


def _place(cols, total, pieces):
    out = jnp.zeros((cols, total), f32)
    for start, mat in pieces:
        out = lax.dynamic_update_slice(out, mat.astype(f32), (0, start))
    return out


def _layer0(h, b, t, g_pre, w_in, w_out, g_post, conv, a_log, dt_bias, gnorm, mu, w0, w2, a0, a2, g2, k_k, k_a,
            r_k, ln_w, ln_b):
    gdn_w = 4 * GDN_HEADS * GDN_D
    w_pad = _place(D_MODEL, AB_COLS, [
        (0, w_in[:, :gdn_w]),
        (AB_BA0, w_in[:, gdn_w:gdn_w + 2 * GDN_HEADS]),
        (AB_RWKV0, w_in[:, gdn_w + 2 * GDN_HEADS:gdn_w + 2 * GDN_HEADS + 3 * RWKV_W]),
        (AB_LORA0, w_in[:, gdn_w + 2 * GDN_HEADS + 3 * RWKV_W:]),
    ]).astype(bf16)
    f3 = norm_matmul(h, g_pre, w_pad).reshape(b, t, AB_COLS)
    arow = jnp.zeros((1, LANES), f32).at[0, GDN_HEADS:2 * GDN_HEADS].set(a_log)
    dtrow = jnp.zeros((1, LANES), f32).at[0, GDN_HEADS:2 * GDN_HEADS].set(dt_bias)
    conv3 = conv.reshape(GDN_CONV, 3, GDN_HEADS * GDN_D).transpose(1, 0, 2)
    o_a = gdn_mixer(f3, conv3, arow, dtrow, gnorm.reshape(1, GDN_D))
    o_b = rwkv_mixer(f3, mu, w0, w2, a0, a2, g2, k_k, k_a, r_k, ln_w, ln_b)
    m = b * t
    n_a = GDN_HEADS * GDN_D
    return out_res_norm([o_a.reshape(m, n_a), o_b.reshape(m, RWKV_W)],
                        [w_out[:n_a].astype(bf16), w_out[n_a:].astype(bf16)], h, g_post)


def _layer1(h, b, t, positions, g_pre, w_in, w_out, g_post, pe_k, w1_k, w2_k, pe_v, w1_v, w2_v):
    qw = NSA_HEADS * NSA_HD
    kvw = 6 * NSA_GROUPS * NSA_HD
    gates = w_in[:, qw + kvw:].reshape(D_MODEL, NSA_GROUPS, NSA_HPG * 3)
    w_pad = _place(D_MODEL, NSA_COLS, [(0, w_in[:, :qw + kvw])]
                   + [(NSA_GATE0 + gi * LANES, gates[:, gi]) for gi in range(NSA_GROUPS)]).astype(bf16)
    f3 = norm_matmul(h, g_pre, w_pad).reshape(b, t, NSA_COLS)
    inv_freq = ROPE_THETA ** (-jnp.arange(ROPE_HALF, dtype=f32) * (2.0 / ROPE_DIM))
    lane = jnp.arange(LANES)
    freq_row = jnp.where(lane % NSA_HD < ROPE_DIM, inv_freq[lane % ROPE_HALF], 0.0).reshape(1, LANES).astype(f32)
    posf = positions.astype(f32)
    qt, kc, vc, ks, vst, kw, vwt = nsa_prep(f3, posf.reshape(b, t, 1), freq_row)
    nrow = t // CMP_STRIDE
    cpos = jnp.concatenate([posf[:, CMP_LEN - 1::CMP_STRIDE], posf[:, -1:]], axis=1).reshape(b, nrow, 1)
    flat = lambda a: a.reshape(b, NSA_GROUPS, nrow, CMP_STRIDE * NSA_HD)
    kcc, vcct = nsa_compress(flat(kc), flat(vc), pe_k.reshape(1, -1), w1_k.astype(bf16), w2_k.astype(bf16),
                             pe_v.reshape(1, -1), w1_v.astype(bf16), w2_v.T.astype(bf16), cpos, freq_row)
    n_sel = t // SEL_BLOCK
    c_start = jnp.arange(nrow) * CMP_STRIDE
    s_start = jnp.arange(n_sel) * SEL_BLOCK
    overlap_t = jnp.clip(jnp.minimum(c_start[None, :] + CMP_LEN, s_start[:, None] + SEL_BLOCK)
                         - jnp.maximum(c_start[None, :], s_start[:, None]), 0, None).astype(f32) / CMP_LEN
    kw_pad = jnp.pad(kw, ((0, 0), (0, 0), (WINDOW, 0), (0, 0)))
    vwt_pad = jnp.pad(vwt, ((0, 0), (0, 0), (0, 0), (WINDOW, 0)))
    o = nsa_attention(qt, kcc, vcct, ks, vst, kw_pad, vwt_pad, f3, overlap_t)
    return out_res_norm([o.reshape(b * t, qw)], [w_out.astype(bf16)], h, g_post)


def kernel(x, positions, norm_mix_pre, norm_mix_post, norm_ffn_pre, norm_ffn_post, w_ffn_up, w_ffn_down, ab_w_in,
           ab_w_out, gdn_conv, gdn_a_log, gdn_dt_bias, gdn_norm, rwkv_mu, rwkv_w0, rwkv_w2, rwkv_a0, rwkv_a2,
           rwkv_g2, rwkv_k_k, rwkv_k_a, rwkv_r_k, rwkv_ln_w, rwkv_ln_b, nsa_w_in, nsa_w_out, nsa_pe_k, nsa_w1_k,
           nsa_w2_k, nsa_pe_v, nsa_w1_v, nsa_w2_v):
    b, t, d = x.shape
    h = x.reshape(b * t, d)
    h = _layer0(h, b, t, norm_mix_pre[0], ab_w_in[0], ab_w_out[0], norm_mix_post[0], gdn_conv[0], gdn_a_log[0],
                gdn_dt_bias[0], gdn_norm[0], rwkv_mu[0], rwkv_w0[0], rwkv_w2[0], rwkv_a0[0], rwkv_a2[0],
                rwkv_g2[0], rwkv_k_k[0], rwkv_k_a[0], rwkv_r_k[0].reshape(-1), rwkv_ln_w[0], rwkv_ln_b[0])
    h = ffn(h, norm_ffn_pre[0], w_ffn_up[0].astype(bf16), w_ffn_down[0].astype(bf16), norm_ffn_post[0])
    h = _layer1(h, b, t, positions, norm_mix_pre[1], nsa_w_in[0], nsa_w_out[0], norm_mix_post[1], nsa_pe_k[0],
                nsa_w1_k[0], nsa_w2_k[0], nsa_pe_v[0], nsa_w1_v[0], nsa_w2_v[0])
    h = ffn(h, norm_ffn_pre[1], w_ffn_up[1].astype(bf16), w_ffn_down[1].astype(bf16), norm_ffn_post[1])
    return h.reshape(b, t, d)
```

```python
import functools
import math

import jax
import jax.numpy as jnp
from jax import lax
from jax.experimental import pallas as pl
from jax.experimental.pallas import tpu as pltpu

f32 = jnp.float32
bf16 = jnp.bfloat16
HI = lax.Precision.HIGHEST

V7X_VMEM_LIMIT_BYTES = 56 * 1024 * 1024
LANES = 128
SUBLANES = 8

D_MODEL = 1024
D_FF = 4 * D_MODEL
NORM_EPS = 1e-6
GDN_HEADS = 4
GDN_D = 128
GDN_CONV = 4
GDN_CHUNK = 128
GDN_CHUNK_GROUP = 4
RWKV_HEADS = 8
RWKV_N = 64
RWKV_W = RWKV_HEADS * RWKV_N
RWKV_CHUNK = 64
RWKV_CHUNK_GROUP = 4
RWKV_LN_EPS = 64e-5
RWKV_DECAY_LORA = 64
RWKV_LR_LORA = 64
RWKV_GATE_LORA = 128
NSA_HEADS = 16
NSA_GROUPS = 2
NSA_HPG = NSA_HEADS // NSA_GROUPS
NSA_HD = 64
CMP_LEN = 32
CMP_STRIDE = 16
CMP_HIDDEN = 256
SEL_BLOCK = 64
SEL_TOPN = 16
WINDOW = 512
NSA_QBLOCK = 64
NSA_QPAIR = 2
NSA_VT_ROWS = 80
NSA_MASKED = -(2.0 ** 126)
ROPE_THETA = 500000.0
ROPE_DIM = NSA_HD // 4
ROPE_HALF = ROPE_DIM // 2

AB_COLS = 4096
AB_RWKV0 = 2048
AB_LORA0 = 3584
AB_BA0 = 3840
NSA_COLS = 2048
NSA_KV0 = 1024
NSA_GATE0 = 1792


def _cparams(sem):
    return pltpu.CompilerParams(dimension_semantics=sem, vmem_limit_bytes=V7X_VMEM_LIMIT_BYTES)


def _rms(x, g):
    return x * lax.rsqrt(jnp.mean(x * x, axis=-1, keepdims=True) + NORM_EPS) * g


def _dot(a, b, precision=None):
    return jnp.dot(a, b, precision=precision, preferred_element_type=f32)


def _dot_nt(a, b, precision=None):
    return lax.dot_general(a, b, (((1,), (1,)), ((), ())), precision=precision, preferred_element_type=f32)


def _dot_tn(a, b, precision=None):
    return lax.dot_general(a, b, (((0,), (0,)), ((), ())), precision=precision, preferred_element_type=f32)


def _bdot(a, b):
    return _dot(a.astype(bf16), b.astype(bf16))


def _bdot_nt(a, b):
    return _dot_nt(a.astype(bf16), b.astype(bf16))


def _bdot_tn(a, b):
    return _dot_tn(a.astype(bf16), b.astype(bf16))


def _dot01(m01, x):
    m = m01.astype(bf16)
    hi = x.astype(bf16)
    rest = x - hi.astype(f32)
    mid = rest.astype(bf16)
    lo = (rest - mid.astype(f32)).astype(bf16)
    return _dot(m, hi) + _dot(m, mid) + _dot(m, lo)


def _iota2(shape, axis):
    return lax.broadcasted_iota(jnp.int32, shape, axis)


def _norm_matmul_body(x_ref, g_ref, w_ref, o_ref, u_ref):
    @pl.when(pl.program_id(1) == 0)
    def _():
        u_ref[...] = _rms(x_ref[...], g_ref[...]).astype(bf16)

    o_ref[...] = _dot(u_ref[...], w_ref[...])


def norm_matmul(x, g, w, tm=1024, tn=1024):
    m, d = x.shape
    n = w.shape[1]
    tm = min(tm, m)
    return pl.pallas_call(
        _norm_matmul_body,
        grid=(m // tm, n // tn),
        in_specs=[pl.BlockSpec((tm, d), lambda i, j: (i, 0)),
                  pl.BlockSpec((1, d), lambda i, j: (0, 0)),
                  pl.BlockSpec((d, tn), lambda i, j: (0, j))],
        out_specs=pl.BlockSpec((tm, tn), lambda i, j: (i, j)),
        out_shape=jax.ShapeDtypeStruct((m, n), f32),
        scratch_shapes=[pltpu.VMEM((tm, d), bf16)],
        compiler_params=_cparams(("parallel", "arbitrary")),
        name="norm_matmul",
    )(x, g.reshape(1, d), w)


def _out_ffn_body(*refs, n_parts):
    a_refs = refs[:n_parts]
    w_refs = refs[n_parts:2 * n_parts]
    h_ref, gm_ref, g1_ref, wup_ref, wdn_ref, g2_ref, o_ref, h1_ref, u_ref, acc_ref = refs[2 * n_parts:]
    k = pl.program_id(1)

    @pl.when(k == 0)
    def _():
        y = _dot(a_refs[0][...].astype(bf16), w_refs[0][...])
        for a_ref, w_ref in zip(a_refs[1:], w_refs[1:]):
            y = y + _dot(a_ref[...].astype(bf16), w_ref[...])
        h1 = h_ref[...] + _rms(y, gm_ref[...])
        h1_ref[...] = h1
        u_ref[...] = _rms(h1, g1_ref[...]).astype(bf16)
        acc_ref[...] = jnp.zeros_like(acc_ref)

    a = _dot(u_ref[...], wup_ref[...])
    a = jnp.square(jnp.maximum(a, 0.0))
    acc_ref[...] += _dot(a.astype(bf16), wdn_ref[...])

    @pl.when(k == pl.num_programs(1) - 1)
    def _():
        o_ref[...] = h1_ref[...] + _rms(acc_ref[...], g2_ref[...])


def out_ffn(parts, w_parts, h, g_mix, g1, wup, wdn, g2, tm=1024, tf=1024):
    m, d = h.shape
    ff = wup.shape[1]
    tm = min(tm, m)
    n_parts = len(parts)
    row = pl.BlockSpec((1, d), lambda i, k: (0, 0))
    in_specs = ([pl.BlockSpec((tm, p.shape[1]), lambda i, k: (i, 0)) for p in parts]
                + [pl.BlockSpec(w.shape, lambda i, k: (0, 0)) for w in w_parts]
                + [pl.BlockSpec((tm, d), lambda i, k: (i, 0)), row, row,
                   pl.BlockSpec((d, tf), lambda i, k: (0, k)),
                   pl.BlockSpec((tf, d), lambda i, k: (k, 0)), row])
    return pl.pallas_call(
        functools.partial(_out_ffn_body, n_parts=n_parts),
        grid=(m // tm, ff // tf),
        in_specs=in_specs,
        out_specs=pl.BlockSpec((tm, d), lambda i, k: (i, 0)),
        out_shape=jax.ShapeDtypeStruct((m, d), f32),
        scratch_shapes=[pltpu.VMEM((tm, d), f32), pltpu.VMEM((tm, d), bf16), pltpu.VMEM((tm, d), f32)],
        compiler_params=_cparams(("parallel", "arbitrary")),
        name="out_ffn",
    )(*parts, *w_parts, h, g_mix.reshape(1, d), g1.reshape(1, d), wup, wdn, g2.reshape(1, d))


def _neumann_inverses(n_mats, size):
    eye = (_iota2((size, size), 0) == _iota2((size, size), 1)).astype(f32)
    ts = [eye + n for n in n_mats]
    ps = list(n_mats)
    for _ in range(int(math.log2(size)) - 1):
        ps = [_bdot(p, p) for p in ps]
        ts = [t + _bdot(t, p) for t, p in zip(ts, ps)]
    return ts


def _gdn_body(q_ref, k_ref, v_ref, z_ref, ba_ref, cw_ref, arow_ref, dtrow_ref, nw_ref, o_ref,
              xp_ref, qkv_ref, s_ref, *, tt):
    c = GDN_CHUNK
    d = GDN_D
    heads = range(GDN_HEADS)
    width = GDN_HEADS * d

    @pl.when(pl.program_id(1) == 0)
    def _():
        xp_ref[:, pl.ds(0, SUBLANES), :] = jnp.zeros((3, SUBLANES, width), f32)
        s_ref[...] = jnp.zeros_like(s_ref)

    conv_rows = min(tt, 64)
    for idx, ref in enumerate((q_ref, k_ref, v_ref)):
        xp_ref[idx, pl.ds(SUBLANES, tt), :] = ref[0]
        for h in heads:
            cols = slice(h * d, (h + 1) * d)
            w = cw_ref[idx, :, cols]
            for r0 in range(0, tt, conv_rows):
                y = xp_ref[idx, pl.ds(SUBLANES + r0, conv_rows), cols] * w[GDN_CONV - 1:GDN_CONV, :]
                for j in range(GDN_CONV - 1):
                    y = y + xp_ref[idx, pl.ds(SUBLANES - (GDN_CONV - 1) + j + r0, conv_rows), cols] * w[j:j + 1, :]
                y = y * jax.nn.sigmoid(y)
                if idx < 2:
                    y = y * lax.rsqrt(jnp.sum(y * y, axis=-1, keepdims=True) + 1e-6)
                if idx == 0:
                    y = y * (d ** -0.5)
                qkv_ref[idx, pl.ds(r0, conv_rows), cols] = y
        xp_ref[idx, pl.ds(0, SUBLANES), :] = xp_ref[idx, pl.ds(tt, SUBLANES), :]

    row = _iota2((c, c), 0)
    col = _iota2((c, c), 1)
    tril = row >= col
    strict = row > col
    eye = row == col
    cum_l = tril.astype(f32)
    last_row = _iota2((c, 1), 0) == c - 1

    group = GDN_CHUNK_GROUP
    nh = GDN_HEADS

    def chunk_group(gi, carry):
        rows = [pl.ds(pl.multiple_of((gi * group + j) * c, c), c) for j in range(group)]
        sig, gcum = [], []
        for r in rows:
            ba = ba_ref[0, r, :]
            sig.append(jax.nn.sigmoid(ba))
            gcum.append(_dot01(cum_l, -jnp.exp(arow_ref[...]) * jax.nn.softplus(ba + dtrow_ref[...])))
        units = lambda f: [f(j, h) for j in range(group) for h in heads]
        qn = units(lambda j, h: qkv_ref[0, rows[j], h * d:(h + 1) * d])
        kn = units(lambda j, h: qkv_ref[1, rows[j], h * d:(h + 1) * d])
        vv = units(lambda j, h: qkv_ref[2, rows[j], h * d:(h + 1) * d])
        beta = units(lambda j, h: sig[j][:, h:h + 1])
        gc = units(lambda j, h: gcum[j][:, nh + h:nh + h + 1])
        gc_row = [jnp.sum(jnp.where(eye, jnp.broadcast_to(g, (c, c)), 0.0), axis=0, keepdims=True) for g in gc]
        gc_last = [jnp.sum(jnp.where(last_row, g, 0.0), axis=0, keepdims=True) for g in gc]
        decay = [jnp.exp(jnp.where(tril, g - gr, -jnp.inf)) for g, gr in zip(gc, gc_row)]
        knb = [k.astype(bf16) for k in kn]
        kk = [_dot_nt(k, k) for k in knb]
        qk = [_dot_nt(q.astype(bf16), k) for q, k in zip(qn, knb)]
        t_inv = _neumann_inverses([-jnp.where(strict, b * x * dc, 0.0) for b, x, dc in zip(beta, kk, decay)], c)
        egc = [jnp.exp(g) for g in gc]
        u = [_bdot(t, v * b) for t, v, b in zip(t_inv, vv, beta)]
        w = [_bdot(t, k * (b * e)).astype(bf16) for t, k, b, e in zip(t_inv, kn, beta, egc)]
        intra = [jnp.where(tril, x * dc, 0.0).astype(bf16) for x, dc in zip(qk, decay)]
        q_g = [(q * e).astype(bf16) for q, e in zip(qn, egc)]
        k_g = [(k * jnp.exp(gl - g)).astype(bf16) for k, gl, g in zip(kn, gc_last, gc)]
        state = [s_ref[h] for h in heads]
        for j in range(group):
            sl = slice(j * nh, (j + 1) * nh)
            sb = [x.astype(bf16) for x in state]
            v_new = [(x - _dot(y, z)).astype(bf16) for x, y, z in zip(u[sl], w[sl], sb)]
            o = [_dot(q, z) + _dot(a, vn) for q, z, a, vn in zip(q_g[sl], sb, intra[sl], v_new)]
            state = [s * jnp.exp(gl) + _dot_tn(k, vn) for s, gl, k, vn in zip(state, gc_last[sl], k_g[sl], v_new)]
            for h in heads:
                z = z_ref[0, rows[j], h * d:(h + 1) * d]
                o_ref[0, rows[j], h * d:(h + 1) * d] = _rms(o[h], nw_ref[...]) * (z * jax.nn.sigmoid(z))
        for h in heads:
            s_ref[h] = state[h]
        return carry

    lax.fori_loop(0, tt // (c * group), chunk_group, 0)


def gdn_mixer(f3, conv_w, arow, dtrow, norm_w, tt=512):
    b, t, _ = f3.shape
    tt = min(tt, t)
    assert t % tt == 0 and tt % (GDN_CHUNK * GDN_CHUNK_GROUP) == 0
    width = GDN_HEADS * GDN_D
    col = lambda j: pl.BlockSpec((1, tt, width), lambda bi, ti, j=j: (bi, ti, j))
    return pl.pallas_call(
        functools.partial(_gdn_body, tt=tt),
        grid=(b, t // tt),
        in_specs=[col(0), col(1), col(2), col(3),
                  pl.BlockSpec((1, tt, LANES), lambda bi, ti: (bi, ti, AB_BA0 // LANES)),
                  pl.BlockSpec((3, GDN_CONV, width), lambda bi, ti: (0, 0, 0)),
                  pl.BlockSpec((1, LANES), lambda bi, ti: (0, 0)),
                  pl.BlockSpec((1, LANES), lambda bi, ti: (0, 0)),
                  pl.BlockSpec((1, GDN_D), lambda bi, ti: (0, 0))],
        out_specs=pl.BlockSpec((1, tt, width), lambda bi, ti: (bi, ti, 0)),
        out_shape=jax.ShapeDtypeStruct((b, t, width), f32),
        scratch_shapes=[pltpu.VMEM((3, tt + SUBLANES, width), f32),
                        pltpu.VMEM((3, tt, width), f32),
                        pltpu.VMEM((GDN_HEADS, GDN_D, GDN_D), f32)],
        compiler_params=_cparams(("parallel", "arbitrary")),
        name="gdn",
    )(f3, f3, f3, f3, f3, conv_w, arow, dtrow, norm_w)


def _rwkv_body(r_ref, k_ref, v_ref, l_ref, mu_ref, w0_ref, w2_ref, a0_ref, a2_ref, g2_ref, kkw_ref, ka_ref,
               rk_ref, lnw_ref, lnb_ref, o_ref, xp_ref, r_s, lw_s, k2_s, v_s, kk_s, a_s, g_s, s_ref, *, tt):
    c = RWKV_CHUNK
    n = RWKV_N
    w = RWKV_W

    @pl.when(pl.program_id(1) == 0)
    def _():
        xp_ref[pl.ds(0, SUBLANES), :] = jnp.zeros((SUBLANES, xp_ref.shape[1]), f32)
        s_ref[...] = jnp.zeros_like(s_ref)

    xp_ref[pl.ds(SUBLANES, tt), 0:w] = r_ref[0]
    xp_ref[pl.ds(SUBLANES, tt), w:2 * w] = k_ref[0]
    xp_ref[pl.ds(SUBLANES, tt), 2 * w:3 * w] = v_ref[0]
    xp_ref[pl.ds(SUBLANES, tt), 3 * w:] = l_ref[0]
    x = xp_ref[pl.ds(SUBLANES, tt), :]
    x = x + (xp_ref[pl.ds(SUBLANES - 1, tt), :] - x) * mu_ref[...]
    xp_ref[pl.ds(0, SUBLANES), :] = xp_ref[pl.ds(tt, SUBLANES), :]

    k = x[:, w:2 * w]
    lora0 = 3 * w
    wd = x[:, lora0:lora0 + RWKV_DECAY_LORA]
    ad = x[:, lora0 + RWKV_DECAY_LORA:lora0 + RWKV_DECAY_LORA + RWKV_LR_LORA]
    gd = x[:, lora0 + RWKV_DECAY_LORA + RWKV_LR_LORA:]
    w_log = -jax.nn.softplus(-(w0_ref[...] + _dot(jnp.tanh(wd), w2_ref[...], HI))) - 0.5
    lr = jax.nn.sigmoid(a0_ref[...] + _dot(ad, a2_ref[...], HI))
    r_s[...] = x[:, 0:w]
    lw_s[...] = -jnp.exp(w_log)
    k2_s[...] = k * (1.0 + (lr - 1.0) * ka_ref[...])
    v_s[...] = x[:, 2 * w:3 * w]
    kk_s[...] = k * kkw_ref[...]
    a_s[...] = lr
    g_s[...] = _dot(jax.nn.sigmoid(gd), g2_ref[...], HI)

    row = _iota2((c, c), 0)
    col = _iota2((c, c), 1)
    tril = row >= col
    strict = row > col
    cum_l = tril.astype(f32)
    last = _iota2((c, 1), 0) == c - 1

    heads = range(RWKV_HEADS)
    per_head = lambda x: [x[:, h * n:(h + 1) * n] for h in heads]

    group = RWKV_CHUNK_GROUP

    def chunk_group(gi, carry):
        rows = [pl.ds(pl.multiple_of((gi * group + j) * c, c), c) for j in range(group)]
        units = lambda f: [x for r in rows for x in f(r)]
        stack = lambda xs, ys: [jnp.concatenate([x, y], axis=0) for x, y in zip(xs, ys)]
        lw_all = [lw_s[r, :] for r in rows]
        p_all = [_dot01(cum_l, x) for x in lw_all]
        em_all = [jnp.exp(-p) for p in p_all]
        rh = units(lambda r: per_head(r_s[r, :]))
        k2 = units(lambda r: per_head(k2_s[r, :]))
        vh = units(lambda r: per_head(v_s[r, :]))
        r_t = [x for r, p in zip(rows, p_all) for x in per_head(r_s[r, :] * jnp.exp(p))]
        k_t = [x for r, e in zip(rows, em_all) for x in per_head(k2_s[r, :] * e)]
        e_prev = [x for p, lw in zip(p_all, lw_all) for x in per_head(jnp.exp(p - lw))]
        lr_em = [x for r, e in zip(rows, em_all) for x in per_head(a_s[r, :] * e)]
        kk = [x * lax.rsqrt(jnp.sum(x * x, axis=-1, keepdims=True) + 1e-6)
              for x in units(lambda r: per_head(kk_s[r, :]))]
        a_t = [-x * e for x, e in zip(kk, e_prev)]
        b_t = [x * e for x, e in zip(kk, lr_em)]
        ar = [x.astype(bf16) for x in stack(a_t, r_t)]
        bk = [x.astype(bf16) for x in stack(b_t, k_t)]
        ar_b = [_dot_nt(x, y[:c]) for x, y in zip(ar, bk)]
        ar_k = [_dot_nt(x, y[c:]) for x, y in zip(ar, bk)]
        m_ab = [jnp.where(strict, x[:c], 0.0) for x in ar_b]
        a_rb = [jnp.where(tril, x[c:], 0.0) for x in ar_b]
        mk_rk = [jnp.where(jnp.concatenate([strict, tril], axis=0), x, 0.0) for x in ar_k]
        eye = (row == col).astype(f32)
        t_inv = [eye + x for x in m_ab]
        pw = [_bdot(x, x) for x in m_ab]
        for _ in range(int(math.log2(c)) - 1):
            z = [_bdot(x, p) for x, p in zip(stack(pw, t_inv), pw)]
            pw = [x[:c] for x in z]
            t_inv = [t + x[c:] for t, x in zip(t_inv, z)]
        mv_yv = [_bdot(x, v) for x, v in zip(mk_rk, vh)]
        w1 = [_bdot(t, a) for t, a in zip(t_inv, a_t)]
        u2 = [_bdot(t, x[:c]) for t, x in zip(t_inv, mv_yv)]
        y_v = [x[c:] for x in mv_yv]
        w1r = stack(w1, r_t)
        decay_last = [x for p in p_all
                      for x in per_head(jnp.exp(jnp.sum(jnp.where(last, p, 0.0), axis=0, keepdims=True)))]
        g_h = units(lambda r: per_head(g_s[r, :]))
        bonus = [jnp.sum(r * k * rk_ref[:, (i % RWKV_HEADS) * n:(i % RWKV_HEADS + 1) * n], axis=-1, keepdims=True) * v
                 for i, (r, k, v) in enumerate(zip(rh, k2, vh))]
        state = [s_ref[h] for h in heads]
        for j in range(group):
            sl_u = slice(j * RWKV_HEADS, (j + 1) * RWKV_HEADS)
            ws = [_bdot_nt(x, s) for x, s in zip(w1r[sl_u], state)]
            u = [x[:c] + y for x, y in zip(ws, u2[sl_u])]
            y = [x[c:] + _bdot(a, z) + yv for x, a, z, yv in zip(ws, a_rb[sl_u], u, y_v[sl_u])]
            ds = [_dot_tn(x.astype(bf16), y) for x, y in zip(stack(u, vh[sl_u]), bk[sl_u])]
            state = [(s + d) * dl for s, d, dl in zip(state, ds, decay_last[sl_u])]
            for h in heads:
                sl = slice(h * n, (h + 1) * n)
                mean = jnp.mean(y[h], axis=-1, keepdims=True)
                yc = y[h] - mean
                var = jnp.mean(yc * yc, axis=-1, keepdims=True)
                yn = yc * lax.rsqrt(var + RWKV_LN_EPS) * lnw_ref[:, sl] + lnb_ref[:, sl]
                o_ref[0, rows[j], sl] = (yn + bonus[j * RWKV_HEADS + h]) * g_h[j * RWKV_HEADS + h]
        for h in heads:
            s_ref[h] = state[h]
        return carry

    lax.fori_loop(0, tt // (c * group), chunk_group, 0)


def rwkv_mixer(f3, mu, w0, w2, a0, a2, g2, k_k, k_a, r_k, ln_w, ln_b, tt=256):
    b, t, _ = f3.shape
    tt = min(tt, t)
    assert t % tt == 0 and tt % (RWKV_CHUNK * RWKV_CHUNK_GROUP) == 0
    w = RWKV_W
    lora = RWKV_DECAY_LORA + RWKV_LR_LORA + RWKV_GATE_LORA
    wide = 3 * w + lora
    row = lambda n: pl.BlockSpec((1, n), lambda bi, ti: (0, 0))
    full = lambda a: pl.BlockSpec(a.shape, lambda bi, ti: (0, 0))
    col = lambda j: pl.BlockSpec((1, tt, w), lambda bi, ti, j=j: (bi, ti, AB_RWKV0 // w + j))
    return pl.pallas_call(
        functools.partial(_rwkv_body, tt=tt),
        grid=(b, t // tt),
        in_specs=[col(0), col(1), col(2),
                  pl.BlockSpec((1, tt, lora), lambda bi, ti: (bi, ti, AB_LORA0 // lora)),
                  row(wide), row(w), full(w2), row(w), full(a2), full(g2)] + [row(w)] * 5,
        out_specs=pl.BlockSpec((1, tt, w), lambda bi, ti: (bi, ti, 0)),
        out_shape=jax.ShapeDtypeStruct((b, t, w), f32),
        scratch_shapes=[pltpu.VMEM((tt + SUBLANES, wide), f32)] + [pltpu.VMEM((tt, w), f32)] * 7
                       + [pltpu.VMEM((RWKV_HEADS, RWKV_N, RWKV_N), f32)],
        compiler_params=_cparams(("parallel", "arbitrary")),
        name="rwkv",
    )(f3, f3, f3, f3, mu.reshape(1, wide), w0.reshape(1, w), w2, a0.reshape(1, w), a2, g2,
      k_k.reshape(1, w), k_a.reshape(1, w), r_k.reshape(1, w), ln_w.reshape(1, w), ln_b.reshape(1, w))


def _rope128(x, cos, sin, sign_lo, sign_hi):
    r_hi = pltpu.roll(x, ROPE_HALF, 1)
    r_lo = pltpu.roll(x, LANES - ROPE_HALF, 1)
    return x * cos + (r_lo * sign_lo + r_hi * sign_hi) * sin


def _rope_tables(pos, freq_row):
    ang = pos * freq_row
    m = _iota2((1, LANES), 1) % NSA_HD
    sign_lo = jnp.where(m < ROPE_HALF, -1.0, 0.0).astype(f32)
    sign_hi = jnp.where((m >= ROPE_HALF) & (m < ROPE_DIM), 1.0, 0.0).astype(f32)
    return jnp.cos(ang), jnp.sin(ang), sign_lo, sign_hi


def _nsa_prep_body(q_ref, kc_i, vc_i, ks_i, vs_i, kw_i, vw_i, pos_ref, freq_ref,
                   qo_ref, kc_ref, vc_ref, ks_ref, vs_ref, kw_ref, vw_ref):
    cos, sin, s_lo, s_hi = _rope_tables(pos_ref[0], freq_ref[...])
    scale = NSA_HD ** -0.5 * math.log2(math.e)
    n_blk = q_ref.shape[1] // NSA_QBLOCK
    eye = (_iota2((LANES, LANES), 0) == _iota2((LANES, LANES), 1)).astype(bf16)

    def transpose_bf16(x):
        return _dot_nt(eye, x.astype(bf16))

    for c in range(NSA_HEADS // 2):
        xt = transpose_bf16(_rope128(q_ref[0, :, c * LANES:(c + 1) * LANES], cos, sin, s_lo, s_hi) * scale)
        for r in range(2):
            head = 2 * c + r
            g, hl = head // NSA_HPG, head % NSA_HPG
            for i in range(n_blk):
                qo_ref[0, g, i, :, hl * NSA_QBLOCK:(hl + 1) * NSA_QBLOCK] = (
                    xt[r * NSA_HD:(r + 1) * NSA_HD, i * NSA_QBLOCK:(i + 1) * NSA_QBLOCK].astype(bf16))

    def split(src, ref, rope, dtype):
        x = src[0]
        if rope:
            x = _rope128(x, cos, sin, s_lo, s_hi)
        ref[0, 0] = x[:, :NSA_HD].astype(dtype)
        ref[0, 1] = x[:, NSA_HD:].astype(dtype)

    def split_t(src, ref):
        xt = transpose_bf16(src[0])
        extra = (_iota2((NSA_VT_ROWS - NSA_HD, xt.shape[1]), 0) == 0).astype(f32)
        for g in range(NSA_GROUPS):
            ref[0, g] = jnp.concatenate([xt[g * NSA_HD:(g + 1) * NSA_HD], extra], axis=0).astype(bf16)

    split(kc_i, kc_ref, False, f32)
    split(vc_i, vc_ref, False, f32)
    tt = ks_i.shape[1]
    tok = pl.program_id(1) * tt + _iota2((tt, NSA_HD), 0)
    block_onehot = (tok // SEL_BLOCK == _iota2((tt, NSA_HD), 1)).astype(f32)
    ks = _rope128(ks_i[0], cos, sin, s_lo, s_hi)
    for g in range(NSA_GROUPS):
        ks_ref[0, g] = jnp.concatenate([ks[:, g * NSA_HD:(g + 1) * NSA_HD], block_onehot], axis=1).astype(bf16)
    split_t(vs_i, vs_ref)
    split(kw_i, kw_ref, True, bf16)
    split_t(vw_i, vw_ref)


def nsa_prep(f3, pos3, freq_row, tt=512):
    b, t, _ = f3.shape
    assert t // SEL_BLOCK <= NSA_HD, "the selection-block one-hot shares the key tile's second 64 lanes"
    tt = min(tt, t)
    g = NSA_GROUPS
    n_blk = tt // NSA_QBLOCK
    kv_in = [pl.BlockSpec((1, tt, LANES), lambda bi, ti, c=NSA_KV0 // LANES + i: (bi, ti, c)) for i in range(6)]
    kv_spec = pl.BlockSpec((1, g, tt, NSA_HD), lambda bi, ti: (bi, 0, ti, 0))
    kvt_spec = pl.BlockSpec((1, g, NSA_VT_ROWS, tt), lambda bi, ti: (bi, 0, 0, ti))
    kv32 = jax.ShapeDtypeStruct((b, g, t, NSA_HD), f32)
    kv16 = jax.ShapeDtypeStruct((b, g, t, NSA_HD), bf16)
    kvt16 = jax.ShapeDtypeStruct((b, g, NSA_VT_ROWS, t), bf16)
    q_lanes = NSA_HPG * NSA_QBLOCK
    return pl.pallas_call(
        _nsa_prep_body,
        grid=(b, t // tt),
        in_specs=[pl.BlockSpec((1, tt, NSA_HEADS * NSA_HD), lambda bi, ti: (bi, ti, 0))] + kv_in
                 + [pl.BlockSpec((1, tt, 1), lambda bi, ti: (bi, ti, 0)),
                    pl.BlockSpec((1, LANES), lambda bi, ti: (0, 0))],
        out_specs=[pl.BlockSpec((1, g, n_blk, NSA_HD, q_lanes), lambda bi, ti: (bi, 0, ti, 0, 0)),
                   kv_spec, kv_spec, pl.BlockSpec((1, g, tt, 2 * NSA_HD), lambda bi, ti: (bi, 0, ti, 0)),
                   kvt_spec, kv_spec, kvt_spec],
        out_shape=[jax.ShapeDtypeStruct((b, g, t // NSA_QBLOCK, NSA_HD, q_lanes), bf16),
                   kv32, kv32, jax.ShapeDtypeStruct((b, g, t, 2 * NSA_HD), bf16), kvt16, kv16, kvt16],
        compiler_params=_cparams(("parallel", "parallel")),
        name="nsa_prep",
    )(f3, f3, f3, f3, f3, f3, f3, pos3, freq_row)


def _nsa_compress_body(kc_ref, vc_ref, pek_ref, w1k_ref, w2k_ref, pev_ref, w1v_ref, w2v_ref, pos_ref, freq_ref,
                       ko_ref, vo_ref):
    half = CMP_STRIDE * NSA_HD
    nrow = kc_ref.shape[2]
    last_row = _iota2((nrow, 1), 0) == nrow - 1

    def hidden(x, pe_ref, w1_ref):
        lo = _dot((x + pe_ref[:, :half]).astype(bf16), w1_ref[:half, :])
        hi = _dot((x + pe_ref[:, half:]).astype(bf16), w1_ref[half:, :])
        hi = jnp.where(last_row, 0.0, pltpu.roll(hi, nrow - 1, 0))
        hid = lo + hi
        return (hid * jax.nn.sigmoid(hid)).astype(bf16)

    ks = [_dot(hidden(kc_ref[0, g], pek_ref, w1k_ref), w2k_ref[...]) for g in range(NSA_GROUPS)]
    cos, sin, s_lo, s_hi = _rope_tables(pos_ref[0], freq_ref[...])
    kr = _rope128(jnp.concatenate(ks, axis=-1), cos, sin, s_lo, s_hi)
    for g in range(NSA_GROUPS):
        ko_ref[0, g] = kr[:, g * NSA_HD:(g + 1) * NSA_HD].astype(bf16)
        vo_ref[0, g] = _dot_nt(w2v_ref[...], hidden(vc_ref[0, g], pev_ref, w1v_ref)).astype(bf16)


def nsa_compress(kc4, vc4, pe_k, w1_k, w2_k, pe_v, w1_v, w2_vt, cpos3, freq_row):
    b, g, nrow, wide = kc4.shape
    full = lambda a: pl.BlockSpec(a.shape, lambda bi: (0,) * a.ndim)
    blk = pl.BlockSpec((1, g, nrow, wide), lambda bi: (bi, 0, 0, 0))
    out = pl.BlockSpec((1, g, nrow, NSA_HD), lambda bi: (bi, 0, 0, 0))
    out_t = pl.BlockSpec((1, g, NSA_HD, nrow), lambda bi: (bi, 0, 0, 0))
    return pl.pallas_call(
        _nsa_compress_body,
        grid=(b,),
        in_specs=[blk, blk, full(pe_k), full(w1_k), full(w2_k), full(pe_v), full(w1_v), full(w2_vt),
                  pl.BlockSpec((1, nrow, 1), lambda bi: (bi, 0, 0)), pl.BlockSpec((1, LANES), lambda bi: (0, 0))],
        out_specs=[out, out_t],
        out_shape=[jax.ShapeDtypeStruct((b, g, nrow, NSA_HD), bf16),
                   jax.ShapeDtypeStruct((b, g, NSA_HD, nrow), bf16)],
        compiler_params=_cparams(("parallel",)),
        name="nsa_compress",
    )(kc4, vc4, pe_k, w1_k, w2_k, pe_v, w1_v, w2_vt, cpos3, freq_row)


def _nsa_attn_body(qt_ref, kc_ref, vct_ref, ks_ref, vst_ref, kw_ref, vwt_ref, gate_ref, ovt_ref, dbias_ref,
                   wbias_ref, rep_ref, o_ref, score_ref, s_ref, acc_ref, sw_ref, *, key_tile):
    qb_n = NSA_QBLOCK
    blk_lanes = NSA_HPG * qb_n
    lanes = NSA_QPAIR * blk_lanes
    pair = pl.program_id(2)
    qt = jnp.concatenate([qt_ref[0, 0, a] for a in range(NSA_QPAIR)], axis=1)
    lane = _iota2((1, lanes), 1)
    t_lane = (pair * NSA_QPAIR + lane // blk_lanes) * qb_n + lane % qb_n

    def softmax_t(s):
        mx = jnp.max(s, axis=0, keepdims=True)
        mx = jnp.where(mx > -jnp.inf, mx, 0.0)
        e = jnp.exp2(s - mx)
        return e, jnp.sum(e, axis=0, keepdims=True), mx

    diag0 = pair * NSA_QPAIR
    n_tiles = (diag0 * SEL_BLOCK + key_tile - 1) // key_tile
    last_tile = ks_ref.shape[2] // key_tile - 1

    gt = jax.nn.sigmoid(gate_ref[0]).T
    gate = lambda br: jnp.concatenate([gt[3 * h + br:3 * h + br + 1, a * qb_n:(a + 1) * qb_n]
                                       for a in range(NSA_QPAIR) for h in range(NSA_HPG)], axis=1)
    gates = [gate(br) for br in range(3)]

    s_cmp = _dot(kc_ref[0, 0], qt)
    span = WINDOW + NSA_QPAIR * qb_n
    w0 = pl.multiple_of(pair * NSA_QPAIR * qb_n, LANES)
    sw_ref[...] = _dot(kw_ref[0, 0, pl.ds(w0, span), :], qt) + wbias_ref[...]
    k0 = pl.multiple_of(diag0 * SEL_BLOCK, NSA_QPAIR * SEL_BLOCK)
    s_diag = _dot(ks_ref[0, 0, pl.ds(k0, NSA_QPAIR * SEL_BLOCK), 0:NSA_HD], qt) + dbias_ref[...]

    n_cmp = kc_ref.shape[2]
    cmp_end = _iota2((n_cmp, 1), 0) * CMP_STRIDE + (CMP_LEN - 1)
    e_c, den_c, _ = softmax_t(jnp.where(cmp_end <= t_lane, s_cmp, -jnp.inf))
    p_c = e_c * (1.0 / jnp.maximum(den_c, 1e-30))
    o_c = _dot(vct_ref[0, 0], p_c.astype(bf16))

    n_sel = ovt_ref.shape[0]
    p_pair = []
    for a in range(NSA_QPAIR):
        acc = p_c[:, a * blk_lanes:a * blk_lanes + LANES]
        for c in range(1, blk_lanes // LANES):
            acc = acc + p_c[:, a * blk_lanes + c * LANES:a * blk_lanes + (c + 1) * LANES]
        p_pair.append(acc[:, :qb_n] + acc[:, qb_n:])
    imp = _dot(ovt_ref[...], jnp.concatenate(p_pair, axis=1), HI)
    blk = _iota2((n_sel, NSA_QPAIR * qb_n), 0)
    cur = pair * NSA_QPAIR + _iota2((n_sel, NSA_QPAIR * qb_n), 1) // qb_n
    valid = blk <= cur
    forced = (blk == 0) | (blk == cur) | (blk == cur - 1)
    score = jnp.where(valid, jnp.where(forced, jnp.inf, imp), -jnp.inf)
    score_ref[...] = score

    e_d, _, m_s = softmax_t(s_diag)
    acc_ref[...] = _dot(vst_ref[0, 0, :, pl.ds(k0, NSA_QPAIR * SEL_BLOCK)], e_d.astype(bf16))

    @pl.when(w0 < WINDOW)
    def _():
        is_pad = _iota2((WINDOW, 1), 0) < WINDOW - w0
        sw_ref[pl.ds(0, WINDOW), :] = jnp.where(is_pad, -jnp.inf, sw_ref[pl.ds(0, WINDOW), :])

    def rank_step(jp, cnt):
        other = score_ref[pl.ds(jp, 1), :]
        ahead = (other > score) | ((other == score) & (blk > jp))
        return cnt + ahead.astype(jnp.int32)

    def rank_step2(jj, cnt):
        return rank_step(2 * jj + 1, rank_step(2 * jj, cnt))

    n_cand = pair * NSA_QPAIR + NSA_QPAIR
    cnt = lax.fori_loop(0, jnp.where(n_cand > SEL_TOPN, n_cand // 2, 0), rank_step2,
                        jnp.zeros((n_sel, NSA_QPAIR * qb_n), jnp.int32))
    bias = jnp.where((cnt < SEL_TOPN) & (blk < diag0), 0.0, NSA_MASKED).astype(bf16)
    bias = _dot(bias, rep_ref[...]).astype(bf16)
    if n_sel < NSA_HD:
        bias = jnp.concatenate([bias, jnp.zeros((NSA_HD - n_sel, lanes), bf16)], axis=0)
    qx = jnp.concatenate([qt, bias], axis=0)

    def score_tile(kt):
        kk0 = pl.multiple_of(kt * key_tile, key_tile)
        return _dot(ks_ref[0, 0, pl.ds(kk0, key_tile), :], qx)

    s_ref[0] = score_tile(0)

    e_w, _, _ = softmax_t(sw_ref[...])
    o_w = _dot(vwt_ref[0, 0, :, pl.ds(w0, span)], e_w.astype(bf16))

    def half_step(kt, slot, carry):
        m_old = carry
        s_ref[1 - slot] = score_tile(jnp.minimum(kt + 1, last_tile))
        sm = s_ref[slot]
        mx = jnp.maximum(m_old, jnp.max(sm, axis=0, keepdims=True))
        alpha = jnp.exp2(m_old - mx)
        p = jnp.exp2(sm - mx)
        kk0 = pl.multiple_of(kt * key_tile, key_tile)
        acc_ref[...] = alpha * acc_ref[...] + _dot(vst_ref[0, 0, :, pl.ds(kk0, key_tile)], p.astype(bf16))
        return mx

    def pair_step(pi, carry):
        return half_step(2 * pi + 1, 1, half_step(2 * pi, 0, carry))

    lax.fori_loop(0, (n_tiles + 1) // 2, pair_step, m_s)

    l_s = acc_ref[pl.ds(NSA_HD, 1), :]
    l_w = o_w[NSA_HD:NSA_HD + 1]
    o_t = (gates[0] * o_c + (gates[1] * (1.0 / l_s)) * acc_ref[pl.ds(0, NSA_HD), :]
           + (gates[2] * (1.0 / l_w)) * o_w[:NSA_HD])
    o_b = o_t.astype(bf16)
    q_idx = _iota2((qb_n, LANES), 0)
    l_idx = _iota2((qb_n, LANES), 1)
    pick = [(l_idx == q_idx + r * qb_n).astype(bf16) for r in range(LANES // qb_n)]
    for a in range(NSA_QPAIR):
        for c in range(blk_lanes // LANES):
            piece = o_b[:, a * blk_lanes + c * LANES:a * blk_lanes + (c + 1) * LANES]
            for r in range(LANES // qb_n):
                h = c * (LANES // qb_n) + r
                o_ref[0, a * qb_n:(a + 1) * qb_n, h * NSA_HD:(h + 1) * NSA_HD] = (
                    _dot_nt(pick[r], piece).astype(bf16))


def nsa_attention(qt, kc, vct, ks, vst, kw, vwt, f3, overlap_t, key_tile=256):
    b, g, n_q, d, blk_lanes = qt.shape
    t = n_q * NSA_QBLOCK
    key_tile = min(key_tile, t)
    assert (t // key_tile) % 2 == 0, "an odd tile count borrows the (fully unselected) tile after the last one"
    n_cmp = kc.shape[2]
    n_sel = overlap_t.shape[0]
    lanes = NSA_QPAIR * blk_lanes
    q_rows = NSA_QPAIR * NSA_QBLOCK
    off = (jnp.arange(lanes) // blk_lanes) * NSA_QBLOCK + jnp.arange(lanes) % NSA_QBLOCK
    r_d = jnp.arange(NSA_QPAIR * SEL_BLOCK)[:, None]
    diag_bias = jnp.where(r_d <= off[None, :], 0.0, -jnp.inf).astype(f32)
    r_w = jnp.arange(WINDOW + q_rows)[:, None]
    win_bias = jnp.where((r_w > off[None, :]) & (r_w <= WINDOW + off[None, :]), 0.0, -jnp.inf).astype(f32)
    rep = (jnp.arange(q_rows)[:, None] == off[None, :]).astype(bf16)
    seq = pl.BlockSpec((1, 1, t, ks.shape[3]), lambda bi, gi, qi: (bi, gi, 0, 0))
    seq_t = pl.BlockSpec((1, 1, NSA_VT_ROWS, t), lambda bi, gi, qi: (bi, gi, 0, 0))
    const = lambda a: pl.BlockSpec(a.shape, lambda bi, gi, qi: (0, 0))
    return pl.pallas_call(
        functools.partial(_nsa_attn_body, key_tile=key_tile),
        grid=(b, g, n_q // NSA_QPAIR),
        in_specs=[pl.BlockSpec((1, 1, NSA_QPAIR, d, blk_lanes), lambda bi, gi, qi: (bi, gi, qi, 0, 0)),
                  pl.BlockSpec((1, 1, n_cmp, d), lambda bi, gi, qi: (bi, gi, 0, 0)),
                  pl.BlockSpec((1, 1, d, n_cmp), lambda bi, gi, qi: (bi, gi, 0, 0)),
                  seq, seq_t,
                  pl.BlockSpec((1, 1, t + WINDOW, d), lambda bi, gi, qi: (bi, gi, 0, 0)),
                  pl.BlockSpec((1, 1, NSA_VT_ROWS, t + WINDOW), lambda bi, gi, qi: (bi, gi, 0, 0)),
                  pl.BlockSpec((1, q_rows, LANES), lambda bi, gi, qi: (bi, qi, NSA_GATE0 // LANES + gi)),
                  const(overlap_t), const(diag_bias), const(win_bias), const(rep)],
        out_specs=pl.BlockSpec((1, q_rows, NSA_HPG * d), lambda bi, gi, qi: (bi, qi, gi)),
        out_shape=jax.ShapeDtypeStruct((b, t, g * NSA_HPG * d), bf16),
        scratch_shapes=[pltpu.VMEM((n_sel, q_rows), f32),
                        pltpu.VMEM((2, key_tile, lanes), f32), pltpu.VMEM((NSA_VT_ROWS, lanes), f32),
                        pltpu.VMEM((WINDOW + q_rows, lanes), f32)],
        compiler_params=_cparams(("parallel", "parallel", "arbitrary")),
        name="nsa_attn",
    )(qt, kc, vct, ks, vst, kw, vwt, f3, overlap_t, diag_bias, win_bias, rep)


def _place(cols, total, pieces):
    out = jnp.zeros((cols, total), f32)
    for start, mat in pieces:
        out = lax.dynamic_update_slice(out, mat.astype(f32), (0, start))
    return out


def _layer0(h, b, t, g_pre, w_in, w_out, conv, a_log, dt_bias, gnorm, mu, w0, w2, a0, a2, g2, k_k, k_a,
            r_k, ln_w, ln_b):
    gdn_w = 4 * GDN_HEADS * GDN_D
    w_pad = _place(D_MODEL, AB_COLS, [
        (0, w_in[:, :gdn_w]),
        (AB_BA0, w_in[:, gdn_w:gdn_w + 2 * GDN_HEADS]),
        (AB_RWKV0, w_in[:, gdn_w + 2 * GDN_HEADS:gdn_w + 2 * GDN_HEADS + 3 * RWKV_W]),
        (AB_LORA0, w_in[:, gdn_w + 2 * GDN_HEADS + 3 * RWKV_W:]),
    ]).astype(bf16)
    f3 = norm_matmul(h, g_pre, w_pad).reshape(b, t, AB_COLS)
    arow = jnp.zeros((1, LANES), f32).at[0, GDN_HEADS:2 * GDN_HEADS].set(a_log)
    dtrow = jnp.zeros((1, LANES), f32).at[0, GDN_HEADS:2 * GDN_HEADS].set(dt_bias)
    conv3 = conv.reshape(GDN_CONV, 3, GDN_HEADS * GDN_D).transpose(1, 0, 2)
    o_a = gdn_mixer(f3, conv3, arow, dtrow, gnorm.reshape(1, GDN_D))
    o_b = rwkv_mixer(f3, mu, w0, w2, a0, a2, g2, k_k, k_a, r_k, ln_w, ln_b)
    m = b * t
    n_a = GDN_HEADS * GDN_D
    return [o_a.reshape(m, n_a), o_b.reshape(m, RWKV_W)], [w_out[:n_a].astype(bf16), w_out[n_a:].astype(bf16)]


def _layer1(h, b, t, positions, g_pre, w_in, w_out, pe_k, w1_k, w2_k, pe_v, w1_v, w2_v):
    qw = NSA_HEADS * NSA_HD
    kvw = 6 * NSA_GROUPS * NSA_HD
    gates = w_in[:, qw + kvw:].reshape(D_MODEL, NSA_GROUPS, NSA_HPG * 3)
    w_pad = _place(D_MODEL, NSA_COLS, [(0, w_in[:, :qw + kvw])]
                   + [(NSA_GATE0 + gi * LANES, gates[:, gi]) for gi in range(NSA_GROUPS)]).astype(bf16)
    f3 = norm_matmul(h, g_pre, w_pad).reshape(b, t, NSA_COLS)
    inv_freq = ROPE_THETA ** (-jnp.arange(ROPE_HALF, dtype=f32) * (2.0 / ROPE_DIM))
    lane = jnp.arange(LANES)
    freq_row = jnp.where(lane % NSA_HD < ROPE_DIM, inv_freq[lane % ROPE_HALF], 0.0).reshape(1, LANES).astype(f32)
    posf = positions.astype(f32)
    qt, kc, vc, ks, vst, kw, vwt = nsa_prep(f3, posf.reshape(b, t, 1), freq_row)
    nrow = t // CMP_STRIDE
    cpos = jnp.concatenate([posf[:, CMP_LEN - 1::CMP_STRIDE], posf[:, -1:]], axis=1).reshape(b, nrow, 1)
    flat = lambda a: a.reshape(b, NSA_GROUPS, nrow, CMP_STRIDE * NSA_HD)
    kcc, vcct = nsa_compress(flat(kc), flat(vc), pe_k.reshape(1, -1), w1_k.astype(bf16), w2_k.astype(bf16),
                             pe_v.reshape(1, -1), w1_v.astype(bf16), w2_v.T.astype(bf16), cpos, freq_row)
    n_sel = t // SEL_BLOCK
    c_start = jnp.arange(nrow) * CMP_STRIDE
    s_start = jnp.arange(n_sel) * SEL_BLOCK
    overlap_t = jnp.clip(jnp.minimum(c_start[None, :] + CMP_LEN, s_start[:, None] + SEL_BLOCK)
                         - jnp.maximum(c_start[None, :], s_start[:, None]), 0, None).astype(f32) / CMP_LEN
    kw_pad = jnp.pad(kw, ((0, 0), (0, 0), (WINDOW, 0), (0, 0)))
    vwt_pad = jnp.pad(vwt, ((0, 0), (0, 0), (0, 0), (WINDOW, 0)))
    o = nsa_attention(qt, kcc, vcct, ks, vst, kw_pad, vwt_pad, f3, overlap_t)
    return [o.reshape(b * t, qw)], [w_out.astype(bf16)]


def kernel(x, positions, norm_mix_pre, norm_mix_post, norm_ffn_pre, norm_ffn_post, w_ffn_up, w_ffn_down, ab_w_in,
           ab_w_out, gdn_conv, gdn_a_log, gdn_dt_bias, gdn_norm, rwkv_mu, rwkv_w0, rwkv_w2, rwkv_a0, rwkv_a2,
           rwkv_g2, rwkv_k_k, rwkv_k_a, rwkv_r_k, rwkv_ln_w, rwkv_ln_b, nsa_w_in, nsa_w_out, nsa_pe_k, nsa_w1_k,
           nsa_w2_k, nsa_pe_v, nsa_w1_v, nsa_w2_v):
    b, t, d = x.shape
    h = x.reshape(b * t, d)
    mix = _layer0(h, b, t, norm_mix_pre[0], ab_w_in[0], ab_w_out[0], gdn_conv[0], gdn_a_log[0],
                  gdn_dt_bias[0], gdn_norm[0], rwkv_mu[0], rwkv_w0[0], rwkv_w2[0], rwkv_a0[0], rwkv_a2[0],
                  rwkv_g2[0], rwkv_k_k[0], rwkv_k_a[0], rwkv_r_k[0].reshape(-1), rwkv_ln_w[0], rwkv_ln_b[0])
    h = out_ffn(*mix, h, norm_mix_post[0], norm_ffn_pre[0], w_ffn_up[0].astype(bf16), w_ffn_down[0].astype(bf16),
                norm_ffn_post[0])
    mix = _layer1(h, b, t, positions, norm_mix_pre[1], nsa_w_in[0], nsa_w_out[0], nsa_pe_k[0],
                  nsa_w1_k[0], nsa_w2_k[0], nsa_pe_v[0], nsa_w1_v[0], nsa_w2_v[0])
    h = out_ffn(*mix, h, norm_mix_post[1], norm_ffn_pre[1], w_ffn_up[1].astype(bf16), w_ffn_down[1].astype(bf16),
                norm_ffn_post[1])
    return h.reshape(b, t, d)
```

```python
import functools
import math

import jax
import jax.numpy as jnp
from jax import lax
from jax.experimental import pallas as pl
from jax.experimental.pallas import tpu as pltpu

f32 = jnp.float32
bf16 = jnp.bfloat16
HI = lax.Precision.HIGHEST

V7X_VMEM_LIMIT_BYTES = 56 * 1024 * 1024
LANES = 128
SUBLANES = 8

D_MODEL = 1024
D_FF = 4 * D_MODEL
NORM_EPS = 1e-6
GDN_HEADS = 4
GDN_D = 128
GDN_CONV = 4
GDN_CHUNK = 128
GDN_CHUNK_GROUP = 4
RWKV_HEADS = 8
RWKV_N = 64
RWKV_W = RWKV_HEADS * RWKV_N
RWKV_CHUNK = 64
RWKV_CHUNK_GROUP = 4
RWKV_LN_EPS = 64e-5
RWKV_DECAY_LORA = 64
RWKV_LR_LORA = 64
RWKV_GATE_LORA = 128
NSA_HEADS = 16
NSA_GROUPS = 2
NSA_HPG = NSA_HEADS // NSA_GROUPS
NSA_HD = 64
CMP_LEN = 32
CMP_STRIDE = 16
CMP_HIDDEN = 256
SEL_BLOCK = 64
SEL_TOPN = 16
WINDOW = 512
NSA_QBLOCK = 64
NSA_QPAIR = 2
NSA_VT_ROWS = 80
NSA_MASKED = -(2.0 ** 126)
ROPE_THETA = 500000.0
ROPE_DIM = NSA_HD // 4
ROPE_HALF = ROPE_DIM // 2

AB_COLS = 4096
AB_RWKV0 = 2048
AB_LORA0 = 3584
AB_BA0 = 3840
NSA_COLS = 2048
NSA_KV0 = 1024
NSA_GATE0 = 1792


def _cparams(sem):
    return pltpu.CompilerParams(dimension_semantics=sem, vmem_limit_bytes=V7X_VMEM_LIMIT_BYTES)


def _rms(x, g):
    return x * lax.rsqrt(jnp.mean(x * x, axis=-1, keepdims=True) + NORM_EPS) * g


def _dot(a, b, precision=None):
    return jnp.dot(a, b, precision=precision, preferred_element_type=f32)


def _dot_nt(a, b, precision=None):
    return lax.dot_general(a, b, (((1,), (1,)), ((), ())), precision=precision, preferred_element_type=f32)


def _dot_tn(a, b, precision=None):
    return lax.dot_general(a, b, (((0,), (0,)), ((), ())), precision=precision, preferred_element_type=f32)


def _bdot(a, b):
    return _dot(a.astype(bf16), b.astype(bf16))


def _bdot_nt(a, b):
    return _dot_nt(a.astype(bf16), b.astype(bf16))


def _bdot_tn(a, b):
    return _dot_tn(a.astype(bf16), b.astype(bf16))


def _dot01(m01, x):
    m = m01.astype(bf16)
    hi = x.astype(bf16)
    rest = x - hi.astype(f32)
    mid = rest.astype(bf16)
    lo = (rest - mid.astype(f32)).astype(bf16)
    return _dot(m, hi) + _dot(m, mid) + _dot(m, lo)


def _iota2(shape, axis):
    return lax.broadcasted_iota(jnp.int32, shape, axis)


def _norm_matmul_body(x_ref, g_ref, w_ref, o_ref, u_ref):
    @pl.when(pl.program_id(1) == 0)
    def _():
        u_ref[...] = _rms(x_ref[...], g_ref[...]).astype(bf16)

    o_ref[...] = _dot(u_ref[...], w_ref[...])


def norm_matmul(x, g, w, tm=2048, tn=1024):
    m, d = x.shape
    n = w.shape[1]
    tm = min(tm, m)
    return pl.pallas_call(
        _norm_matmul_body,
        grid=(m // tm, n // tn),
        in_specs=[pl.BlockSpec((tm, d), lambda i, j: (i, 0)),
                  pl.BlockSpec((1, d), lambda i, j: (0, 0)),
                  pl.BlockSpec((d, tn), lambda i, j: (0, j))],
        out_specs=pl.BlockSpec((tm, tn), lambda i, j: (i, j)),
        out_shape=jax.ShapeDtypeStruct((m, n), f32),
        scratch_shapes=[pltpu.VMEM((tm, d), bf16)],
        compiler_params=_cparams(("parallel", "arbitrary")),
        name="norm_matmul",
    )(x, g.reshape(1, d), w)


def _out_ffn_body(*refs, n_parts):
    a_refs = refs[:n_parts]
    w_refs = refs[n_parts:2 * n_parts]
    h_ref, gm_ref, g1_ref, wup_ref, wdn_ref, g2_ref, o_ref, h1_ref, u_ref, acc_ref = refs[2 * n_parts:]
    k = pl.program_id(1)

    @pl.when(k == 0)
    def _():
        y = _dot(a_refs[0][...].astype(bf16), w_refs[0][...])
        for a_ref, w_ref in zip(a_refs[1:], w_refs[1:]):
            y = y + _dot(a_ref[...].astype(bf16), w_ref[...])
        h1 = h_ref[...] + _rms(y, gm_ref[...])
        h1_ref[...] = h1
        u_ref[...] = _rms(h1, g1_ref[...]).astype(bf16)
        acc_ref[...] = jnp.zeros_like(acc_ref)

    a = _dot(u_ref[...], wup_ref[...])
    a = jnp.square(jnp.maximum(a, 0.0))
    acc_ref[...] += _dot(a.astype(bf16), wdn_ref[...])

    @pl.when(k == pl.num_programs(1) - 1)
    def _():
        o_ref[...] = h1_ref[...] + _rms(acc_ref[...], g2_ref[...])


def out_ffn(parts, w_parts, h, g_mix, g1, wup, wdn, g2, tm=1024, tf=1024):
    m, d = h.shape
    ff = wup.shape[1]
    tm = min(tm, m)
    n_parts = len(parts)
    row = pl.BlockSpec((1, d), lambda i, k: (0, 0))
    in_specs = ([pl.BlockSpec((tm, p.shape[1]), lambda i, k: (i, 0)) for p in parts]
                + [pl.BlockSpec(w.shape, lambda i, k: (0, 0)) for w in w_parts]
                + [pl.BlockSpec((tm, d), lambda i, k: (i, 0)), row, row,
                   pl.BlockSpec((d, tf), lambda i, k: (0, k)),
                   pl.BlockSpec((tf, d), lambda i, k: (k, 0)), row])
    return pl.pallas_call(
        functools.partial(_out_ffn_body, n_parts=n_parts),
        grid=(m // tm, ff // tf),
        in_specs=in_specs,
        out_specs=pl.BlockSpec((tm, d), lambda i, k: (i, 0)),
        out_shape=jax.ShapeDtypeStruct((m, d), f32),
        scratch_shapes=[pltpu.VMEM((tm, d), f32), pltpu.VMEM((tm, d), bf16), pltpu.VMEM((tm, d), f32)],
        compiler_params=_cparams(("parallel", "arbitrary")),
        name="out_ffn",
    )(*parts, *w_parts, h, g_mix.reshape(1, d), g1.reshape(1, d), wup, wdn, g2.reshape(1, d))


def _neumann_inverses(n_mats, size):
    eye = (_iota2((size, size), 0) == _iota2((size, size), 1)).astype(f32)
    ts = [eye + n for n in n_mats]
    ps = list(n_mats)
    for _ in range(int(math.log2(size)) - 1):
        ps = [_bdot(p, p) for p in ps]
        ts = [t + _bdot(t, p) for t, p in zip(ts, ps)]
    return ts


def _gdn_body(q_ref, k_ref, v_ref, z_ref, ba_ref, cw_ref, arow_ref, dtrow_ref, nw_ref, o_ref,
              xp_ref, qkv_ref, s_ref, *, tt):
    c = GDN_CHUNK
    d = GDN_D
    heads = range(GDN_HEADS)
    width = GDN_HEADS * d

    @pl.when(pl.program_id(1) == 0)
    def _():
        xp_ref[:, pl.ds(0, SUBLANES), :] = jnp.zeros((3, SUBLANES, width), f32)
        s_ref[...] = jnp.zeros_like(s_ref)

    conv_rows = min(tt, 64)
    for idx, ref in enumerate((q_ref, k_ref, v_ref)):
        xp_ref[idx, pl.ds(SUBLANES, tt), :] = ref[0]
        for h in heads:
            cols = slice(h * d, (h + 1) * d)
            w = cw_ref[idx, :, cols]
            for r0 in range(0, tt, conv_rows):
                y = xp_ref[idx, pl.ds(SUBLANES + r0, conv_rows), cols] * w[GDN_CONV - 1:GDN_CONV, :]
                for j in range(GDN_CONV - 1):
                    y = y + xp_ref[idx, pl.ds(SUBLANES - (GDN_CONV - 1) + j + r0, conv_rows), cols] * w[j:j + 1, :]
                y = y * jax.nn.sigmoid(y)
                if idx < 2:
                    y = y * lax.rsqrt(jnp.sum(y * y, axis=-1, keepdims=True) + 1e-6)
                if idx == 0:
                    y = y * (d ** -0.5)
                qkv_ref[idx, pl.ds(r0, conv_rows), cols] = y
        xp_ref[idx, pl.ds(0, SUBLANES), :] = xp_ref[idx, pl.ds(tt, SUBLANES), :]

    row = _iota2((c, c), 0)
    col = _iota2((c, c), 1)
    tril = row >= col
    strict = row > col
    eye = row == col
    cum_l = tril.astype(f32)
    last_row = _iota2((c, 1), 0) == c - 1

    group = GDN_CHUNK_GROUP
    nh = GDN_HEADS

    def chunk_group(gi, carry):
        rows = [pl.ds(pl.multiple_of((gi * group + j) * c, c), c) for j in range(group)]
        sig, gcum = [], []
        for r in rows:
            ba = ba_ref[0, r, :]
            sig.append(jax.nn.sigmoid(ba))
            gcum.append(_dot01(cum_l, -jnp.exp(arow_ref[...]) * jax.nn.softplus(ba + dtrow_ref[...])))
        units = lambda f: [f(j, h) for j in range(group) for h in heads]
        qn = units(lambda j, h: qkv_ref[0, rows[j], h * d:(h + 1) * d])
        kn = units(lambda j, h: qkv_ref[1, rows[j], h * d:(h + 1) * d])
        vv = units(lambda j, h: qkv_ref[2, rows[j], h * d:(h + 1) * d])
        beta = units(lambda j, h: sig[j][:, h:h + 1])
        gc = units(lambda j, h: gcum[j][:, nh + h:nh + h + 1])
        gc_row = [jnp.sum(jnp.where(eye, jnp.broadcast_to(g, (c, c)), 0.0), axis=0, keepdims=True) for g in gc]
        gc_last = [jnp.sum(jnp.where(last_row, g, 0.0), axis=0, keepdims=True) for g in gc]
        decay = [jnp.exp(jnp.where(tril, g - gr, -jnp.inf)) for g, gr in zip(gc, gc_row)]
        knb = [k.astype(bf16) for k in kn]
        kk = [_dot_nt(k, k) for k in knb]
        qk = [_dot_nt(q.astype(bf16), k) for q, k in zip(qn, knb)]
        t_inv = _neumann_inverses([-jnp.where(strict, b * x * dc, 0.0) for b, x, dc in zip(beta, kk, decay)], c)
        egc = [jnp.exp(g) for g in gc]
        u = [_bdot(t, v * b) for t, v, b in zip(t_inv, vv, beta)]
        w = [_bdot(t, k * (b * e)).astype(bf16) for t, k, b, e in zip(t_inv, kn, beta, egc)]
        intra = [jnp.where(tril, x * dc, 0.0).astype(bf16) for x, dc in zip(qk, decay)]
        q_g = [(q * e).astype(bf16) for q, e in zip(qn, egc)]
        k_g = [(k * jnp.exp(gl - g)).astype(bf16) for k, gl, g in zip(kn, gc_last, gc)]
        state = [s_ref[h] for h in heads]
        for j in range(group):
            sl = slice(j * nh, (j + 1) * nh)
            sb = [x.astype(bf16) for x in state]
            v_new = [(x - _dot(y, z)).astype(bf16) for x, y, z in zip(u[sl], w[sl], sb)]
            o = [_dot(q, z) + _dot(a, vn) for q, z, a, vn in zip(q_g[sl], sb, intra[sl], v_new)]
            state = [s * jnp.exp(gl) + _dot_tn(k, vn) for s, gl, k, vn in zip(state, gc_last[sl], k_g[sl], v_new)]
            for h in heads:
                z = z_ref[0, rows[j], h * d:(h + 1) * d]
                o_ref[0, rows[j], h * d:(h + 1) * d] = _rms(o[h], nw_ref[...]) * (z * jax.nn.sigmoid(z))
        for h in heads:
            s_ref[h] = state[h]
        return carry

    lax.fori_loop(0, tt // (c * group), chunk_group, 0)


def gdn_mixer(f3, conv_w, arow, dtrow, norm_w, tt=512):
    b, t, _ = f3.shape
    tt = min(tt, t)
    assert t % tt == 0 and tt % (GDN_CHUNK * GDN_CHUNK_GROUP) == 0
    width = GDN_HEADS * GDN_D
    col = lambda j: pl.BlockSpec((1, tt, width), lambda bi, ti, j=j: (bi, ti, j))
    return pl.pallas_call(
        functools.partial(_gdn_body, tt=tt),
        grid=(b, t // tt),
        in_specs=[col(0), col(1), col(2), col(3),
                  pl.BlockSpec((1, tt, LANES), lambda bi, ti: (bi, ti, AB_BA0 // LANES)),
                  pl.BlockSpec((3, GDN_CONV, width), lambda bi, ti: (0, 0, 0)),
                  pl.BlockSpec((1, LANES), lambda bi, ti: (0, 0)),
                  pl.BlockSpec((1, LANES), lambda bi, ti: (0, 0)),
                  pl.BlockSpec((1, GDN_D), lambda bi, ti: (0, 0))],
        out_specs=pl.BlockSpec((1, tt, width), lambda bi, ti: (bi, ti, 0)),
        out_shape=jax.ShapeDtypeStruct((b, t, width), f32),
        scratch_shapes=[pltpu.VMEM((3, tt + SUBLANES, width), f32),
                        pltpu.VMEM((3, tt, width), f32),
                        pltpu.VMEM((GDN_HEADS, GDN_D, GDN_D), f32)],
        compiler_params=_cparams(("parallel", "arbitrary")),
        name="gdn",
    )(f3, f3, f3, f3, f3, conv_w, arow, dtrow, norm_w)


def _rwkv_body(r_ref, k_ref, v_ref, l_ref, mu_ref, w0_ref, w2_ref, a0_ref, a2_ref, g2_ref, kkw_ref, ka_ref,
               rk_ref, lnw_ref, lnb_ref, o_ref, xp_ref, r_s, lw_s, k2_s, v_s, kk_s, a_s, g_s, s_ref, *, tt):
    c = RWKV_CHUNK
    n = RWKV_N
    w = RWKV_W

    @pl.when(pl.program_id(1) == 0)
    def _():
        xp_ref[pl.ds(0, SUBLANES), :] = jnp.zeros((SUBLANES, xp_ref.shape[1]), f32)
        s_ref[...] = jnp.zeros_like(s_ref)

    xp_ref[pl.ds(SUBLANES, tt), 0:w] = r_ref[0]
    xp_ref[pl.ds(SUBLANES, tt), w:2 * w] = k_ref[0]
    xp_ref[pl.ds(SUBLANES, tt), 2 * w:3 * w] = v_ref[0]
    xp_ref[pl.ds(SUBLANES, tt), 3 * w:] = l_ref[0]
    x = xp_ref[pl.ds(SUBLANES, tt), :]
    x = x + (xp_ref[pl.ds(SUBLANES - 1, tt), :] - x) * mu_ref[...]
    xp_ref[pl.ds(0, SUBLANES), :] = xp_ref[pl.ds(tt, SUBLANES), :]

    k = x[:, w:2 * w]
    lora0 = 3 * w
    wd = x[:, lora0:lora0 + RWKV_DECAY_LORA]
    ad = x[:, lora0 + RWKV_DECAY_LORA:lora0 + RWKV_DECAY_LORA + RWKV_LR_LORA]
    gd = x[:, lora0 + RWKV_DECAY_LORA + RWKV_LR_LORA:]
    w_log = -jax.nn.softplus(-(w0_ref[...] + _dot(jnp.tanh(wd), w2_ref[...], HI))) - 0.5
    lr = jax.nn.sigmoid(a0_ref[...] + _dot(ad, a2_ref[...], HI))
    r_s[...] = x[:, 0:w]
    lw_s[...] = -jnp.exp(w_log)
    k2_s[...] = k * (1.0 + (lr - 1.0) * ka_ref[...])
    v_s[...] = x[:, 2 * w:3 * w]
    kk_s[...] = k * kkw_ref[...]
    a_s[...] = lr
    g_s[...] = _dot(jax.nn.sigmoid(gd), g2_ref[...], HI)

    row = _iota2((c, c), 0)
    col = _iota2((c, c), 1)
    tril = row >= col
    strict = row > col
    cum_l = tril.astype(f32)
    last = _iota2((c, 1), 0) == c - 1

    heads = range(RWKV_HEADS)
    per_head = lambda x: [x[:, h * n:(h + 1) * n] for h in heads]

    group = RWKV_CHUNK_GROUP

    def chunk_group(gi, carry):
        rows = [pl.ds(pl.multiple_of((gi * group + j) * c, c), c) for j in range(group)]
        units = lambda f: [x for r in rows for x in f(r)]
        stack = lambda xs, ys: [jnp.concatenate([x, y], axis=0) for x, y in zip(xs, ys)]
        lw_all = [lw_s[r, :] for r in rows]
        p_all = [_dot01(cum_l, x) for x in lw_all]
        em_all = [jnp.exp(-p) for p in p_all]
        rh = units(lambda r: per_head(r_s[r, :]))
        k2 = units(lambda r: per_head(k2_s[r, :]))
        vh = units(lambda r: per_head(v_s[r, :]))
        r_t = [x for r, p in zip(rows, p_all) for x in per_head(r_s[r, :] * jnp.exp(p))]
        k_t = [x for r, e in zip(rows, em_all) for x in per_head(k2_s[r, :] * e)]
        e_prev = [x for p, lw in zip(p_all, lw_all) for x in per_head(jnp.exp(p - lw))]
        lr_em = [x for r, e in zip(rows, em_all) for x in per_head(a_s[r, :] * e)]
        kk = [x * lax.rsqrt(jnp.sum(x * x, axis=-1, keepdims=True) + 1e-6)
              for x in units(lambda r: per_head(kk_s[r, :]))]
        a_t = [-x * e for x, e in zip(kk, e_prev)]
        b_t = [x * e for x, e in zip(kk, lr_em)]
        ar = [x.astype(bf16) for x in stack(a_t, r_t)]
        bk = [x.astype(bf16) for x in stack(b_t, k_t)]
        ar_b = [_dot_nt(x, y[:c]) for x, y in zip(ar, bk)]
        ar_k = [_dot_nt(x, y[c:]) for x, y in zip(ar, bk)]
        m_ab = [jnp.where(strict, x[:c], 0.0) for x in ar_b]
        a_rb = [jnp.where(tril, x[c:], 0.0) for x in ar_b]
        mk_rk = [jnp.where(jnp.concatenate([strict, tril], axis=0), x, 0.0) for x in ar_k]
        eye = (row == col).astype(f32)
        t_inv = [eye + x for x in m_ab]
        pw = [_bdot(x, x) for x in m_ab]
        for _ in range(int(math.log2(c)) - 1):
            z = [_bdot(x, p) for x, p in zip(stack(pw, t_inv), pw)]
            pw = [x[:c] for x in z]
            t_inv = [t + x[c:] for t, x in zip(t_inv, z)]
        mv_yv = [_bdot(x, v) for x, v in zip(mk_rk, vh)]
        w1 = [_bdot(t, a) for t, a in zip(t_inv, a_t)]
        u2 = [_bdot(t, x[:c]) for t, x in zip(t_inv, mv_yv)]
        y_v = [x[c:] for x in mv_yv]
        w1r = stack(w1, r_t)
        decay_last = [x for p in p_all
                      for x in per_head(jnp.exp(jnp.sum(jnp.where(last, p, 0.0), axis=0, keepdims=True)))]
        g_h = units(lambda r: per_head(g_s[r, :]))
        bonus = [jnp.sum(r * k * rk_ref[:, (i % RWKV_HEADS) * n:(i % RWKV_HEADS + 1) * n], axis=-1, keepdims=True) * v
                 for i, (r, k, v) in enumerate(zip(rh, k2, vh))]
        state = [s_ref[h] for h in heads]
        for j in range(group):
            sl_u = slice(j * RWKV_HEADS, (j + 1) * RWKV_HEADS)
            ws = [_bdot_nt(x, s) for x, s in zip(w1r[sl_u], state)]
            u = [x[:c] + y for x, y in zip(ws, u2[sl_u])]
            y = [x[c:] + _bdot(a, z) + yv for x, a, z, yv in zip(ws, a_rb[sl_u], u, y_v[sl_u])]
            ds = [_dot_tn(x.astype(bf16), y) for x, y in zip(stack(u, vh[sl_u]), bk[sl_u])]
            state = [(s + d) * dl for s, d, dl in zip(state, ds, decay_last[sl_u])]
            for h in heads:
                sl = slice(h * n, (h + 1) * n)
                mean = jnp.mean(y[h], axis=-1, keepdims=True)
                yc = y[h] - mean
                var = jnp.mean(yc * yc, axis=-1, keepdims=True)
                yn = yc * lax.rsqrt(var + RWKV_LN_EPS) * lnw_ref[:, sl] + lnb_ref[:, sl]
                o_ref[0, rows[j], sl] = (yn + bonus[j * RWKV_HEADS + h]) * g_h[j * RWKV_HEADS + h]
        for h in heads:
            s_ref[h] = state[h]
        return carry

    lax.fori_loop(0, tt // (c * group), chunk_group, 0)


def rwkv_mixer(f3, mu, w0, w2, a0, a2, g2, k_k, k_a, r_k, ln_w, ln_b, tt=256):
    b, t, _ = f3.shape
    tt = min(tt, t)
    assert t % tt == 0 and tt % (RWKV_CHUNK * RWKV_CHUNK_GROUP) == 0
    w = RWKV_W
    lora = RWKV_DECAY_LORA + RWKV_LR_LORA + RWKV_GATE_LORA
    wide = 3 * w + lora
    row = lambda n: pl.BlockSpec((1, n), lambda bi, ti: (0, 0))
    full = lambda a: pl.BlockSpec(a.shape, lambda bi, ti: (0, 0))
    col = lambda j: pl.BlockSpec((1, tt, w), lambda bi, ti, j=j: (bi, ti, AB_RWKV0 // w + j))
    return pl.pallas_call(
        functools.partial(_rwkv_body, tt=tt),
        grid=(b, t // tt),
        in_specs=[col(0), col(1), col(2),
                  pl.BlockSpec((1, tt, lora), lambda bi, ti: (bi, ti, AB_LORA0 // lora)),
                  row(wide), row(w), full(w2), row(w), full(a2), full(g2)] + [row(w)] * 5,
        out_specs=pl.BlockSpec((1, tt, w), lambda bi, ti: (bi, ti, 0)),
        out_shape=jax.ShapeDtypeStruct((b, t, w), f32),
        scratch_shapes=[pltpu.VMEM((tt + SUBLANES, wide), f32)] + [pltpu.VMEM((tt, w), f32)] * 7
                       + [pltpu.VMEM((RWKV_HEADS, RWKV_N, RWKV_N), f32)],
        compiler_params=_cparams(("parallel", "arbitrary")),
        name="rwkv",
    )(f3, f3, f3, f3, mu.reshape(1, wide), w0.reshape(1, w), w2, a0.reshape(1, w), a2, g2,
      k_k.reshape(1, w), k_a.reshape(1, w), r_k.reshape(1, w), ln_w.reshape(1, w), ln_b.reshape(1, w))


def _rope128(x, cos, sin, sign_lo, sign_hi):
    r_hi = pltpu.roll(x, ROPE_HALF, 1)
    r_lo = pltpu.roll(x, LANES - ROPE_HALF, 1)
    return x * cos + (r_lo * sign_lo + r_hi * sign_hi) * sin


def _rope_tables(pos, freq_row):
    ang = pos * freq_row
    m = _iota2((1, LANES), 1) % NSA_HD
    sign_lo = jnp.where(m < ROPE_HALF, -1.0, 0.0).astype(f32)
    sign_hi = jnp.where((m >= ROPE_HALF) & (m < ROPE_DIM), 1.0, 0.0).astype(f32)
    return jnp.cos(ang), jnp.sin(ang), sign_lo, sign_hi


def _nsa_prep_body(q_ref, kc_i, vc_i, ks_i, vs_i, kw_i, vw_i, pos_ref, posr_ref, freq_ref, fcol_ref,
                   qo_ref, kc_ref, vc_ref, ks_ref, vs_ref, kw_ref, vw_ref):
    cos, sin, s_lo, s_hi = _rope_tables(pos_ref[0], freq_ref[...])
    scale = NSA_HD ** -0.5 * math.log2(math.e)
    n_blk = q_ref.shape[1] // NSA_QBLOCK
    eye = (_iota2((LANES, LANES), 0) == _iota2((LANES, LANES), 1)).astype(bf16)

    def transpose_bf16(x):
        return _dot_nt(eye, x.astype(bf16))

    ang_t = fcol_ref[...] * posr_ref[0]
    cos_t, sin_t = jnp.cos(ang_t), jnp.sin(ang_t)
    for c in range(NSA_HEADS // 2):
        xt = transpose_bf16(q_ref[0, :, c * LANES:(c + 1) * LANES] * scale)
        for r in range(2):
            head = 2 * c + r
            g, hl = head // NSA_HPG, head % NSA_HPG
            x1 = xt[r * NSA_HD:r * NSA_HD + ROPE_HALF]
            x2 = xt[r * NSA_HD + ROPE_HALF:r * NSA_HD + ROPE_DIM]
            roped = jnp.concatenate([x1 * cos_t - x2 * sin_t, x2 * cos_t + x1 * sin_t,
                                     xt[r * NSA_HD + ROPE_DIM:(r + 1) * NSA_HD]], axis=0).astype(bf16)
            for i in range(n_blk):
                qo_ref[0, g, i, :, hl * NSA_QBLOCK:(hl + 1) * NSA_QBLOCK] = (
                    roped[:, i * NSA_QBLOCK:(i + 1) * NSA_QBLOCK])

    def split(src, ref, rope, dtype):
        x = src[0]
        if rope:
            x = _rope128(x, cos, sin, s_lo, s_hi)
        ref[0, 0] = x[:, :NSA_HD].astype(dtype)
        ref[0, 1] = x[:, NSA_HD:].astype(dtype)

    def split_t(src, ref):
        xt = transpose_bf16(src[0])
        extra = (_iota2((NSA_VT_ROWS - NSA_HD, xt.shape[1]), 0) == 0).astype(f32)
        for g in range(NSA_GROUPS):
            ref[0, g] = jnp.concatenate([xt[g * NSA_HD:(g + 1) * NSA_HD], extra], axis=0).astype(bf16)

    split(kc_i, kc_ref, False, f32)
    split(vc_i, vc_ref, False, f32)
    tt = ks_i.shape[1]
    tok = pl.program_id(1) * tt + _iota2((tt, NSA_HD), 0)
    block_onehot = (tok // SEL_BLOCK == _iota2((tt, NSA_HD), 1)).astype(f32)
    ks = _rope128(ks_i[0], cos, sin, s_lo, s_hi)
    for g in range(NSA_GROUPS):
        ks_ref[0, g] = jnp.concatenate([ks[:, g * NSA_HD:(g + 1) * NSA_HD], block_onehot], axis=1).astype(bf16)
    split_t(vs_i, vs_ref)
    split(kw_i, kw_ref, True, bf16)
    split_t(vw_i, vw_ref)


def nsa_prep(f3, pos3, freq_row, freq_col, tt=512):
    b, t, _ = f3.shape
    assert t // SEL_BLOCK <= NSA_HD, "the selection-block one-hot shares the key tile's second 64 lanes"
    tt = min(tt, t)
    g = NSA_GROUPS
    n_blk = tt // NSA_QBLOCK
    kv_in = [pl.BlockSpec((1, tt, LANES), lambda bi, ti, c=NSA_KV0 // LANES + i: (bi, ti, c)) for i in range(6)]
    kv_spec = pl.BlockSpec((1, g, tt, NSA_HD), lambda bi, ti: (bi, 0, ti, 0))
    kvt_spec = pl.BlockSpec((1, g, NSA_VT_ROWS, tt), lambda bi, ti: (bi, 0, 0, ti))
    kv32 = jax.ShapeDtypeStruct((b, g, t, NSA_HD), f32)
    kv16 = jax.ShapeDtypeStruct((b, g, t, NSA_HD), bf16)
    kvt16 = jax.ShapeDtypeStruct((b, g, NSA_VT_ROWS, t), bf16)
    q_lanes = NSA_HPG * NSA_QBLOCK
    return pl.pallas_call(
        _nsa_prep_body,
        grid=(b, t // tt),
        in_specs=[pl.BlockSpec((1, tt, NSA_HEADS * NSA_HD), lambda bi, ti: (bi, ti, 0))] + kv_in
                 + [pl.BlockSpec((1, tt, 1), lambda bi, ti: (bi, ti, 0)),
                    pl.BlockSpec((1, 1, tt), lambda bi, ti: (bi, 0, ti)),
                    pl.BlockSpec((1, LANES), lambda bi, ti: (0, 0)),
                    pl.BlockSpec((ROPE_HALF, 1), lambda bi, ti: (0, 0))],
        out_specs=[pl.BlockSpec((1, g, n_blk, NSA_HD, q_lanes), lambda bi, ti: (bi, 0, ti, 0, 0)),
                   kv_spec, kv_spec, pl.BlockSpec((1, g, tt, 2 * NSA_HD), lambda bi, ti: (bi, 0, ti, 0)),
                   kvt_spec, kv_spec, kvt_spec],
        out_shape=[jax.ShapeDtypeStruct((b, g, t // NSA_QBLOCK, NSA_HD, q_lanes), bf16),
                   kv32, kv32, jax.ShapeDtypeStruct((b, g, t, 2 * NSA_HD), bf16), kvt16, kv16, kvt16],
        compiler_params=_cparams(("parallel", "parallel")),
        name="nsa_prep",
    )(f3, f3, f3, f3, f3, f3, f3, pos3, pos3.reshape(b, 1, t), freq_row, freq_col)


def _nsa_compress_body(kc_ref, vc_ref, pek_ref, w1k_ref, w2k_ref, pev_ref, w1v_ref, w2v_ref, pos_ref, freq_ref,
                       ko_ref, vo_ref):
    half = CMP_STRIDE * NSA_HD
    nrow = kc_ref.shape[2]
    last_row = _iota2((nrow, 1), 0) == nrow - 1

    def hidden(x, pe_ref, w1_ref):
        lo = _dot((x + pe_ref[:, :half]).astype(bf16), w1_ref[:half, :])
        hi = _dot((x + pe_ref[:, half:]).astype(bf16), w1_ref[half:, :])
        hi = jnp.where(last_row, 0.0, pltpu.roll(hi, nrow - 1, 0))
        hid = lo + hi
        return (hid * jax.nn.sigmoid(hid)).astype(bf16)

    ks = [_dot(hidden(kc_ref[0, g], pek_ref, w1k_ref), w2k_ref[...]) for g in range(NSA_GROUPS)]
    cos, sin, s_lo, s_hi = _rope_tables(pos_ref[0], freq_ref[...])
    kr = _rope128(jnp.concatenate(ks, axis=-1), cos, sin, s_lo, s_hi)
    for g in range(NSA_GROUPS):
        ko_ref[0, g] = kr[:, g * NSA_HD:(g + 1) * NSA_HD].astype(bf16)
        vo_ref[0, g] = _dot_nt(w2v_ref[...], hidden(vc_ref[0, g], pev_ref, w1v_ref)).astype(bf16)


def nsa_compress(kc4, vc4, pe_k, w1_k, w2_k, pe_v, w1_v, w2_vt, cpos3, freq_row):
    b, g, nrow, wide = kc4.shape
    full = lambda a: pl.BlockSpec(a.shape, lambda bi: (0,) * a.ndim)
    blk = pl.BlockSpec((1, g, nrow, wide), lambda bi: (bi, 0, 0, 0))
    out = pl.BlockSpec((1, g, nrow, NSA_HD), lambda bi: (bi, 0, 0, 0))
    out_t = pl.BlockSpec((1, g, NSA_HD, nrow), lambda bi: (bi, 0, 0, 0))
    return pl.pallas_call(
        _nsa_compress_body,
        grid=(b,),
        in_specs=[blk, blk, full(pe_k), full(w1_k), full(w2_k), full(pe_v), full(w1_v), full(w2_vt),
                  pl.BlockSpec((1, nrow, 1), lambda bi: (bi, 0, 0)), pl.BlockSpec((1, LANES), lambda bi: (0, 0))],
        out_specs=[out, out_t],
        out_shape=[jax.ShapeDtypeStruct((b, g, nrow, NSA_HD), bf16),
                   jax.ShapeDtypeStruct((b, g, NSA_HD, nrow), bf16)],
        compiler_params=_cparams(("parallel",)),
        name="nsa_compress",
    )(kc4, vc4, pe_k, w1_k, w2_k, pe_v, w1_v, w2_vt, cpos3, freq_row)


def _nsa_attn_body(qt_ref, kc_ref, vct_ref, ks_ref, vst_ref, kw_ref, vwt_ref, gate_ref, ovt_ref, dbias_ref,
                   wbias_ref, rep_ref, o_ref, score_ref, s_ref, acc_ref, sw_ref, *, key_tile):
    qb_n = NSA_QBLOCK
    blk_lanes = NSA_HPG * qb_n
    lanes = NSA_QPAIR * blk_lanes
    pair = pl.program_id(2)
    qt = jnp.concatenate([qt_ref[0, 0, a] for a in range(NSA_QPAIR)], axis=1)
    lane = _iota2((1, lanes), 1)
    t_lane = (pair * NSA_QPAIR + lane // blk_lanes) * qb_n + lane % qb_n

    def softmax_t(s):
        mx = jnp.max(s, axis=0, keepdims=True)
        mx = jnp.where(mx > -jnp.inf, mx, 0.0)
        e = jnp.exp2(s - mx)
        return e, jnp.sum(e, axis=0, keepdims=True), mx

    diag0 = pair * NSA_QPAIR
    n_tiles = (diag0 * SEL_BLOCK + key_tile - 1) // key_tile
    last_tile = ks_ref.shape[2] // key_tile - 1

    gt = jax.nn.sigmoid(gate_ref[0]).T
    gate = lambda br: jnp.concatenate([gt[3 * h + br:3 * h + br + 1, a * qb_n:(a + 1) * qb_n]
                                       for a in range(NSA_QPAIR) for h in range(NSA_HPG)], axis=1)
    gates = [gate(br) for br in range(3)]

    s_cmp = _dot(kc_ref[0, 0], qt)
    span = WINDOW + NSA_QPAIR * qb_n
    w0 = pl.multiple_of(pair * NSA_QPAIR * qb_n, LANES)
    sw_ref[...] = _dot(kw_ref[0, 0, pl.ds(w0, span), :], qt) + wbias_ref[...]
    k0 = pl.multiple_of(diag0 * SEL_BLOCK, NSA_QPAIR * SEL_BLOCK)
    s_diag = _dot(ks_ref[0, 0, pl.ds(k0, NSA_QPAIR * SEL_BLOCK), 0:NSA_HD], qt) + dbias_ref[...]

    n_cmp = kc_ref.shape[2]
    cmp_end = _iota2((n_cmp, 1), 0) * CMP_STRIDE + (CMP_LEN - 1)
    e_c, den_c, _ = softmax_t(jnp.where(cmp_end <= t_lane, s_cmp, -jnp.inf))
    p_c = e_c * (1.0 / jnp.maximum(den_c, 1e-30))
    o_c = _dot(vct_ref[0, 0], p_c.astype(bf16))

    n_sel = ovt_ref.shape[0]
    p_pair = []
    for a in range(NSA_QPAIR):
        acc = p_c[:, a * blk_lanes:a * blk_lanes + LANES]
        for c in range(1, blk_lanes // LANES):
            acc = acc + p_c[:, a * blk_lanes + c * LANES:a * blk_lanes + (c + 1) * LANES]
        p_pair.append(acc[:, :qb_n] + acc[:, qb_n:])
    imp = _dot(ovt_ref[...], jnp.concatenate(p_pair, axis=1), HI)
    blk = _iota2((n_sel, NSA_QPAIR * qb_n), 0)
    cur = pair * NSA_QPAIR + _iota2((n_sel, NSA_QPAIR * qb_n), 1) // qb_n
    valid = blk <= cur
    forced = (blk == 0) | (blk == cur) | (blk == cur - 1)
    score = jnp.where(valid, jnp.where(forced, jnp.inf, imp), -jnp.inf)
    score_ref[...] = score

    e_d, _, m_s = softmax_t(s_diag)
    acc_ref[...] = _dot(vst_ref[0, 0, :, pl.ds(k0, NSA_QPAIR * SEL_BLOCK)], e_d.astype(bf16))

    @pl.when(w0 < WINDOW)
    def _():
        is_pad = _iota2((WINDOW, 1), 0) < WINDOW - w0
        sw_ref[pl.ds(0, WINDOW), :] = jnp.where(is_pad, -jnp.inf, sw_ref[pl.ds(0, WINDOW), :])

    def rank_step(jp, cnt):
        other = score_ref[pl.ds(jp, 1), :]
        ahead = (other > score) | ((other == score) & (blk > jp))
        return cnt + ahead.astype(jnp.int32)

    def rank_step2(jj, cnt):
        return rank_step(2 * jj + 1, rank_step(2 * jj, cnt))

    n_cand = pair * NSA_QPAIR + NSA_QPAIR
    cnt = lax.fori_loop(0, jnp.where(n_cand > SEL_TOPN, n_cand // 2, 0), rank_step2,
                        jnp.zeros((n_sel, NSA_QPAIR * qb_n), jnp.int32))
    bias = jnp.where((cnt < SEL_TOPN) & (blk < diag0), 0.0, NSA_MASKED).astype(bf16)
    bias = _dot(bias, rep_ref[...]).astype(bf16)
    if n_sel < NSA_HD:
        bias = jnp.concatenate([bias, jnp.zeros((NSA_HD - n_sel, lanes), bf16)], axis=0)
    qx = jnp.concatenate([qt, bias], axis=0)

    def score_tile(kt):
        kk0 = pl.multiple_of(kt * key_tile, key_tile)
        return _dot(ks_ref[0, 0, pl.ds(kk0, key_tile), :], qx)

    s_ref[0] = score_tile(0)

    e_w, _, _ = softmax_t(sw_ref[...])
    o_w = _dot(vwt_ref[0, 0, :, pl.ds(w0, span)], e_w.astype(bf16))

    def half_step(kt, slot, carry):
        m_old = carry
        s_ref[1 - slot] = score_tile(jnp.minimum(kt + 1, last_tile))
        sm = s_ref[slot]
        mx = jnp.maximum(m_old, jnp.max(sm, axis=0, keepdims=True))
        alpha = jnp.exp2(m_old - mx)
        p = jnp.exp2(sm - mx)
        kk0 = pl.multiple_of(kt * key_tile, key_tile)
        acc_ref[...] = alpha * acc_ref[...] + _dot(vst_ref[0, 0, :, pl.ds(kk0, key_tile)], p.astype(bf16))
        return mx

    def pair_step(pi, carry):
        return half_step(2 * pi + 1, 1, half_step(2 * pi, 0, carry))

    lax.fori_loop(0, (n_tiles + 1) // 2, pair_step, m_s)

    l_s = acc_ref[pl.ds(NSA_HD, 1), :]
    l_w = o_w[NSA_HD:NSA_HD + 1]
    o_t = (gates[0] * o_c + (gates[1] * (1.0 / l_s)) * acc_ref[pl.ds(0, NSA_HD), :]
           + (gates[2] * (1.0 / l_w)) * o_w[:NSA_HD])
    o_b = o_t.astype(bf16)
    q_idx = _iota2((qb_n, LANES), 0)
    l_idx = _iota2((qb_n, LANES), 1)
    pick = [(l_idx == q_idx + r * qb_n).astype(bf16) for r in range(LANES // qb_n)]
    for a in range(NSA_QPAIR):
        for c in range(blk_lanes // LANES):
            piece = o_b[:, a * blk_lanes + c * LANES:a * blk_lanes + (c + 1) * LANES]
            for r in range(LANES // qb_n):
                h = c * (LANES // qb_n) + r
                o_ref[0, a * qb_n:(a + 1) * qb_n, h * NSA_HD:(h + 1) * NSA_HD] = (
                    _dot_nt(pick[r], piece).astype(bf16))


def nsa_attention(qt, kc, vct, ks, vst, kw, vwt, f3, overlap_t, key_tile=256):
    b, g, n_q, d, blk_lanes = qt.shape
    t = n_q * NSA_QBLOCK
    key_tile = min(key_tile, t)
    assert (t // key_tile) % 2 == 0, "an odd tile count borrows the (fully unselected) tile after the last one"
    assert NSA_QPAIR == 2, "the diagonal tile is masked by causality alone only for a pair of query blocks"
    n_cmp = kc.shape[2]
    n_sel = overlap_t.shape[0]
    lanes = NSA_QPAIR * blk_lanes
    q_rows = NSA_QPAIR * NSA_QBLOCK
    off = (jnp.arange(lanes) // blk_lanes) * NSA_QBLOCK + jnp.arange(lanes) % NSA_QBLOCK
    r_d = jnp.arange(NSA_QPAIR * SEL_BLOCK)[:, None]
    diag_bias = jnp.where(r_d <= off[None, :], 0.0, -jnp.inf).astype(f32)
    r_w = jnp.arange(WINDOW + q_rows)[:, None]
    win_bias = jnp.where((r_w > off[None, :]) & (r_w <= WINDOW + off[None, :]), 0.0, -jnp.inf).astype(f32)
    rep = (jnp.arange(q_rows)[:, None] == off[None, :]).astype(bf16)
    seq = pl.BlockSpec((1, 1, t, ks.shape[3]), lambda bi, gi, qi: (bi, gi, 0, 0))
    seq_t = pl.BlockSpec((1, 1, NSA_VT_ROWS, t), lambda bi, gi, qi: (bi, gi, 0, 0))
    const = lambda a: pl.BlockSpec(a.shape, lambda bi, gi, qi: (0, 0))
    return pl.pallas_call(
        functools.partial(_nsa_attn_body, key_tile=key_tile),
        grid=(b, g, n_q // NSA_QPAIR),
        in_specs=[pl.BlockSpec((1, 1, NSA_QPAIR, d, blk_lanes), lambda bi, gi, qi: (bi, gi, qi, 0, 0)),
                  pl.BlockSpec((1, 1, n_cmp, d), lambda bi, gi, qi: (bi, gi, 0, 0)),
                  pl.BlockSpec((1, 1, d, n_cmp), lambda bi, gi, qi: (bi, gi, 0, 0)),
                  seq, seq_t,
                  pl.BlockSpec((1, 1, t + WINDOW, d), lambda bi, gi, qi: (bi, gi, 0, 0)),
                  pl.BlockSpec((1, 1, NSA_VT_ROWS, t + WINDOW), lambda bi, gi, qi: (bi, gi, 0, 0)),
                  pl.BlockSpec((1, q_rows, LANES), lambda bi, gi, qi: (bi, qi, NSA_GATE0 // LANES + gi)),
                  const(overlap_t), const(diag_bias), const(win_bias), const(rep)],
        out_specs=pl.BlockSpec((1, q_rows, NSA_HPG * d), lambda bi, gi, qi: (bi, qi, gi)),
        out_shape=jax.ShapeDtypeStruct((b, t, g * NSA_HPG * d), bf16),
        scratch_shapes=[pltpu.VMEM((n_sel, q_rows), f32),
                        pltpu.VMEM((2, key_tile, lanes), f32), pltpu.VMEM((NSA_VT_ROWS, lanes), f32),
                        pltpu.VMEM((WINDOW + q_rows, lanes), f32)],
        compiler_params=_cparams(("parallel", "parallel", "arbitrary")),
        name="nsa_attn",
    )(qt, kc, vct, ks, vst, kw, vwt, f3, overlap_t, diag_bias, win_bias, rep)


def _place(cols, total, pieces):
    out = jnp.zeros((cols, total), f32)
    for start, mat in pieces:
        out = lax.dynamic_update_slice(out, mat.astype(f32), (0, start))
    return out


def _layer0(h, b, t, g_pre, w_in, w_out, conv, a_log, dt_bias, gnorm, mu, w0, w2, a0, a2, g2, k_k, k_a,
            r_k, ln_w, ln_b):
    gdn_w = 4 * GDN_HEADS * GDN_D
    w_pad = _place(D_MODEL, AB_COLS, [
        (0, w_in[:, :gdn_w]),
        (AB_BA0, w_in[:, gdn_w:gdn_w + 2 * GDN_HEADS]),
        (AB_RWKV0, w_in[:, gdn_w + 2 * GDN_HEADS:gdn_w + 2 * GDN_HEADS + 3 * RWKV_W]),
        (AB_LORA0, w_in[:, gdn_w + 2 * GDN_HEADS + 3 * RWKV_W:]),
    ]).astype(bf16)
    f3 = norm_matmul(h, g_pre, w_pad).reshape(b, t, AB_COLS)
    arow = jnp.zeros((1, LANES), f32).at[0, GDN_HEADS:2 * GDN_HEADS].set(a_log)
    dtrow = jnp.zeros((1, LANES), f32).at[0, GDN_HEADS:2 * GDN_HEADS].set(dt_bias)
    conv3 = conv.reshape(GDN_CONV, 3, GDN_HEADS * GDN_D).transpose(1, 0, 2)
    o_a = gdn_mixer(f3, conv3, arow, dtrow, gnorm.reshape(1, GDN_D))
    o_b = rwkv_mixer(f3, mu, w0, w2, a0, a2, g2, k_k, k_a, r_k, ln_w, ln_b)
    m = b * t
    n_a = GDN_HEADS * GDN_D
    return [o_a.reshape(m, n_a), o_b.reshape(m, RWKV_W)], [w_out[:n_a].astype(bf16), w_out[n_a:].astype(bf16)]


def _layer1(h, b, t, positions, g_pre, w_in, w_out, pe_k, w1_k, w2_k, pe_v, w1_v, w2_v):
    qw = NSA_HEADS * NSA_HD
    kvw = 6 * NSA_GROUPS * NSA_HD
    gates = w_in[:, qw + kvw:].reshape(D_MODEL, NSA_GROUPS, NSA_HPG * 3)
    w_pad = _place(D_MODEL, NSA_COLS, [(0, w_in[:, :qw + kvw])]
                   + [(NSA_GATE0 + gi * LANES, gates[:, gi]) for gi in range(NSA_GROUPS)]).astype(bf16)
    f3 = norm_matmul(h, g_pre, w_pad).reshape(b, t, NSA_COLS)
    inv_freq = ROPE_THETA ** (-jnp.arange(ROPE_HALF, dtype=f32) * (2.0 / ROPE_DIM))
    lane = jnp.arange(LANES)
    freq_row = jnp.where(lane % NSA_HD < ROPE_DIM, inv_freq[lane % ROPE_HALF], 0.0).reshape(1, LANES).astype(f32)
    posf = positions.astype(f32)
    qt, kc, vc, ks, vst, kw, vwt = nsa_prep(f3, posf.reshape(b, t, 1), freq_row, inv_freq.reshape(ROPE_HALF, 1))
    nrow = t // CMP_STRIDE
    cpos = jnp.concatenate([posf[:, CMP_LEN - 1::CMP_STRIDE], posf[:, -1:]], axis=1).reshape(b, nrow, 1)
    flat = lambda a: a.reshape(b, NSA_GROUPS, nrow, CMP_STRIDE * NSA_HD)
    kcc, vcct = nsa_compress(flat(kc), flat(vc), pe_k.reshape(1, -1), w1_k.astype(bf16), w2_k.astype(bf16),
                             pe_v.reshape(1, -1), w1_v.astype(bf16), w2_v.T.astype(bf16), cpos, freq_row)
    n_sel = t // SEL_BLOCK
    c_start = jnp.arange(nrow) * CMP_STRIDE
    s_start = jnp.arange(n_sel) * SEL_BLOCK
    overlap_t = jnp.clip(jnp.minimum(c_start[None, :] + CMP_LEN, s_start[:, None] + SEL_BLOCK)
                         - jnp.maximum(c_start[None, :], s_start[:, None]), 0, None).astype(f32) / CMP_LEN
    kw_pad = jnp.pad(kw, ((0, 0), (0, 0), (WINDOW, 0), (0, 0)))
    vwt_pad = jnp.pad(vwt, ((0, 0), (0, 0), (0, 0), (WINDOW, 0)))
    o = nsa_attention(qt, kcc, vcct, ks, vst, kw_pad, vwt_pad, f3, overlap_t)
    return [o.reshape(b * t, qw)], [w_out.astype(bf16)]


def kernel(x, positions, norm_mix_pre, norm_mix_post, norm_ffn_pre, norm_ffn_post, w_ffn_up, w_ffn_down, ab_w_in,
           ab_w_out, gdn_conv, gdn_a_log, gdn_dt_bias, gdn_norm, rwkv_mu, rwkv_w0, rwkv_w2, rwkv_a0, rwkv_a2,
           rwkv_g2, rwkv_k_k, rwkv_k_a, rwkv_r_k, rwkv_ln_w, rwkv_ln_b, nsa_w_in, nsa_w_out, nsa_pe_k, nsa_w1_k,
           nsa_w2_k, nsa_pe_v, nsa_w1_v, nsa_w2_v):
    b, t, d = x.shape
    h = x.reshape(b * t, d)
    mix = _layer0(h, b, t, norm_mix_pre[0], ab_w_in[0], ab_w_out[0], gdn_conv[0], gdn_a_log[0],
                  gdn_dt_bias[0], gdn_norm[0], rwkv_mu[0], rwkv_w0[0], rwkv_w2[0], rwkv_a0[0], rwkv_a2[0],
                  rwkv_g2[0], rwkv_k_k[0], rwkv_k_a[0], rwkv_r_k[0].reshape(-1), rwkv_ln_w[0], rwkv_ln_b[0])
    h = out_ffn(*mix, h, norm_mix_post[0], norm_ffn_pre[0], w_ffn_up[0].astype(bf16), w_ffn_down[0].astype(bf16),
                norm_ffn_post[0])
    mix = _layer1(h, b, t, positions, norm_mix_pre[1], nsa_w_in[0], nsa_w_out[0], nsa_pe_k[0],
                  nsa_w1_k[0], nsa_w2_k[0], nsa_pe_v[0], nsa_w1_v[0], nsa_w2_v[0])
    h = out_ffn(*mix, h, norm_mix_post[1], norm_ffn_pre[1], w_ffn_up[1].astype(bf16), w_ffn_down[1].astype(bf16),
                norm_ffn_post[1])
    return h.reshape(b, t, d)
```

```python
import functools
import math

import jax
import jax.numpy as jnp
from jax import lax
from jax.experimental import pallas as pl
from jax.experimental.pallas import tpu as pltpu

f32 = jnp.float32
bf16 = jnp.bfloat16
HI = lax.Precision.HIGHEST

V7X_VMEM_LIMIT_BYTES = 56 * 1024 * 1024
LANES = 128
SUBLANES = 8

D_MODEL = 1024
D_FF = 4 * D_MODEL
FFN_ROW_BLOCK = 256
NORM_EPS = 1e-6
GDN_HEADS = 4
GDN_D = 128
GDN_CONV = 4
GDN_CHUNK = 128
GDN_CHUNK_GROUP = 4
RWKV_HEADS = 8
RWKV_N = 64
RWKV_W = RWKV_HEADS * RWKV_N
RWKV_CHUNK = 64
RWKV_CHUNK_GROUP = 4
RWKV_LN_EPS = 64e-5
RWKV_DECAY_LORA = 64
RWKV_LR_LORA = 64
RWKV_GATE_LORA = 128
NSA_HEADS = 16
NSA_GROUPS = 2
NSA_HPG = NSA_HEADS // NSA_GROUPS
NSA_HD = 64
CMP_LEN = 32
CMP_STRIDE = 16
CMP_HIDDEN = 256
SEL_BLOCK = 64
SEL_TOPN = 16
WINDOW = 512
NSA_QBLOCK = 64
NSA_QPAIR = 2
NSA_VT_ROWS = 80
NSA_MASKED = -(2.0 ** 126)
ROPE_THETA = 500000.0
ROPE_DIM = NSA_HD // 4
ROPE_HALF = ROPE_DIM // 2

AB_COLS = 4096
AB_RWKV0 = 2048
AB_LORA0 = 3584
AB_BA0 = 3840
NSA_COLS = 2048
NSA_KV0 = 1024
NSA_GATE0 = 1792


def _cparams(sem):
    return pltpu.CompilerParams(dimension_semantics=sem, vmem_limit_bytes=V7X_VMEM_LIMIT_BYTES)


def _rms(x, g):
    return x * lax.rsqrt(jnp.mean(x * x, axis=-1, keepdims=True) + NORM_EPS) * g


def _dot(a, b, precision=None):
    return jnp.dot(a, b, precision=precision, preferred_element_type=f32)


def _dot_nt(a, b, precision=None):
    return lax.dot_general(a, b, (((1,), (1,)), ((), ())), precision=precision, preferred_element_type=f32)


def _dot_tn(a, b, precision=None):
    return lax.dot_general(a, b, (((0,), (0,)), ((), ())), precision=precision, preferred_element_type=f32)


def _bdot(a, b):
    return _dot(a.astype(bf16), b.astype(bf16))


def _bdot_nt(a, b):
    return _dot_nt(a.astype(bf16), b.astype(bf16))


def _bdot_tn(a, b):
    return _dot_tn(a.astype(bf16), b.astype(bf16))


def _dot01(m01, x):
    m = m01.astype(bf16)
    hi = x.astype(bf16)
    rest = x - hi.astype(f32)
    mid = rest.astype(bf16)
    lo = (rest - mid.astype(f32)).astype(bf16)
    return _dot(m, hi) + _dot(m, mid) + _dot(m, lo)


def _iota2(shape, axis):
    return lax.broadcasted_iota(jnp.int32, shape, axis)


def _norm_matmul_body(x_ref, g_ref, w_ref, o_ref, u_ref):
    @pl.when(pl.program_id(1) == 0)
    def _():
        u_ref[...] = _rms(x_ref[...], g_ref[...]).astype(bf16)

    o_ref[...] = _dot(u_ref[...], w_ref[...])


def norm_matmul(x, g, w, tm=2048, tn=1024):
    m, d = x.shape
    n = w.shape[1]
    tm = min(tm, m)
    return pl.pallas_call(
        _norm_matmul_body,
        grid=(m // tm, n // tn),
        in_specs=[pl.BlockSpec((tm, d), lambda i, j: (i, 0)),
                  pl.BlockSpec((1, d), lambda i, j: (0, 0)),
                  pl.BlockSpec((d, tn), lambda i, j: (0, j))],
        out_specs=pl.BlockSpec((tm, tn), lambda i, j: (i, j)),
        out_shape=jax.ShapeDtypeStruct((m, n), f32),
        scratch_shapes=[pltpu.VMEM((tm, d), bf16)],
        compiler_params=_cparams(("parallel", "arbitrary")),
        name="norm_matmul",
    )(x, g.reshape(1, d), w)


def _out_ffn_body(*refs, n_parts):
    a_refs = refs[:n_parts]
    w_refs = refs[n_parts:2 * n_parts]
    h_ref, gm_ref, g1_ref, wup_ref, wdn_ref, g2_ref, o_ref, h1_ref, u_ref, acc_ref = refs[2 * n_parts:]
    k = pl.program_id(1)
    last = pl.num_programs(1) - 1
    tm = h_ref.shape[0]
    row_blocks = [pl.ds(r0, min(tm, FFN_ROW_BLOCK)) for r0 in range(0, tm, min(tm, FFN_ROW_BLOCK))]

    def mlp_step(rows):
        a = _dot(u_ref[rows, :], wup_ref[...])
        a = jnp.square(jnp.maximum(a, 0.0))
        return _dot(a.astype(bf16), wdn_ref[...])

    def out_proj(rows):
        y = _dot(a_refs[0][rows, :].astype(bf16), w_refs[0][...])
        for a_ref, w_ref in zip(a_refs[1:], w_refs[1:]):
            y = y + _dot(a_ref[rows, :].astype(bf16), w_ref[...])
        return y

    @pl.when(k == 0)
    def _():
        y = out_proj(row_blocks[0])
        for idx, rows in enumerate(row_blocks):
            h1 = h_ref[rows, :] + _rms(y, gm_ref[...])
            h1_ref[rows, :] = h1
            u_ref[rows, :] = _rms(h1, g1_ref[...]).astype(bf16)
            if idx + 1 < len(row_blocks):
                y = out_proj(row_blocks[idx + 1])
            acc_ref[rows, :] = mlp_step(rows)

    @pl.when((k > 0) & (k < last))
    def _():
        for rows in row_blocks:
            acc_ref[rows, :] += mlp_step(rows)

    @pl.when(k == last)
    def _():
        for rows in row_blocks:
            o_ref[rows, :] = h1_ref[rows, :] + _rms(acc_ref[rows, :] + mlp_step(rows), g2_ref[...])


def out_ffn(parts, w_parts, h, g_mix, g1, wup, wdn, g2, tm=1024, tf=1024):
    m, d = h.shape
    ff = wup.shape[1]
    tm = min(tm, m)
    assert ff // tf >= 2, "the first and the last reduction step are distinct branches"
    n_parts = len(parts)
    row = pl.BlockSpec((1, d), lambda i, k: (0, 0))
    in_specs = ([pl.BlockSpec((tm, p.shape[1]), lambda i, k: (i, 0)) for p in parts]
                + [pl.BlockSpec(w.shape, lambda i, k: (0, 0)) for w in w_parts]
                + [pl.BlockSpec((tm, d), lambda i, k: (i, 0)), row, row,
                   pl.BlockSpec((d, tf), lambda i, k: (0, k)),
                   pl.BlockSpec((tf, d), lambda i, k: (k, 0)), row])
    return pl.pallas_call(
        functools.partial(_out_ffn_body, n_parts=n_parts),
        grid=(m // tm, ff // tf),
        in_specs=in_specs,
        out_specs=pl.BlockSpec((tm, d), lambda i, k: (i, 0)),
        out_shape=jax.ShapeDtypeStruct((m, d), f32),
        scratch_shapes=[pltpu.VMEM((tm, d), f32), pltpu.VMEM((tm, d), bf16), pltpu.VMEM((tm, d), f32)],
        compiler_params=_cparams(("parallel", "arbitrary")),
        name="out_ffn",
    )(*parts, *w_parts, h, g_mix.reshape(1, d), g1.reshape(1, d), wup, wdn, g2.reshape(1, d))


def _neumann_inverses(n_mats, size):
    eye = (_iota2((size, size), 0) == _iota2((size, size), 1)).astype(f32)
    ts = [eye + n for n in n_mats]
    ps = list(n_mats)
    for _ in range(int(math.log2(size)) - 1):
        ps = [_bdot(p, p) for p in ps]
        ts = [t + _bdot(t, p) for t, p in zip(ts, ps)]
    return ts


def _gdn_body(q_ref, k_ref, v_ref, z_ref, ba_ref, cw_ref, arow_ref, dtrow_ref, nw_ref, o_ref,
              xp_ref, qkv_ref, s_ref, *, tt):
    c = GDN_CHUNK
    d = GDN_D
    heads = range(GDN_HEADS)
    width = GDN_HEADS * d

    @pl.when(pl.program_id(1) == 0)
    def _():
        xp_ref[:, pl.ds(0, SUBLANES), :] = jnp.zeros((3, SUBLANES, width), f32)
        s_ref[...] = jnp.zeros_like(s_ref)

    conv_rows = min(tt, 64)
    for idx, ref in enumerate((q_ref, k_ref, v_ref)):
        xp_ref[idx, pl.ds(SUBLANES, tt), :] = ref[0]
        for h in heads:
            cols = slice(h * d, (h + 1) * d)
            w = cw_ref[idx, :, cols]
            for r0 in range(0, tt, conv_rows):
                y = xp_ref[idx, pl.ds(SUBLANES + r0, conv_rows), cols] * w[GDN_CONV - 1:GDN_CONV, :]
                for j in range(GDN_CONV - 1):
                    y = y + xp_ref[idx, pl.ds(SUBLANES - (GDN_CONV - 1) + j + r0, conv_rows), cols] * w[j:j + 1, :]
                y = y * jax.nn.sigmoid(y)
                if idx < 2:
                    y = y * lax.rsqrt(jnp.sum(y * y, axis=-1, keepdims=True) + 1e-6)
                if idx == 0:
                    y = y * (d ** -0.5)
                qkv_ref[idx, pl.ds(r0, conv_rows), cols] = y
        xp_ref[idx, pl.ds(0, SUBLANES), :] = xp_ref[idx, pl.ds(tt, SUBLANES), :]

    row = _iota2((c, c), 0)
    col = _iota2((c, c), 1)
    tril = row >= col
    strict = row > col
    eye = row == col
    cum_l = tril.astype(f32)
    last_row = _iota2((c, 1), 0) == c - 1

    group = GDN_CHUNK_GROUP
    nh = GDN_HEADS

    def chunk_group(gi, carry):
        rows = [pl.ds(pl.multiple_of((gi * group + j) * c, c), c) for j in range(group)]
        sig, gcum = [], []
        for r in rows:
            ba = ba_ref[0, r, :]
            sig.append(jax.nn.sigmoid(ba))
            gcum.append(_dot01(cum_l, -jnp.exp(arow_ref[...]) * jax.nn.softplus(ba + dtrow_ref[...])))
        units = lambda f: [f(j, h) for j in range(group) for h in heads]
        qn = units(lambda j, h: qkv_ref[0, rows[j], h * d:(h + 1) * d])
        kn = units(lambda j, h: qkv_ref[1, rows[j], h * d:(h + 1) * d])
        vv = units(lambda j, h: qkv_ref[2, rows[j], h * d:(h + 1) * d])
        beta = units(lambda j, h: sig[j][:, h:h + 1])
        gc = units(lambda j, h: gcum[j][:, nh + h:nh + h + 1])
        gc_row = [jnp.sum(jnp.where(eye, jnp.broadcast_to(g, (c, c)), 0.0), axis=0, keepdims=True) for g in gc]
        gc_last = [jnp.sum(jnp.where(last_row, g, 0.0), axis=0, keepdims=True) for g in gc]
        decay = [jnp.exp(jnp.where(tril, g - gr, -jnp.inf)) for g, gr in zip(gc, gc_row)]
        knb = [k.astype(bf16) for k in kn]
        kk = [_dot_nt(k, k) for k in knb]
        qk = [_dot_nt(q.astype(bf16), k) for q, k in zip(qn, knb)]
        t_inv = _neumann_inverses([-jnp.where(strict, b * x * dc, 0.0) for b, x, dc in zip(beta, kk, decay)], c)
        egc = [jnp.exp(g) for g in gc]
        u = [_bdot(t, v * b) for t, v, b in zip(t_inv, vv, beta)]
        w = [_bdot(t, k * (b * e)).astype(bf16) for t, k, b, e in zip(t_inv, kn, beta, egc)]
        intra = [jnp.where(tril, x * dc, 0.0).astype(bf16) for x, dc in zip(qk, decay)]
        q_g = [(q * e).astype(bf16) for q, e in zip(qn, egc)]
        k_g = [(k * jnp.exp(gl - g)).astype(bf16) for k, gl, g in zip(kn, gc_last, gc)]
        state = [s_ref[h] for h in heads]
        for j in range(group):
            sl = slice(j * nh, (j + 1) * nh)
            sb = [x.astype(bf16) for x in state]
            v_new = [(x - _dot(y, z)).astype(bf16) for x, y, z in zip(u[sl], w[sl], sb)]
            o = [_dot(q, z) + _dot(a, vn) for q, z, a, vn in zip(q_g[sl], sb, intra[sl], v_new)]
            state = [s * jnp.exp(gl) + _dot_tn(k, vn) for s, gl, k, vn in zip(state, gc_last[sl], k_g[sl], v_new)]
            for h in heads:
                z = z_ref[0, rows[j], h * d:(h + 1) * d]
                o_ref[0, rows[j], h * d:(h + 1) * d] = _rms(o[h], nw_ref[...]) * (z * jax.nn.sigmoid(z))
        for h in heads:
            s_ref[h] = state[h]
        return carry

    lax.fori_loop(0, tt // (c * group), chunk_group, 0)


def gdn_mixer(f3, conv_w, arow, dtrow, norm_w, tt=512):
    b, t, _ = f3.shape
    tt = min(tt, t)
    assert t % tt == 0 and tt % (GDN_CHUNK * GDN_CHUNK_GROUP) == 0
    width = GDN_HEADS * GDN_D
    col = lambda j: pl.BlockSpec((1, tt, width), lambda bi, ti, j=j: (bi, ti, j))
    return pl.pallas_call(
        functools.partial(_gdn_body, tt=tt),
        grid=(b, t // tt),
        in_specs=[col(0), col(1), col(2), col(3),
                  pl.BlockSpec((1, tt, LANES), lambda bi, ti: (bi, ti, AB_BA0 // LANES)),
                  pl.BlockSpec((3, GDN_CONV, width), lambda bi, ti: (0, 0, 0)),
                  pl.BlockSpec((1, LANES), lambda bi, ti: (0, 0)),
                  pl.BlockSpec((1, LANES), lambda bi, ti: (0, 0)),
                  pl.BlockSpec((1, GDN_D), lambda bi, ti: (0, 0))],
        out_specs=pl.BlockSpec((1, tt, width), lambda bi, ti: (bi, ti, 0)),
        out_shape=jax.ShapeDtypeStruct((b, t, width), f32),
        scratch_shapes=[pltpu.VMEM((3, tt + SUBLANES, width), f32),
                        pltpu.VMEM((3, tt, width), f32),
                        pltpu.VMEM((GDN_HEADS, GDN_D, GDN_D), f32)],
        compiler_params=_cparams(("parallel", "arbitrary")),
        name="gdn",
    )(f3, f3, f3, f3, f3, conv_w, arow, dtrow, norm_w)


def _rwkv_body(r_ref, k_ref, v_ref, l_ref, mu_ref, w0_ref, w2_ref, a0_ref, a2_ref, g2_ref, kkw_ref, ka_ref,
               rk_ref, lnw_ref, lnb_ref, o_ref, xp_ref, r_s, lw_s, k2_s, v_s, kk_s, a_s, g_s, s_ref, *, tt):
    c = RWKV_CHUNK
    n = RWKV_N
    w = RWKV_W

    @pl.when(pl.program_id(1) == 0)
    def _():
        xp_ref[pl.ds(0, SUBLANES), :] = jnp.zeros((SUBLANES, xp_ref.shape[1]), f32)
        s_ref[...] = jnp.zeros_like(s_ref)

    xp_ref[pl.ds(SUBLANES, tt), 0:w] = r_ref[0]
    xp_ref[pl.ds(SUBLANES, tt), w:2 * w] = k_ref[0]
    xp_ref[pl.ds(SUBLANES, tt), 2 * w:3 * w] = v_ref[0]
    xp_ref[pl.ds(SUBLANES, tt), 3 * w:] = l_ref[0]
    x = xp_ref[pl.ds(SUBLANES, tt), :]
    x = x + (xp_ref[pl.ds(SUBLANES - 1, tt), :] - x) * mu_ref[...]
    xp_ref[pl.ds(0, SUBLANES), :] = xp_ref[pl.ds(tt, SUBLANES), :]

    k = x[:, w:2 * w]
    lora0 = 3 * w
    wd = x[:, lora0:lora0 + RWKV_DECAY_LORA]
    ad = x[:, lora0 + RWKV_DECAY_LORA:lora0 + RWKV_DECAY_LORA + RWKV_LR_LORA]
    gd = x[:, lora0 + RWKV_DECAY_LORA + RWKV_LR_LORA:]
    w_log = -jax.nn.softplus(-(w0_ref[...] + _dot(jnp.tanh(wd), w2_ref[...], HI))) - 0.5
    lr = jax.nn.sigmoid(a0_ref[...] + _dot(ad, a2_ref[...], HI))
    r_s[...] = x[:, 0:w]
    lw_s[...] = -jnp.exp(w_log)
    k2_s[...] = k * (1.0 + (lr - 1.0) * ka_ref[...])
    v_s[...] = x[:, 2 * w:3 * w]
    kk_s[...] = k * kkw_ref[...]
    a_s[...] = lr
    g_s[...] = _dot(jax.nn.sigmoid(gd), g2_ref[...], HI)

    row = _iota2((c, c), 0)
    col = _iota2((c, c), 1)
    tril = row >= col
    strict = row > col
    cum_l = tril.astype(f32)
    last = _iota2((c, 1), 0) == c - 1

    heads = range(RWKV_HEADS)
    per_head = lambda x: [x[:, h * n:(h + 1) * n] for h in heads]

    group = RWKV_CHUNK_GROUP

    def chunk_group(gi, carry):
        rows = [pl.ds(pl.multiple_of((gi * group + j) * c, c), c) for j in range(group)]
        units = lambda f: [x for r in rows for x in f(r)]
        stack = lambda xs, ys: [jnp.concatenate([x, y], axis=0) for x, y in zip(xs, ys)]
        lw_all = [lw_s[r, :] for r in rows]
        p_all = [_dot01(cum_l, x) for x in lw_all]
        em_all = [jnp.exp(-p) for p in p_all]
        rh = units(lambda r: per_head(r_s[r, :]))
        k2 = units(lambda r: per_head(k2_s[r, :]))
        vh = units(lambda r: per_head(v_s[r, :]))
        r_t = [x for r, p in zip(rows, p_all) for x in per_head(r_s[r, :] * jnp.exp(p))]
        k_t = [x for r, e in zip(rows, em_all) for x in per_head(k2_s[r, :] * e)]
        e_prev = [x for p, lw in zip(p_all, lw_all) for x in per_head(jnp.exp(p - lw))]
        lr_em = [x for r, e in zip(rows, em_all) for x in per_head(a_s[r, :] * e)]
        kk = [x * lax.rsqrt(jnp.sum(x * x, axis=-1, keepdims=True) + 1e-6)
              for x in units(lambda r: per_head(kk_s[r, :]))]
        a_t = [-x * e for x, e in zip(kk, e_prev)]
        b_t = [x * e for x, e in zip(kk, lr_em)]
        ar = [x.astype(bf16) for x in stack(a_t, r_t)]
        bk = [x.astype(bf16) for x in stack(b_t, k_t)]
        ar_b = [_dot_nt(x, y[:c]) for x, y in zip(ar, bk)]
        ar_k = [_dot_nt(x, y[c:]) for x, y in zip(ar, bk)]
        m_ab = [jnp.where(strict, x[:c], 0.0) for x in ar_b]
        a_rb = [jnp.where(tril, x[c:], 0.0) for x in ar_b]
        mk_rk = [jnp.where(jnp.concatenate([strict, tril], axis=0), x, 0.0) for x in ar_k]
        eye = (row == col).astype(f32)
        t_inv = [eye + x for x in m_ab]
        pw = [_bdot(x, x) for x in m_ab]
        for _ in range(int(math.log2(c)) - 1):
            z = [_bdot(x, p) for x, p in zip(stack(pw, t_inv), pw)]
            pw = [x[:c] for x in z]
            t_inv = [t + x[c:] for t, x in zip(t_inv, z)]
        mv_yv = [_bdot(x, v) for x, v in zip(mk_rk, vh)]
        w1 = [_bdot(t, a) for t, a in zip(t_inv, a_t)]
        u2 = [_bdot(t, x[:c]) for t, x in zip(t_inv, mv_yv)]
        y_v = [x[c:] for x in mv_yv]
        w1r = stack(w1, r_t)
        decay_last = [x for p in p_all
                      for x in per_head(jnp.exp(jnp.sum(jnp.where(last, p, 0.0), axis=0, keepdims=True)))]
        g_h = units(lambda r: per_head(g_s[r, :]))
        bonus = [jnp.sum(r * k * rk_ref[:, (i % RWKV_HEADS) * n:(i % RWKV_HEADS + 1) * n], axis=-1, keepdims=True) * v
                 for i, (r, k, v) in enumerate(zip(rh, k2, vh))]
        state = [s_ref[h] for h in heads]
        for j in range(group):
            sl_u = slice(j * RWKV_HEADS, (j + 1) * RWKV_HEADS)
            ws = [_bdot_nt(x, s) for x, s in zip(w1r[sl_u], state)]
            u = [x[:c] + y for x, y in zip(ws, u2[sl_u])]
            y = [x[c:] + _bdot(a, z) + yv for x, a, z, yv in zip(ws, a_rb[sl_u], u, y_v[sl_u])]
            ds = [_dot_tn(x.astype(bf16), y) for x, y in zip(stack(u, vh[sl_u]), bk[sl_u])]
            state = [(s + d) * dl for s, d, dl in zip(state, ds, decay_last[sl_u])]
            for h in heads:
                sl = slice(h * n, (h + 1) * n)
                mean = jnp.mean(y[h], axis=-1, keepdims=True)
                yc = y[h] - mean
                var = jnp.mean(yc * yc, axis=-1, keepdims=True)
                yn = yc * lax.rsqrt(var + RWKV_LN_EPS) * lnw_ref[:, sl] + lnb_ref[:, sl]
                o_ref[0, rows[j], sl] = (yn + bonus[j * RWKV_HEADS + h]) * g_h[j * RWKV_HEADS + h]
        for h in heads:
            s_ref[h] = state[h]
        return carry

    lax.fori_loop(0, tt // (c * group), chunk_group, 0)


def rwkv_mixer(f3, mu, w0, w2, a0, a2, g2, k_k, k_a, r_k, ln_w, ln_b, tt=256):
    b, t, _ = f3.shape
    tt = min(tt, t)
    assert t % tt == 0 and tt % (RWKV_CHUNK * RWKV_CHUNK_GROUP) == 0
    w = RWKV_W
    lora = RWKV_DECAY_LORA + RWKV_LR_LORA + RWKV_GATE_LORA
    wide = 3 * w + lora
    row = lambda n: pl.BlockSpec((1, n), lambda bi, ti: (0, 0))
    full = lambda a: pl.BlockSpec(a.shape, lambda bi, ti: (0, 0))
    col = lambda j: pl.BlockSpec((1, tt, w), lambda bi, ti, j=j: (bi, ti, AB_RWKV0 // w + j))
    return pl.pallas_call(
        functools.partial(_rwkv_body, tt=tt),
        grid=(b, t // tt),
        in_specs=[col(0), col(1), col(2),
                  pl.BlockSpec((1, tt, lora), lambda bi, ti: (bi, ti, AB_LORA0 // lora)),
                  row(wide), row(w), full(w2), row(w), full(a2), full(g2)] + [row(w)] * 5,
        out_specs=pl.BlockSpec((1, tt, w), lambda bi, ti: (bi, ti, 0)),
        out_shape=jax.ShapeDtypeStruct((b, t, w), f32),
        scratch_shapes=[pltpu.VMEM((tt + SUBLANES, wide), f32)] + [pltpu.VMEM((tt, w), f32)] * 7
                       + [pltpu.VMEM((RWKV_HEADS, RWKV_N, RWKV_N), f32)],
        compiler_params=_cparams(("parallel", "arbitrary")),
        name="rwkv",
    )(f3, f3, f3, f3, mu.reshape(1, wide), w0.reshape(1, w), w2, a0.reshape(1, w), a2, g2,
      k_k.reshape(1, w), k_a.reshape(1, w), r_k.reshape(1, w), ln_w.reshape(1, w), ln_b.reshape(1, w))


def _rope128(x, cos, sin, sign_lo, sign_hi):
    r_hi = pltpu.roll(x, ROPE_HALF, 1)
    r_lo = pltpu.roll(x, LANES - ROPE_HALF, 1)
    return x * cos + (r_lo * sign_lo + r_hi * sign_hi) * sin


def _rope_tables(pos, freq_row):
    ang = pos * freq_row
    m = _iota2((1, LANES), 1) % NSA_HD
    sign_lo = jnp.where(m < ROPE_HALF, -1.0, 0.0).astype(f32)
    sign_hi = jnp.where((m >= ROPE_HALF) & (m < ROPE_DIM), 1.0, 0.0).astype(f32)
    return jnp.cos(ang), jnp.sin(ang), sign_lo, sign_hi


def _nsa_prep_body(q_ref, kc_i, vc_i, ks_i, vs_i, kw_i, vw_i, pos_ref, posr_ref, freq_ref, fcol_ref,
                   qo_ref, kc_ref, vc_ref, ks_ref, vs_ref, kw_ref, vw_ref):
    cos, sin, s_lo, s_hi = _rope_tables(pos_ref[0], freq_ref[...])
    scale = NSA_HD ** -0.5 * math.log2(math.e)
    n_blk = q_ref.shape[1] // NSA_QBLOCK
    eye = (_iota2((LANES, LANES), 0) == _iota2((LANES, LANES), 1)).astype(bf16)

    def transpose_bf16(x):
        return _dot_nt(eye, x.astype(bf16))

    ang_t = fcol_ref[...] * posr_ref[0]
    cos_t, sin_t = jnp.cos(ang_t), jnp.sin(ang_t)
    for c in range(NSA_HEADS // 2):
        xt = transpose_bf16(q_ref[0, :, c * LANES:(c + 1) * LANES] * scale)
        for r in range(2):
            head = 2 * c + r
            g, hl = head // NSA_HPG, head % NSA_HPG
            x1 = xt[r * NSA_HD:r * NSA_HD + ROPE_HALF]
            x2 = xt[r * NSA_HD + ROPE_HALF:r * NSA_HD + ROPE_DIM]
            roped = jnp.concatenate([x1 * cos_t - x2 * sin_t, x2 * cos_t + x1 * sin_t,
                                     xt[r * NSA_HD + ROPE_DIM:(r + 1) * NSA_HD]], axis=0).astype(bf16)
            for i in range(n_blk):
                qo_ref[0, g, i, :, hl * NSA_QBLOCK:(hl + 1) * NSA_QBLOCK] = (
                    roped[:, i * NSA_QBLOCK:(i + 1) * NSA_QBLOCK])

    def split(src, ref, rope, dtype):
        x = src[0]
        if rope:
            x = _rope128(x, cos, sin, s_lo, s_hi)
        ref[0, 0] = x[:, :NSA_HD].astype(dtype)
        ref[0, 1] = x[:, NSA_HD:].astype(dtype)

    def split_t(src, ref):
        xt = transpose_bf16(src[0])
        extra = (_iota2((NSA_VT_ROWS - NSA_HD, xt.shape[1]), 0) == 0).astype(f32)
        for g in range(NSA_GROUPS):
            ref[0, g] = jnp.concatenate([xt[g * NSA_HD:(g + 1) * NSA_HD], extra], axis=0).astype(bf16)

    split(kc_i, kc_ref, False, f32)
    split(vc_i, vc_ref, False, f32)
    tt = ks_i.shape[1]
    tok = pl.program_id(1) * tt + _iota2((tt, NSA_HD), 0)
    block_onehot = (tok // SEL_BLOCK == _iota2((tt, NSA_HD), 1)).astype(f32)
    ks = _rope128(ks_i[0], cos, sin, s_lo, s_hi)
    for g in range(NSA_GROUPS):
        ks_ref[0, g] = jnp.concatenate([ks[:, g * NSA_HD:(g + 1) * NSA_HD], block_onehot], axis=1).astype(bf16)
    split_t(vs_i, vs_ref)
    split(kw_i, kw_ref, True, bf16)
    split_t(vw_i, vw_ref)


def nsa_prep(f3, pos3, freq_row, freq_col, tt=512):
    b, t, _ = f3.shape
    assert t // SEL_BLOCK <= NSA_HD, "the selection-block one-hot shares the key tile's second 64 lanes"
    tt = min(tt, t)
    g = NSA_GROUPS
    n_blk = tt // NSA_QBLOCK
    kv_in = [pl.BlockSpec((1, tt, LANES), lambda bi, ti, c=NSA_KV0 // LANES + i: (bi, ti, c)) for i in range(6)]
    kv_spec = pl.BlockSpec((1, g, tt, NSA_HD), lambda bi, ti: (bi, 0, ti, 0))
    kvt_spec = pl.BlockSpec((1, g, NSA_VT_ROWS, tt), lambda bi, ti: (bi, 0, 0, ti))
    kv32 = jax.ShapeDtypeStruct((b, g, t, NSA_HD), f32)
    kv16 = jax.ShapeDtypeStruct((b, g, t, NSA_HD), bf16)
    kvt16 = jax.ShapeDtypeStruct((b, g, NSA_VT_ROWS, t), bf16)
    q_lanes = NSA_HPG * NSA_QBLOCK
    return pl.pallas_call(
        _nsa_prep_body,
        grid=(b, t // tt),
        in_specs=[pl.BlockSpec((1, tt, NSA_HEADS * NSA_HD), lambda bi, ti: (bi, ti, 0))] + kv_in
                 + [pl.BlockSpec((1, tt, 1), lambda bi, ti: (bi, ti, 0)),
                    pl.BlockSpec((1, 1, tt), lambda bi, ti: (bi, 0, ti)),
                    pl.BlockSpec((1, LANES), lambda bi, ti: (0, 0)),
                    pl.BlockSpec((ROPE_HALF, 1), lambda bi, ti: (0, 0))],
        out_specs=[pl.BlockSpec((1, g, n_blk, NSA_HD, q_lanes), lambda bi, ti: (bi, 0, ti, 0, 0)),
                   kv_spec, kv_spec, pl.BlockSpec((1, g, tt, 2 * NSA_HD), lambda bi, ti: (bi, 0, ti, 0)),
                   kvt_spec, kv_spec, kvt_spec],
        out_shape=[jax.ShapeDtypeStruct((b, g, t // NSA_QBLOCK, NSA_HD, q_lanes), bf16),
                   kv32, kv32, jax.ShapeDtypeStruct((b, g, t, 2 * NSA_HD), bf16), kvt16, kv16, kvt16],
        compiler_params=_cparams(("parallel", "parallel")),
        name="nsa_prep",
    )(f3, f3, f3, f3, f3, f3, f3, pos3, pos3.reshape(b, 1, t), freq_row, freq_col)


def _nsa_compress_body(kc_ref, vc_ref, pek_ref, w1k_ref, w2k_ref, pev_ref, w1v_ref, w2v_ref, pos_ref, freq_ref,
                       ko_ref, vo_ref):
    half = CMP_STRIDE * NSA_HD
    nrow = kc_ref.shape[2]
    last_row = _iota2((nrow, 1), 0) == nrow - 1

    def hidden(x, pe_ref, w1_ref):
        lo = _dot((x + pe_ref[:, :half]).astype(bf16), w1_ref[:half, :])
        hi = _dot((x + pe_ref[:, half:]).astype(bf16), w1_ref[half:, :])
        hi = jnp.where(last_row, 0.0, pltpu.roll(hi, nrow - 1, 0))
        hid = lo + hi
        return (hid * jax.nn.sigmoid(hid)).astype(bf16)

    ks = [_dot(hidden(kc_ref[0, g], pek_ref, w1k_ref), w2k_ref[...]) for g in range(NSA_GROUPS)]
    cos, sin, s_lo, s_hi = _rope_tables(pos_ref[0], freq_ref[...])
    kr = _rope128(jnp.concatenate(ks, axis=-1), cos, sin, s_lo, s_hi)
    for g in range(NSA_GROUPS):
        ko_ref[0, g] = kr[:, g * NSA_HD:(g + 1) * NSA_HD].astype(bf16)
        vo_ref[0, g] = _dot_nt(w2v_ref[...], hidden(vc_ref[0, g], pev_ref, w1v_ref)).astype(bf16)


def nsa_compress(kc4, vc4, pe_k, w1_k, w2_k, pe_v, w1_v, w2_vt, cpos3, freq_row):
    b, g, nrow, wide = kc4.shape
    full = lambda a: pl.BlockSpec(a.shape, lambda bi: (0,) * a.ndim)
    blk = pl.BlockSpec((1, g, nrow, wide), lambda bi: (bi, 0, 0, 0))
    out = pl.BlockSpec((1, g, nrow, NSA_HD), lambda bi: (bi, 0, 0, 0))
    out_t = pl.BlockSpec((1, g, NSA_HD, nrow), lambda bi: (bi, 0, 0, 0))
    return pl.pallas_call(
        _nsa_compress_body,
        grid=(b,),
        in_specs=[blk, blk, full(pe_k), full(w1_k), full(w2_k), full(pe_v), full(w1_v), full(w2_vt),
                  pl.BlockSpec((1, nrow, 1), lambda bi: (bi, 0, 0)), pl.BlockSpec((1, LANES), lambda bi: (0, 0))],
        out_specs=[out, out_t],
        out_shape=[jax.ShapeDtypeStruct((b, g, nrow, NSA_HD), bf16),
                   jax.ShapeDtypeStruct((b, g, NSA_HD, nrow), bf16)],
        compiler_params=_cparams(("parallel",)),
        name="nsa_compress",
    )(kc4, vc4, pe_k, w1_k, w2_k, pe_v, w1_v, w2_vt, cpos3, freq_row)


def _nsa_attn_body(qt_ref, kc_ref, vct_ref, ks_ref, vst_ref, kw_ref, vwt_ref, gate_ref, ovt_ref, dbias_ref,
                   wbias_ref, rep_ref, o_ref, score_ref, s_ref, acc_ref, sw_ref, *, key_tile):
    qb_n = NSA_QBLOCK
    blk_lanes = NSA_HPG * qb_n
    lanes = NSA_QPAIR * blk_lanes
    pair = pl.program_id(2)
    qt = jnp.concatenate([qt_ref[0, 0, a] for a in range(NSA_QPAIR)], axis=1)
    lane = _iota2((1, lanes), 1)
    t_lane = (pair * NSA_QPAIR + lane // blk_lanes) * qb_n + lane % qb_n

    def softmax_t(s):
        mx = jnp.max(s, axis=0, keepdims=True)
        mx = jnp.where(mx > -jnp.inf, mx, 0.0)
        e = jnp.exp2(s - mx)
        return e, jnp.sum(e, axis=0, keepdims=True), mx

    diag0 = pair * NSA_QPAIR
    n_tiles = (diag0 * SEL_BLOCK + key_tile - 1) // key_tile
    last_tile = ks_ref.shape[2] // key_tile - 1

    gt = jax.nn.sigmoid(gate_ref[0]).T
    gate = lambda br: jnp.concatenate([gt[3 * h + br:3 * h + br + 1, a * qb_n:(a + 1) * qb_n]
                                       for a in range(NSA_QPAIR) for h in range(NSA_HPG)], axis=1)
    gates = [gate(br) for br in range(3)]

    s_cmp = _dot(kc_ref[0, 0], qt)
    span = WINDOW + NSA_QPAIR * qb_n
    w0 = pl.multiple_of(pair * NSA_QPAIR * qb_n, LANES)
    sw_ref[...] = _dot(kw_ref[0, 0, pl.ds(w0, span), :], qt) + wbias_ref[...]
    k0 = pl.multiple_of(diag0 * SEL_BLOCK, NSA_QPAIR * SEL_BLOCK)
    s_diag = _dot(ks_ref[0, 0, pl.ds(k0, NSA_QPAIR * SEL_BLOCK), 0:NSA_HD], qt) + dbias_ref[...]

    n_cmp = kc_ref.shape[2]
    cmp_end = _iota2((n_cmp, 1), 0) * CMP_STRIDE + (CMP_LEN - 1)
    e_c, den_c, _ = softmax_t(jnp.where(cmp_end <= t_lane, s_cmp, -jnp.inf))
    p_c = e_c * (1.0 / jnp.maximum(den_c, 1e-30))
    o_c = _dot(vct_ref[0, 0], p_c.astype(bf16))

    n_sel = ovt_ref.shape[0]
    p_pair = []
    for a in range(NSA_QPAIR):
        acc = p_c[:, a * blk_lanes:a * blk_lanes + LANES]
        for c in range(1, blk_lanes // LANES):
            acc = acc + p_c[:, a * blk_lanes + c * LANES:a * blk_lanes + (c + 1) * LANES]
        p_pair.append(acc[:, :qb_n] + acc[:, qb_n:])
    imp = _dot(ovt_ref[...], jnp.concatenate(p_pair, axis=1), HI)
    blk = _iota2((n_sel, NSA_QPAIR * qb_n), 0)
    cur = pair * NSA_QPAIR + _iota2((n_sel, NSA_QPAIR * qb_n), 1) // qb_n
    valid = blk <= cur
    forced = (blk == 0) | (blk == cur) | (blk == cur - 1)
    score = jnp.where(valid, jnp.where(forced, jnp.inf, imp), -jnp.inf)
    score_ref[...] = score

    e_d, _, m_s = softmax_t(s_diag)
    acc_ref[...] = _dot(vst_ref[0, 0, :, pl.ds(k0, NSA_QPAIR * SEL_BLOCK)], e_d.astype(bf16))

    @pl.when(w0 < WINDOW)
    def _():
        is_pad = _iota2((WINDOW, 1), 0) < WINDOW - w0
        sw_ref[pl.ds(0, WINDOW), :] = jnp.where(is_pad, -jnp.inf, sw_ref[pl.ds(0, WINDOW), :])

    def rank_step(jp, cnt):
        other = score_ref[pl.ds(jp, 1), :]
        ahead = (other > score) | ((other == score) & (blk > jp))
        return cnt + ahead.astype(jnp.int32)

    def rank_step2(jj, cnt):
        return rank_step(2 * jj + 1, rank_step(2 * jj, cnt))

    n_cand = pair * NSA_QPAIR + NSA_QPAIR
    cnt = lax.fori_loop(0, jnp.where(n_cand > SEL_TOPN, n_cand // 2, 0), rank_step2,
                        jnp.zeros((n_sel, NSA_QPAIR * qb_n), jnp.int32))
    bias = jnp.where((cnt < SEL_TOPN) & (blk < diag0), 0.0, NSA_MASKED).astype(bf16)
    bias = _dot(bias, rep_ref[...]).astype(bf16)
    if n_sel < NSA_HD:
        bias = jnp.concatenate([bias, jnp.zeros((NSA_HD - n_sel, lanes), bf16)], axis=0)
    qx = jnp.concatenate([qt, bias], axis=0)

    def score_tile(kt):
        kk0 = pl.multiple_of(kt * key_tile, key_tile)
        return _dot(ks_ref[0, 0, pl.ds(kk0, key_tile), :], qx)

    s_ref[0] = score_tile(0)

    e_w, _, _ = softmax_t(sw_ref[...])
    o_w = _dot(vwt_ref[0, 0, :, pl.ds(w0, span)], e_w.astype(bf16))

    def half_step(kt, slot, carry):
        m_old = carry
        s_ref[1 - slot] = score_tile(jnp.minimum(kt + 1, last_tile))
        sm = s_ref[slot]
        mx = jnp.maximum(m_old, jnp.max(sm, axis=0, keepdims=True))
        alpha = jnp.exp2(m_old - mx)
        p = jnp.exp2(sm - mx)
        kk0 = pl.multiple_of(kt * key_tile, key_tile)
        acc_ref[...] = alpha * acc_ref[...] + _dot(vst_ref[0, 0, :, pl.ds(kk0, key_tile)], p.astype(bf16))
        return mx

    def pair_step(pi, carry):
        return half_step(2 * pi + 1, 1, half_step(2 * pi, 0, carry))

    lax.fori_loop(0, (n_tiles + 1) // 2, pair_step, m_s)

    l_s = acc_ref[pl.ds(NSA_HD, 1), :]
    l_w = o_w[NSA_HD:NSA_HD + 1]
    o_t = (gates[0] * o_c + (gates[1] * (1.0 / l_s)) * acc_ref[pl.ds(0, NSA_HD), :]
           + (gates[2] * (1.0 / l_w)) * o_w[:NSA_HD])
    o_b = o_t.astype(bf16)
    q_idx = _iota2((qb_n, LANES), 0)
    l_idx = _iota2((qb_n, LANES), 1)
    pick = [(l_idx == q_idx + r * qb_n).astype(bf16) for r in range(LANES // qb_n)]
    for a in range(NSA_QPAIR):
        for c in range(blk_lanes // LANES):
            piece = o_b[:, a * blk_lanes + c * LANES:a * blk_lanes + (c + 1) * LANES]
            for r in range(LANES // qb_n):
                h = c * (LANES // qb_n) + r
                o_ref[0, a * qb_n:(a + 1) * qb_n, h * NSA_HD:(h + 1) * NSA_HD] = (
                    _dot_nt(pick[r], piece).astype(bf16))


def nsa_attention(qt, kc, vct, ks, vst, kw, vwt, f3, overlap_t, key_tile=256):
    b, g, n_q, d, blk_lanes = qt.shape
    t = n_q * NSA_QBLOCK
    key_tile = min(key_tile, t)
    assert (t // key_tile) % 2 == 0, "an odd tile count borrows the (fully unselected) tile after the last one"
    assert NSA_QPAIR == 2, "the diagonal tile is masked by causality alone only for a pair of query blocks"
    n_cmp = kc.shape[2]
    n_sel = overlap_t.shape[0]
    lanes = NSA_QPAIR * blk_lanes
    q_rows = NSA_QPAIR * NSA_QBLOCK
    off = (jnp.arange(lanes) // blk_lanes) * NSA_QBLOCK + jnp.arange(lanes) % NSA_QBLOCK
    r_d = jnp.arange(NSA_QPAIR * SEL_BLOCK)[:, None]
    diag_bias = jnp.where(r_d <= off[None, :], 0.0, -jnp.inf).astype(f32)
    r_w = jnp.arange(WINDOW + q_rows)[:, None]
    win_bias = jnp.where((r_w > off[None, :]) & (r_w <= WINDOW + off[None, :]), 0.0, -jnp.inf).astype(f32)
    rep = (jnp.arange(q_rows)[:, None] == off[None, :]).astype(bf16)
    seq = pl.BlockSpec((1, 1, t, ks.shape[3]), lambda bi, gi, qi: (bi, gi, 0, 0))
    seq_t = pl.BlockSpec((1, 1, NSA_VT_ROWS, t), lambda bi, gi, qi: (bi, gi, 0, 0))
    const = lambda a: pl.BlockSpec(a.shape, lambda bi, gi, qi: (0, 0))
    return pl.pallas_call(
        functools.partial(_nsa_attn_body, key_tile=key_tile),
        grid=(b, g, n_q // NSA_QPAIR),
        in_specs=[pl.BlockSpec((1, 1, NSA_QPAIR, d, blk_lanes), lambda bi, gi, qi: (bi, gi, qi, 0, 0)),
                  pl.BlockSpec((1, 1, n_cmp, d), lambda bi, gi, qi: (bi, gi, 0, 0)),
                  pl.BlockSpec((1, 1, d, n_cmp), lambda bi, gi, qi: (bi, gi, 0, 0)),
                  seq, seq_t,
                  pl.BlockSpec((1, 1, t + WINDOW, d), lambda bi, gi, qi: (bi, gi, 0, 0)),
                  pl.BlockSpec((1, 1, NSA_VT_ROWS, t + WINDOW), lambda bi, gi, qi: (bi, gi, 0, 0)),
                  pl.BlockSpec((1, q_rows, LANES), lambda bi, gi, qi: (bi, qi, NSA_GATE0 // LANES + gi)),
                  const(overlap_t), const(diag_bias), const(win_bias), const(rep)],
        out_specs=pl.BlockSpec((1, q_rows, NSA_HPG * d), lambda bi, gi, qi: (bi, qi, gi)),
        out_shape=jax.ShapeDtypeStruct((b, t, g * NSA_HPG * d), bf16),
        scratch_shapes=[pltpu.VMEM((n_sel, q_rows), f32),
                        pltpu.VMEM((2, key_tile, lanes), f32), pltpu.VMEM((NSA_VT_ROWS, lanes), f32),
                        pltpu.VMEM((WINDOW + q_rows, lanes), f32)],
        compiler_params=_cparams(("parallel", "parallel", "arbitrary")),
        name="nsa_attn",
    )(qt, kc, vct, ks, vst, kw, vwt, f3, overlap_t, diag_bias, win_bias, rep)


def _place(cols, total, pieces):
    out = jnp.zeros((cols, total), f32)
    for start, mat in pieces:
        out = lax.dynamic_update_slice(out, mat.astype(f32), (0, start))
    return out


def _layer0(h, b, t, g_pre, w_in, w_out, conv, a_log, dt_bias, gnorm, mu, w0, w2, a0, a2, g2, k_k, k_a,
            r_k, ln_w, ln_b):
    gdn_w = 4 * GDN_HEADS * GDN_D
    w_pad = _place(D_MODEL, AB_COLS, [
        (0, w_in[:, :gdn_w]),
        (AB_BA0, w_in[:, gdn_w:gdn_w + 2 * GDN_HEADS]),
        (AB_RWKV0, w_in[:, gdn_w + 2 * GDN_HEADS:gdn_w + 2 * GDN_HEADS + 3 * RWKV_W]),
        (AB_LORA0, w_in[:, gdn_w + 2 * GDN_HEADS + 3 * RWKV_W:]),
    ]).astype(bf16)
    f3 = norm_matmul(h, g_pre, w_pad).reshape(b, t, AB_COLS)
    arow = jnp.zeros((1, LANES), f32).at[0, GDN_HEADS:2 * GDN_HEADS].set(a_log)
    dtrow = jnp.zeros((1, LANES), f32).at[0, GDN_HEADS:2 * GDN_HEADS].set(dt_bias)
    conv3 = conv.reshape(GDN_CONV, 3, GDN_HEADS * GDN_D).transpose(1, 0, 2)
    o_a = gdn_mixer(f3, conv3, arow, dtrow, gnorm.reshape(1, GDN_D))
    o_b = rwkv_mixer(f3, mu, w0, w2, a0, a2, g2, k_k, k_a, r_k, ln_w, ln_b)
    m = b * t
    n_a = GDN_HEADS * GDN_D
    return [o_a.reshape(m, n_a), o_b.reshape(m, RWKV_W)], [w_out[:n_a].astype(bf16), w_out[n_a:].astype(bf16)]


def _layer1(h, b, t, positions, g_pre, w_in, w_out, pe_k, w1_k, w2_k, pe_v, w1_v, w2_v):
    qw = NSA_HEADS * NSA_HD
    kvw = 6 * NSA_GROUPS * NSA_HD
    gates = w_in[:, qw + kvw:].reshape(D_MODEL, NSA_GROUPS, NSA_HPG * 3)
    w_pad = _place(D_MODEL, NSA_COLS, [(0, w_in[:, :qw + kvw])]
                   + [(NSA_GATE0 + gi * LANES, gates[:, gi]) for gi in range(NSA_GROUPS)]).astype(bf16)
    f3 = norm_matmul(h, g_pre, w_pad).reshape(b, t, NSA_COLS)
    inv_freq = ROPE_THETA ** (-jnp.arange(ROPE_HALF, dtype=f32) * (2.0 / ROPE_DIM))
    lane = jnp.arange(LANES)
    freq_row = jnp.where(lane % NSA_HD < ROPE_DIM, inv_freq[lane % ROPE_HALF], 0.0).reshape(1, LANES).astype(f32)
    posf = positions.astype(f32)
    qt, kc, vc, ks, vst, kw, vwt = nsa_prep(f3, posf.reshape(b, t, 1), freq_row, inv_freq.reshape(ROPE_HALF, 1))
    nrow = t // CMP_STRIDE
    cpos = jnp.concatenate([posf[:, CMP_LEN - 1::CMP_STRIDE], posf[:, -1:]], axis=1).reshape(b, nrow, 1)
    flat = lambda a: a.reshape(b, NSA_GROUPS, nrow, CMP_STRIDE * NSA_HD)
    kcc, vcct = nsa_compress(flat(kc), flat(vc), pe_k.reshape(1, -1), w1_k.astype(bf16), w2_k.astype(bf16),
                             pe_v.reshape(1, -1), w1_v.astype(bf16), w2_v.T.astype(bf16), cpos, freq_row)
    n_sel = t // SEL_BLOCK
    c_start = jnp.arange(nrow) * CMP_STRIDE
    s_start = jnp.arange(n_sel) * SEL_BLOCK
    overlap_t = jnp.clip(jnp.minimum(c_start[None, :] + CMP_LEN, s_start[:, None] + SEL_BLOCK)
                         - jnp.maximum(c_start[None, :], s_start[:, None]), 0, None).astype(f32) / CMP_LEN
    kw_pad = jnp.pad(kw, ((0, 0), (0, 0), (WINDOW, 0), (0, 0)))
    vwt_pad = jnp.pad(vwt, ((0, 0), (0, 0), (0, 0), (WINDOW, 0)))
    o = nsa_attention(qt, kcc, vcct, ks, vst, kw_pad, vwt_pad, f3, overlap_t)
    return [o.reshape(b * t, qw)], [w_out.astype(bf16)]


def kernel(x, positions, norm_mix_pre, norm_mix_post, norm_ffn_pre, norm_ffn_post, w_ffn_up, w_ffn_down, ab_w_in,
           ab_w_out, gdn_conv, gdn_a_log, gdn_dt_bias, gdn_norm, rwkv_mu, rwkv_w0, rwkv_w2, rwkv_a0, rwkv_a2,
           rwkv_g2, rwkv_k_k, rwkv_k_a, rwkv_r_k, rwkv_ln_w, rwkv_ln_b, nsa_w_in, nsa_w_out, nsa_pe_k, nsa_w1_k,
           nsa_w2_k, nsa_pe_v, nsa_w1_v, nsa_w2_v):
    b, t, d = x.shape
    h = x.reshape(b * t, d)
    mix = _layer0(h, b, t, norm_mix_pre[0], ab_w_in[0], ab_w_out[0], gdn_conv[0], gdn_a_log[0],
                  gdn_dt_bias[0], gdn_norm[0], rwkv_mu[0], rwkv_w0[0], rwkv_w2[0], rwkv_a0[0], rwkv_a2[0],
                  rwkv_g2[0], rwkv_k_k[0], rwkv_k_a[0], rwkv_r_k[0].reshape(-1), rwkv_ln_w[0], rwkv_ln_b[0])
    h = out_ffn(*mix, h, norm_mix_post[0], norm_ffn_pre[0], w_ffn_up[0].astype(bf16), w_ffn_down[0].astype(bf16),
                norm_ffn_post[0])
    mix = _layer1(h, b, t, positions, norm_mix_pre[1], nsa_w_in[0], nsa_w_out[0], nsa_pe_k[0],
                  nsa_w1_k[0], nsa_w2_k[0], nsa_pe_v[0], nsa_w1_v[0], nsa_w2_v[0])
    h = out_ffn(*mix, h, norm_mix_post[1], norm_ffn_pre[1], w_ffn_up[1].astype(bf16), w_ffn_down[1].astype(bf16),
                norm_ffn_post[1])
    return h.reshape(b, t, d)
```

```python
import functools
import math

import jax
import jax.numpy as jnp
from jax import lax
from jax.experimental import pallas as pl
from jax.experimental.pallas import tpu as pltpu

f32 = jnp.float32
bf16 = jnp.bfloat16
HI = lax.Precision.HIGHEST

V7X_VMEM_LIMIT_BYTES = 56 * 1024 * 1024
LANES = 128
SUBLANES = 8

D_MODEL = 1024
D_FF = 4 * D_MODEL
DENSE_ROW_BLOCK = 256
DENSE_F32_TAIL = 1024
NORM_EPS = 1e-6
GDN_HEADS = 4
GDN_D = 128
GDN_CONV = 4
GDN_CHUNK = 128
GDN_CHUNK_GROUP = 4
RWKV_HEADS = 8
RWKV_N = 64
RWKV_W = RWKV_HEADS * RWKV_N
RWKV_CHUNK = 64
RWKV_CHUNK_GROUP = 4
RWKV_LN_EPS = 64e-5
RWKV_DECAY_LORA = 64
RWKV_LR_LORA = 64
RWKV_GATE_LORA = 128
NSA_HEADS = 16
NSA_GROUPS = 2
NSA_HPG = NSA_HEADS // NSA_GROUPS
NSA_HD = 64
CMP_LEN = 32
CMP_STRIDE = 16
CMP_HIDDEN = 256
SEL_BLOCK = 64
SEL_TOPN = 16
WINDOW = 512
NSA_QBLOCK = 64
NSA_QPAIR = 2
NSA_VT_ROWS = 80
NSA_MASKED = -(2.0 ** 126)
ROPE_THETA = 500000.0
ROPE_DIM = NSA_HD // 4
ROPE_HALF = ROPE_DIM // 2

AB_COLS = 4096
AB_RWKV0 = 2048
AB_LORA0 = 3584
AB_BA0 = 3840
AB_TAIL0 = AB_COLS - DENSE_F32_TAIL
NSA_COLS = 2048
NSA_KV0 = 1024
NSA_GATE0 = 1792
NSA_TAIL0 = NSA_COLS - DENSE_F32_TAIL


def _cparams(sem):
    return pltpu.CompilerParams(dimension_semantics=sem, vmem_limit_bytes=V7X_VMEM_LIMIT_BYTES)


def _rms(x, g):
    return x * lax.rsqrt(jnp.mean(x * x, axis=-1, keepdims=True) + NORM_EPS) * g


def _dot(a, b, precision=None):
    return jnp.dot(a, b, precision=precision, preferred_element_type=f32)


def _dot_nt(a, b, precision=None):
    return lax.dot_general(a, b, (((1,), (1,)), ((), ())), precision=precision, preferred_element_type=f32)


def _dot_tn(a, b, precision=None):
    return lax.dot_general(a, b, (((0,), (0,)), ((), ())), precision=precision, preferred_element_type=f32)


def _bdot(a, b):
    return _dot(a.astype(bf16), b.astype(bf16))


def _bdot_nt(a, b):
    return _dot_nt(a.astype(bf16), b.astype(bf16))


def _bdot_tn(a, b):
    return _dot_tn(a.astype(bf16), b.astype(bf16))


def _dot01(m01, x):
    m = m01.astype(bf16)
    hi = x.astype(bf16)
    rest = x - hi.astype(f32)
    mid = rest.astype(bf16)
    lo = (rest - mid.astype(f32)).astype(bf16)
    return _dot(m, hi) + _dot(m, mid) + _dot(m, lo)


def _iota2(shape, axis):
    return lax.broadcasted_iota(jnp.int32, shape, axis)


def _norm_matmul_body(x_ref, g_ref, w_ref, o16_ref, o32_ref, u_ref):
    j = pl.program_id(1)
    last = pl.num_programs(1) - 1
    tm = x_ref.shape[0]
    row_blocks = [pl.ds(r0, min(tm, DENSE_ROW_BLOCK)) for r0 in range(0, tm, min(tm, DENSE_ROW_BLOCK))]

    @pl.when(j == 0)
    def _():
        for rows in row_blocks:
            u_ref[rows, :] = _rms(x_ref[rows, :], g_ref[...]).astype(bf16)
            o16_ref[rows, :] = _dot(u_ref[rows, :], w_ref[...]).astype(bf16)

    @pl.when((j > 0) & (j < last))
    def _():
        o16_ref[...] = _dot(u_ref[...], w_ref[...]).astype(bf16)

    @pl.when(j == last)
    def _():
        o32_ref[...] = _dot(u_ref[...], w_ref[...])


def norm_matmul(x, g, w, tm=2048, tn=DENSE_F32_TAIL):
    m, d = x.shape
    n = w.shape[1]
    tm = min(tm, m)
    n_col = n // tn
    assert n % tn == 0 and n_col >= 2
    return pl.pallas_call(
        _norm_matmul_body,
        grid=(m // tm, n_col),
        in_specs=[pl.BlockSpec((tm, d), lambda i, j: (i, 0)),
                  pl.BlockSpec((1, d), lambda i, j: (0, 0)),
                  pl.BlockSpec((d, tn), lambda i, j: (0, j))],
        out_specs=[pl.BlockSpec((tm, tn), lambda i, j: (i, jnp.minimum(j, n_col - 2))),
                   pl.BlockSpec((tm, tn), lambda i, j: (i, 0))],
        out_shape=[jax.ShapeDtypeStruct((m, n - tn), bf16), jax.ShapeDtypeStruct((m, tn), f32)],
        scratch_shapes=[pltpu.VMEM((tm, d), bf16)],
        compiler_params=_cparams(("parallel", "arbitrary")),
        name="norm_matmul",
    )(x, g.reshape(1, d), w)


def _out_ffn_body(*refs, n_parts):
    a_refs = refs[:n_parts]
    w_refs = refs[n_parts:2 * n_parts]
    h_ref, gm_ref, g1_ref, wup_ref, wdn_ref, g2_ref, o_ref, h1_ref, u_ref, acc_ref = refs[2 * n_parts:]
    k = pl.program_id(1)
    last = pl.num_programs(1) - 1
    tm = h_ref.shape[0]
    row_blocks = [pl.ds(r0, min(tm, DENSE_ROW_BLOCK)) for r0 in range(0, tm, min(tm, DENSE_ROW_BLOCK))]

    def mlp_step(rows):
        a = _dot(u_ref[rows, :], wup_ref[...])
        a = jnp.square(jnp.maximum(a, 0.0))
        return _dot(a.astype(bf16), wdn_ref[...])

    def out_proj(rows):
        y = _dot(a_refs[0][rows, :].astype(bf16), w_refs[0][...])
        for a_ref, w_ref in zip(a_refs[1:], w_refs[1:]):
            y = y + _dot(a_ref[rows, :].astype(bf16), w_ref[...])
        return y

    @pl.when(k == 0)
    def _():
        y = out_proj(row_blocks[0])
        for idx, rows in enumerate(row_blocks):
            h1 = h_ref[rows, :] + _rms(y, gm_ref[...])
            h1_ref[rows, :] = h1
            u_ref[rows, :] = _rms(h1, g1_ref[...]).astype(bf16)
            if idx + 1 < len(row_blocks):
                y = out_proj(row_blocks[idx + 1])
            acc_ref[rows, :] = mlp_step(rows)

    @pl.when((k > 0) & (k < last))
    def _():
        for rows in row_blocks:
            acc_ref[rows, :] += mlp_step(rows)

    @pl.when(k == last)
    def _():
        for rows in row_blocks:
            o_ref[rows, :] = h1_ref[rows, :] + _rms(acc_ref[rows, :] + mlp_step(rows), g2_ref[...])


def out_ffn(parts, w_parts, h, g_mix, g1, wup, wdn, g2, tm=1024, tf=1024):
    m, d = h.shape
    ff = wup.shape[1]
    tm = min(tm, m)
    assert ff // tf >= 2, "the first and the last reduction step are distinct branches"
    n_parts = len(parts)
    row = pl.BlockSpec((1, d), lambda i, k: (0, 0))
    in_specs = ([pl.BlockSpec((tm, p.shape[1]), lambda i, k: (i, 0)) for p in parts]
                + [pl.BlockSpec(w.shape, lambda i, k: (0, 0)) for w in w_parts]
                + [pl.BlockSpec((tm, d), lambda i, k: (i, 0)), row, row,
                   pl.BlockSpec((d, tf), lambda i, k: (0, k)),
                   pl.BlockSpec((tf, d), lambda i, k: (k, 0)), row])
    return pl.pallas_call(
        functools.partial(_out_ffn_body, n_parts=n_parts),
        grid=(m // tm, ff // tf),
        in_specs=in_specs,
        out_specs=pl.BlockSpec((tm, d), lambda i, k: (i, 0)),
        out_shape=jax.ShapeDtypeStruct((m, d), f32),
        scratch_shapes=[pltpu.VMEM((tm, d), f32), pltpu.VMEM((tm, d), bf16), pltpu.VMEM((tm, d), f32)],
        compiler_params=_cparams(("parallel", "arbitrary")),
        name="out_ffn",
    )(*parts, *w_parts, h, g_mix.reshape(1, d), g1.reshape(1, d), wup, wdn, g2.reshape(1, d))


def _neumann_inverses(n_mats, size):
    eye = (_iota2((size, size), 0) == _iota2((size, size), 1)).astype(f32)
    ts = [eye + n for n in n_mats]
    ps = list(n_mats)
    for _ in range(int(math.log2(size)) - 1):
        ps = [_bdot(p, p) for p in ps]
        ts = [t + _bdot(t, p) for t, p in zip(ts, ps)]
    return ts


def _gdn_body(q_ref, k_ref, v_ref, z_ref, ba_ref, cw_ref, arow_ref, dtrow_ref, nw_ref, o_ref,
              xp_ref, qkv_ref, s_ref, *, tt):
    c = GDN_CHUNK
    d = GDN_D
    heads = range(GDN_HEADS)
    width = GDN_HEADS * d

    @pl.when(pl.program_id(1) == 0)
    def _():
        xp_ref[:, pl.ds(0, SUBLANES), :] = jnp.zeros((3, SUBLANES, width), f32)
        s_ref[...] = jnp.zeros_like(s_ref)

    conv_rows = min(tt, 64)
    for idx, ref in enumerate((q_ref, k_ref, v_ref)):
        xp_ref[idx, pl.ds(SUBLANES, tt), :] = ref[0].astype(f32)
        for h in heads:
            cols = slice(h * d, (h + 1) * d)
            w = cw_ref[idx, :, cols]
            for r0 in range(0, tt, conv_rows):
                y = xp_ref[idx, pl.ds(SUBLANES + r0, conv_rows), cols] * w[GDN_CONV - 1:GDN_CONV, :]
                for j in range(GDN_CONV - 1):
                    y = y + xp_ref[idx, pl.ds(SUBLANES - (GDN_CONV - 1) + j + r0, conv_rows), cols] * w[j:j + 1, :]
                y = y * jax.nn.sigmoid(y)
                if idx < 2:
                    y = y * lax.rsqrt(jnp.sum(y * y, axis=-1, keepdims=True) + 1e-6)
                if idx == 0:
                    y = y * (d ** -0.5)
                qkv_ref[idx, pl.ds(r0, conv_rows), cols] = y
        xp_ref[idx, pl.ds(0, SUBLANES), :] = xp_ref[idx, pl.ds(tt, SUBLANES), :]

    row = _iota2((c, c), 0)
    col = _iota2((c, c), 1)
    tril = row >= col
    strict = row > col
    eye = row == col
    cum_l = tril.astype(f32)
    last_row = _iota2((c, 1), 0) == c - 1

    group = GDN_CHUNK_GROUP
    nh = GDN_HEADS

    def chunk_group(gi, carry):
        rows = [pl.ds(pl.multiple_of((gi * group + j) * c, c), c) for j in range(group)]
        sig, gcum = [], []
        for r in rows:
            ba = ba_ref[0, r, :]
            sig.append(jax.nn.sigmoid(ba))
            gcum.append(_dot01(cum_l, -jnp.exp(arow_ref[...]) * jax.nn.softplus(ba + dtrow_ref[...])))
        units = lambda f: [f(j, h) for j in range(group) for h in heads]
        qn = units(lambda j, h: qkv_ref[0, rows[j], h * d:(h + 1) * d])
        kn = units(lambda j, h: qkv_ref[1, rows[j], h * d:(h + 1) * d])
        vv = units(lambda j, h: qkv_ref[2, rows[j], h * d:(h + 1) * d])
        beta = units(lambda j, h: sig[j][:, h:h + 1])
        gc = units(lambda j, h: gcum[j][:, nh + h:nh + h + 1])
        gc_row = [jnp.sum(jnp.where(eye, jnp.broadcast_to(g, (c, c)), 0.0), axis=0, keepdims=True) for g in gc]
        gc_last = [jnp.sum(jnp.where(last_row, g, 0.0), axis=0, keepdims=True) for g in gc]
        decay = [jnp.exp(jnp.where(tril, g - gr, -jnp.inf)) for g, gr in zip(gc, gc_row)]
        knb = [k.astype(bf16) for k in kn]
        kk = [_dot_nt(k, k) for k in knb]
        qk = [_dot_nt(q.astype(bf16), k) for q, k in zip(qn, knb)]
        t_inv = _neumann_inverses([-jnp.where(strict, b * x * dc, 0.0) for b, x, dc in zip(beta, kk, decay)], c)
        egc = [jnp.exp(g) for g in gc]
        u = [_bdot(t, v * b) for t, v, b in zip(t_inv, vv, beta)]
        w = [_bdot(t, k * (b * e)).astype(bf16) for t, k, b, e in zip(t_inv, kn, beta, egc)]
        intra = [jnp.where(tril, x * dc, 0.0).astype(bf16) for x, dc in zip(qk, decay)]
        q_g = [(q * e).astype(bf16) for q, e in zip(qn, egc)]
        k_g = [(k * jnp.exp(gl - g)).astype(bf16) for k, gl, g in zip(kn, gc_last, gc)]
        state = [s_ref[h] for h in heads]
        for j in range(group):
            sl = slice(j * nh, (j + 1) * nh)
            sb = [x.astype(bf16) for x in state]
            v_new = [(x - _dot(y, z)).astype(bf16) for x, y, z in zip(u[sl], w[sl], sb)]
            o = [_dot(q, z) + _dot(a, vn) for q, z, a, vn in zip(q_g[sl], sb, intra[sl], v_new)]
            state = [s * jnp.exp(gl) + _dot_tn(k, vn) for s, gl, k, vn in zip(state, gc_last[sl], k_g[sl], v_new)]
            for h in heads:
                z = z_ref[0, rows[j], h * d:(h + 1) * d].astype(f32)
                o_ref[0, rows[j], h * d:(h + 1) * d] = _rms(o[h], nw_ref[...]) * (z * jax.nn.sigmoid(z))
        for h in heads:
            s_ref[h] = state[h]
        return carry

    lax.fori_loop(0, tt // (c * group), chunk_group, 0)


def gdn_mixer(f16, f32_tail, conv_w, arow, dtrow, norm_w, tt=512):
    b, t, _ = f16.shape
    tt = min(tt, t)
    assert t % tt == 0 and tt % (GDN_CHUNK * GDN_CHUNK_GROUP) == 0
    width = GDN_HEADS * GDN_D
    col = lambda j: pl.BlockSpec((1, tt, width), lambda bi, ti, j=j: (bi, ti, j))
    return pl.pallas_call(
        functools.partial(_gdn_body, tt=tt),
        grid=(b, t // tt),
        in_specs=[col(0), col(1), col(2), col(3),
                  pl.BlockSpec((1, tt, LANES), lambda bi, ti: (bi, ti, (AB_BA0 - AB_TAIL0) // LANES)),
                  pl.BlockSpec((3, GDN_CONV, width), lambda bi, ti: (0, 0, 0)),
                  pl.BlockSpec((1, LANES), lambda bi, ti: (0, 0)),
                  pl.BlockSpec((1, LANES), lambda bi, ti: (0, 0)),
                  pl.BlockSpec((1, GDN_D), lambda bi, ti: (0, 0))],
        out_specs=pl.BlockSpec((1, tt, width), lambda bi, ti: (bi, ti, 0)),
        out_shape=jax.ShapeDtypeStruct((b, t, width), f32),
        scratch_shapes=[pltpu.VMEM((3, tt + SUBLANES, width), f32),
                        pltpu.VMEM((3, tt, width), f32),
                        pltpu.VMEM((GDN_HEADS, GDN_D, GDN_D), f32)],
        compiler_params=_cparams(("parallel", "arbitrary")),
        name="gdn",
    )(f16, f16, f16, f16, f32_tail, conv_w, arow, dtrow, norm_w)


def _rwkv_body(r_ref, k_ref, v_ref, l_ref, mu_ref, w0_ref, w2_ref, a0_ref, a2_ref, g2_ref, kkw_ref, ka_ref,
               rk_ref, lnw_ref, lnb_ref, o_ref, xp_ref, r_s, lw_s, k2_s, v_s, kk_s, a_s, g_s, s_ref, *, tt):
    c = RWKV_CHUNK
    n = RWKV_N
    w = RWKV_W

    @pl.when(pl.program_id(1) == 0)
    def _():
        xp_ref[pl.ds(0, SUBLANES), :] = jnp.zeros((SUBLANES, xp_ref.shape[1]), f32)
        s_ref[...] = jnp.zeros_like(s_ref)

    xp_ref[pl.ds(SUBLANES, tt), 0:w] = r_ref[0].astype(f32)
    xp_ref[pl.ds(SUBLANES, tt), w:2 * w] = k_ref[0].astype(f32)
    xp_ref[pl.ds(SUBLANES, tt), 2 * w:3 * w] = v_ref[0]
    xp_ref[pl.ds(SUBLANES, tt), 3 * w:] = l_ref[0]
    x = xp_ref[pl.ds(SUBLANES, tt), :]
    x = x + (xp_ref[pl.ds(SUBLANES - 1, tt), :] - x) * mu_ref[...]
    xp_ref[pl.ds(0, SUBLANES), :] = xp_ref[pl.ds(tt, SUBLANES), :]

    k = x[:, w:2 * w]
    lora0 = 3 * w
    wd = x[:, lora0:lora0 + RWKV_DECAY_LORA]
    ad = x[:, lora0 + RWKV_DECAY_LORA:lora0 + RWKV_DECAY_LORA + RWKV_LR_LORA]
    gd = x[:, lora0 + RWKV_DECAY_LORA + RWKV_LR_LORA:]
    w_log = -jax.nn.softplus(-(w0_ref[...] + _dot(jnp.tanh(wd), w2_ref[...], HI))) - 0.5
    lr = jax.nn.sigmoid(a0_ref[...] + _dot(ad, a2_ref[...], HI))
    r_s[...] = x[:, 0:w]
    lw_s[...] = -jnp.exp(w_log)
    k2_s[...] = k * (1.0 + (lr - 1.0) * ka_ref[...])
    v_s[...] = x[:, 2 * w:3 * w]
    kk_s[...] = k * kkw_ref[...]
    a_s[...] = lr
    g_s[...] = _dot(jax.nn.sigmoid(gd), g2_ref[...], HI)

    row = _iota2((c, c), 0)
    col = _iota2((c, c), 1)
    tril = row >= col
    strict = row > col
    cum_l = tril.astype(f32)
    last = _iota2((c, 1), 0) == c - 1

    heads = range(RWKV_HEADS)
    per_head = lambda x: [x[:, h * n:(h + 1) * n] for h in heads]

    group = RWKV_CHUNK_GROUP

    def chunk_group(gi, carry):
        rows = [pl.ds(pl.multiple_of((gi * group + j) * c, c), c) for j in range(group)]
        units = lambda f: [x for r in rows for x in f(r)]
        stack = lambda xs, ys: [jnp.concatenate([x, y], axis=0) for x, y in zip(xs, ys)]
        lw_all = [lw_s[r, :] for r in rows]
        p_all = [_dot01(cum_l, x) for x in lw_all]
        em_all = [jnp.exp(-p) for p in p_all]
        rh = units(lambda r: per_head(r_s[r, :]))
        k2 = units(lambda r: per_head(k2_s[r, :]))
        vh = units(lambda r: per_head(v_s[r, :]))
        r_t = [x for r, p in zip(rows, p_all) for x in per_head(r_s[r, :] * jnp.exp(p))]
        k_t = [x for r, e in zip(rows, em_all) for x in per_head(k2_s[r, :] * e)]
        e_prev = [x for p, lw in zip(p_all, lw_all) for x in per_head(jnp.exp(p - lw))]
        lr_em = [x for r, e in zip(rows, em_all) for x in per_head(a_s[r, :] * e)]
        kk = [x * lax.rsqrt(jnp.sum(x * x, axis=-1, keepdims=True) + 1e-6)
              for x in units(lambda r: per_head(kk_s[r, :]))]
        a_t = [-x * e for x, e in zip(kk, e_prev)]
        b_t = [x * e for x, e in zip(kk, lr_em)]
        ar = [x.astype(bf16) for x in stack(a_t, r_t)]
        bk = [x.astype(bf16) for x in stack(b_t, k_t)]
        ar_b = [_dot_nt(x, y[:c]) for x, y in zip(ar, bk)]
        ar_k = [_dot_nt(x, y[c:]) for x, y in zip(ar, bk)]
        m_ab = [jnp.where(strict, x[:c], 0.0) for x in ar_b]
        a_rb = [jnp.where(tril, x[c:], 0.0) for x in ar_b]
        mk_rk = [jnp.where(jnp.concatenate([strict, tril], axis=0), x, 0.0) for x in ar_k]
        eye = (row == col).astype(f32)
        t_inv = [eye + x for x in m_ab]
        pw = [_bdot(x, x) for x in m_ab]
        for _ in range(int(math.log2(c)) - 1):
            z = [_bdot(x, p) for x, p in zip(stack(pw, t_inv), pw)]
            pw = [x[:c] for x in z]
            t_inv = [t + x[c:] for t, x in zip(t_inv, z)]
        mv_yv = [_bdot(x, v) for x, v in zip(mk_rk, vh)]
        w1 = [_bdot(t, a) for t, a in zip(t_inv, a_t)]
        u2 = [_bdot(t, x[:c]) for t, x in zip(t_inv, mv_yv)]
        y_v = [x[c:] for x in mv_yv]
        w1r = stack(w1, r_t)
        decay_last = [x for p in p_all
                      for x in per_head(jnp.exp(jnp.sum(jnp.where(last, p, 0.0), axis=0, keepdims=True)))]
        g_h = units(lambda r: per_head(g_s[r, :]))
        bonus = [jnp.sum(r * k * rk_ref[:, (i % RWKV_HEADS) * n:(i % RWKV_HEADS + 1) * n], axis=-1, keepdims=True) * v
                 for i, (r, k, v) in enumerate(zip(rh, k2, vh))]
        state = [s_ref[h] for h in heads]
        for j in range(group):
            sl_u = slice(j * RWKV_HEADS, (j + 1) * RWKV_HEADS)
            ws = [_bdot_nt(x, s) for x, s in zip(w1r[sl_u], state)]
            u = [x[:c] + y for x, y in zip(ws, u2[sl_u])]
            y = [x[c:] + _bdot(a, z) + yv for x, a, z, yv in zip(ws, a_rb[sl_u], u, y_v[sl_u])]
            ds = [_dot_tn(x.astype(bf16), y) for x, y in zip(stack(u, vh[sl_u]), bk[sl_u])]
            state = [(s + d) * dl for s, d, dl in zip(state, ds, decay_last[sl_u])]
            for h in heads:
                sl = slice(h * n, (h + 1) * n)
                mean = jnp.mean(y[h], axis=-1, keepdims=True)
                yc = y[h] - mean
                var = jnp.mean(yc * yc, axis=-1, keepdims=True)
                yn = yc * lax.rsqrt(var + RWKV_LN_EPS) * lnw_ref[:, sl] + lnb_ref[:, sl]
                o_ref[0, rows[j], sl] = (yn + bonus[j * RWKV_HEADS + h]) * g_h[j * RWKV_HEADS + h]
        for h in heads:
            s_ref[h] = state[h]
        return carry

    lax.fori_loop(0, tt // (c * group), chunk_group, 0)


def rwkv_mixer(f16, f32_tail, mu, w0, w2, a0, a2, g2, k_k, k_a, r_k, ln_w, ln_b, tt=256):
    b, t, _ = f16.shape
    tt = min(tt, t)
    assert t % tt == 0 and tt % (RWKV_CHUNK * RWKV_CHUNK_GROUP) == 0
    w = RWKV_W
    lora = RWKV_DECAY_LORA + RWKV_LR_LORA + RWKV_GATE_LORA
    wide = 3 * w + lora
    row = lambda n: pl.BlockSpec((1, n), lambda bi, ti: (0, 0))
    full = lambda a: pl.BlockSpec(a.shape, lambda bi, ti: (0, 0))
    col = lambda j: pl.BlockSpec((1, tt, w), lambda bi, ti, j=j: (bi, ti, AB_RWKV0 // w + j))
    v_col = (AB_RWKV0 + 2 * w - AB_TAIL0) // w
    return pl.pallas_call(
        functools.partial(_rwkv_body, tt=tt),
        grid=(b, t // tt),
        in_specs=[col(0), col(1), pl.BlockSpec((1, tt, w), lambda bi, ti: (bi, ti, v_col)),
                  pl.BlockSpec((1, tt, lora), lambda bi, ti: (bi, ti, (AB_LORA0 - AB_TAIL0) // lora)),
                  row(wide), row(w), full(w2), row(w), full(a2), full(g2)] + [row(w)] * 5,
        out_specs=pl.BlockSpec((1, tt, w), lambda bi, ti: (bi, ti, 0)),
        out_shape=jax.ShapeDtypeStruct((b, t, w), f32),
        scratch_shapes=[pltpu.VMEM((tt + SUBLANES, wide), f32)] + [pltpu.VMEM((tt, w), f32)] * 7
                       + [pltpu.VMEM((RWKV_HEADS, RWKV_N, RWKV_N), f32)],
        compiler_params=_cparams(("parallel", "arbitrary")),
        name="rwkv",
    )(f16, f16, f32_tail, f32_tail, mu.reshape(1, wide), w0.reshape(1, w), w2, a0.reshape(1, w), a2, g2,
      k_k.reshape(1, w), k_a.reshape(1, w), r_k.reshape(1, w), ln_w.reshape(1, w), ln_b.reshape(1, w))


def _rope128(x, cos, sin, sign_lo, sign_hi):
    r_hi = pltpu.roll(x, ROPE_HALF, 1)
    r_lo = pltpu.roll(x, LANES - ROPE_HALF, 1)
    return x * cos + (r_lo * sign_lo + r_hi * sign_hi) * sin


def _rope_tables(pos, freq_row):
    ang = pos * freq_row
    m = _iota2((1, LANES), 1) % NSA_HD
    sign_lo = jnp.where(m < ROPE_HALF, -1.0, 0.0).astype(f32)
    sign_hi = jnp.where((m >= ROPE_HALF) & (m < ROPE_DIM), 1.0, 0.0).astype(f32)
    return jnp.cos(ang), jnp.sin(ang), sign_lo, sign_hi


def _nsa_prep_body(q_ref, kc_i, vc_i, ks_i, vs_i, kw_i, vw_i, pos_ref, posr_ref, freq_ref, fcol_ref,
                   qo_ref, kc_ref, vc_ref, ks_ref, vs_ref, kw_ref, vw_ref):
    cos, sin, s_lo, s_hi = _rope_tables(pos_ref[0], freq_ref[...])
    scale = NSA_HD ** -0.5 * math.log2(math.e)
    n_blk = q_ref.shape[1] // NSA_QBLOCK
    eye = (_iota2((LANES, LANES), 0) == _iota2((LANES, LANES), 1)).astype(bf16)

    def transpose_bf16(x):
        return _dot_nt(eye, x.astype(bf16))

    ang_t = fcol_ref[...] * posr_ref[0]
    cos_t, sin_t = jnp.cos(ang_t), jnp.sin(ang_t)
    for c in range(NSA_HEADS // 2):
        xt = transpose_bf16(q_ref[0, :, c * LANES:(c + 1) * LANES].astype(f32) * scale)
        for r in range(2):
            head = 2 * c + r
            g, hl = head // NSA_HPG, head % NSA_HPG
            x1 = xt[r * NSA_HD:r * NSA_HD + ROPE_HALF]
            x2 = xt[r * NSA_HD + ROPE_HALF:r * NSA_HD + ROPE_DIM]
            roped = jnp.concatenate([x1 * cos_t - x2 * sin_t, x2 * cos_t + x1 * sin_t,
                                     xt[r * NSA_HD + ROPE_DIM:(r + 1) * NSA_HD]], axis=0).astype(bf16)
            for i in range(n_blk):
                qo_ref[0, g, i, :, hl * NSA_QBLOCK:(hl + 1) * NSA_QBLOCK] = (
                    roped[:, i * NSA_QBLOCK:(i + 1) * NSA_QBLOCK])

    def split(src, ref, rope, dtype):
        x = src[0]
        if rope:
            x = _rope128(x, cos, sin, s_lo, s_hi)
        ref[0, 0] = x[:, :NSA_HD].astype(dtype)
        ref[0, 1] = x[:, NSA_HD:].astype(dtype)

    def split_t(src, ref):
        xt = transpose_bf16(src[0])
        extra = (_iota2((NSA_VT_ROWS - NSA_HD, xt.shape[1]), 0) == 0).astype(f32)
        for g in range(NSA_GROUPS):
            ref[0, g] = jnp.concatenate([xt[g * NSA_HD:(g + 1) * NSA_HD], extra], axis=0).astype(bf16)

    split(kc_i, kc_ref, False, f32)
    split(vc_i, vc_ref, False, f32)
    tt = ks_i.shape[1]
    tok = pl.program_id(1) * tt + _iota2((tt, NSA_HD), 0)
    block_onehot = (tok // SEL_BLOCK == _iota2((tt, NSA_HD), 1)).astype(f32)
    ks = _rope128(ks_i[0], cos, sin, s_lo, s_hi)
    for g in range(NSA_GROUPS):
        ks_ref[0, g] = jnp.concatenate([ks[:, g * NSA_HD:(g + 1) * NSA_HD], block_onehot], axis=1).astype(bf16)
    split_t(vs_i, vs_ref)
    split(kw_i, kw_ref, True, bf16)
    split_t(vw_i, vw_ref)


def nsa_prep(f16, f32_tail, pos3, freq_row, freq_col, tt=512):
    b, t, _ = f16.shape
    assert t // SEL_BLOCK <= NSA_HD, "the selection-block one-hot shares the key tile's second 64 lanes"
    tt = min(tt, t)
    g = NSA_GROUPS
    n_blk = tt // NSA_QBLOCK
    kv_in = [pl.BlockSpec((1, tt, LANES), lambda bi, ti, c=(NSA_KV0 - NSA_TAIL0) // LANES + i: (bi, ti, c))
             for i in range(6)]
    kv_spec = pl.BlockSpec((1, g, tt, NSA_HD), lambda bi, ti: (bi, 0, ti, 0))
    kvt_spec = pl.BlockSpec((1, g, NSA_VT_ROWS, tt), lambda bi, ti: (bi, 0, 0, ti))
    kv32 = jax.ShapeDtypeStruct((b, g, t, NSA_HD), f32)
    kv16 = jax.ShapeDtypeStruct((b, g, t, NSA_HD), bf16)
    kvt16 = jax.ShapeDtypeStruct((b, g, NSA_VT_ROWS, t), bf16)
    q_lanes = NSA_HPG * NSA_QBLOCK
    return pl.pallas_call(
        _nsa_prep_body,
        grid=(b, t // tt),
        in_specs=[pl.BlockSpec((1, tt, NSA_HEADS * NSA_HD), lambda bi, ti: (bi, ti, 0))] + kv_in
                 + [pl.BlockSpec((1, tt, 1), lambda bi, ti: (bi, ti, 0)),
                    pl.BlockSpec((1, 1, tt), lambda bi, ti: (bi, 0, ti)),
                    pl.BlockSpec((1, LANES), lambda bi, ti: (0, 0)),
                    pl.BlockSpec((ROPE_HALF, 1), lambda bi, ti: (0, 0))],
        out_specs=[pl.BlockSpec((1, g, n_blk, NSA_HD, q_lanes), lambda bi, ti: (bi, 0, ti, 0, 0)),
                   kv_spec, kv_spec, pl.BlockSpec((1, g, tt, 2 * NSA_HD), lambda bi, ti: (bi, 0, ti, 0)),
                   kvt_spec, kv_spec, kvt_spec],
        out_shape=[jax.ShapeDtypeStruct((b, g, t // NSA_QBLOCK, NSA_HD, q_lanes), bf16),
                   kv32, kv32, jax.ShapeDtypeStruct((b, g, t, 2 * NSA_HD), bf16), kvt16, kv16, kvt16],
        compiler_params=_cparams(("parallel", "parallel")),
        name="nsa_prep",
    )(f16, f32_tail, f32_tail, f32_tail, f32_tail, f32_tail, f32_tail, pos3, pos3.reshape(b, 1, t), freq_row,
      freq_col)


def _nsa_compress_body(kc_ref, vc_ref, pek_ref, w1k_ref, w2k_ref, pev_ref, w1v_ref, w2v_ref, pos_ref, freq_ref,
                       ko_ref, vo_ref):
    half = CMP_STRIDE * NSA_HD
    nrow = kc_ref.shape[2]
    last_row = _iota2((nrow, 1), 0) == nrow - 1

    def hidden(x, pe_ref, w1_ref):
        lo = _dot((x + pe_ref[:, :half]).astype(bf16), w1_ref[:half, :])
        hi = _dot((x + pe_ref[:, half:]).astype(bf16), w1_ref[half:, :])
        hi = jnp.where(last_row, 0.0, pltpu.roll(hi, nrow - 1, 0))
        hid = lo + hi
        return (hid * jax.nn.sigmoid(hid)).astype(bf16)

    ks = [_dot(hidden(kc_ref[0, g], pek_ref, w1k_ref), w2k_ref[...]) for g in range(NSA_GROUPS)]
    cos, sin, s_lo, s_hi = _rope_tables(pos_ref[0], freq_ref[...])
    kr = _rope128(jnp.concatenate(ks, axis=-1), cos, sin, s_lo, s_hi)
    for g in range(NSA_GROUPS):
        ko_ref[0, g] = kr[:, g * NSA_HD:(g + 1) * NSA_HD].astype(bf16)
        vo_ref[0, g] = _dot_nt(w2v_ref[...], hidden(vc_ref[0, g], pev_ref, w1v_ref)).astype(bf16)


def nsa_compress(kc4, vc4, pe_k, w1_k, w2_k, pe_v, w1_v, w2_vt, cpos3, freq_row):
    b, g, nrow, wide = kc4.shape
    full = lambda a: pl.BlockSpec(a.shape, lambda bi: (0,) * a.ndim)
    blk = pl.BlockSpec((1, g, nrow, wide), lambda bi: (bi, 0, 0, 0))
    out = pl.BlockSpec((1, g, nrow, NSA_HD), lambda bi: (bi, 0, 0, 0))
    out_t = pl.BlockSpec((1, g, NSA_HD, nrow), lambda bi: (bi, 0, 0, 0))
    return pl.pallas_call(
        _nsa_compress_body,
        grid=(b,),
        in_specs=[blk, blk, full(pe_k), full(w1_k), full(w2_k), full(pe_v), full(w1_v), full(w2_vt),
                  pl.BlockSpec((1, nrow, 1), lambda bi: (bi, 0, 0)), pl.BlockSpec((1, LANES), lambda bi: (0, 0))],
        out_specs=[out, out_t],
        out_shape=[jax.ShapeDtypeStruct((b, g, nrow, NSA_HD), bf16),
                   jax.ShapeDtypeStruct((b, g, NSA_HD, nrow), bf16)],
        compiler_params=_cparams(("parallel",)),
        name="nsa_compress",
    )(kc4, vc4, pe_k, w1_k, w2_k, pe_v, w1_v, w2_vt, cpos3, freq_row)


def _nsa_attn_body(qt_ref, kc_ref, vct_ref, ks_ref, vst_ref, kw_ref, vwt_ref, gate_ref, ovt_ref, dbias_ref,
                   wbias_ref, rep_ref, o_ref, score_ref, s_ref, acc_ref, sw_ref, *, key_tile):
    qb_n = NSA_QBLOCK
    blk_lanes = NSA_HPG * qb_n
    lanes = NSA_QPAIR * blk_lanes
    pair = pl.program_id(2)
    qt = jnp.concatenate([qt_ref[0, 0, a] for a in range(NSA_QPAIR)], axis=1)
    lane = _iota2((1, lanes), 1)
    t_lane = (pair * NSA_QPAIR + lane // blk_lanes) * qb_n + lane % qb_n

    def softmax_t(s):
        mx = jnp.max(s, axis=0, keepdims=True)
        mx = jnp.where(mx > -jnp.inf, mx, 0.0)
        e = jnp.exp2(s - mx)
        return e, jnp.sum(e, axis=0, keepdims=True), mx

    diag0 = pair * NSA_QPAIR
    n_tiles = (diag0 * SEL_BLOCK + key_tile - 1) // key_tile
    last_tile = ks_ref.shape[2] // key_tile - 1

    gt = jax.nn.sigmoid(gate_ref[0]).T
    gate = lambda br: jnp.concatenate([gt[3 * h + br:3 * h + br + 1, a * qb_n:(a + 1) * qb_n]
                                       for a in range(NSA_QPAIR) for h in range(NSA_HPG)], axis=1)
    gates = [gate(br) for br in range(3)]

    s_cmp = _dot(kc_ref[0, 0], qt)
    span = WINDOW + NSA_QPAIR * qb_n
    w0 = pl.multiple_of(pair * NSA_QPAIR * qb_n, LANES)
    sw_ref[...] = _dot(kw_ref[0, 0, pl.ds(w0, span), :], qt) + wbias_ref[...]
    k0 = pl.multiple_of(diag0 * SEL_BLOCK, NSA_QPAIR * SEL_BLOCK)
    s_diag = _dot(ks_ref[0, 0, pl.ds(k0, NSA_QPAIR * SEL_BLOCK), 0:NSA_HD], qt) + dbias_ref[...]

    n_cmp = kc_ref.shape[2]
    cmp_end = _iota2((n_cmp, 1), 0) * CMP_STRIDE + (CMP_LEN - 1)
    e_c, den_c, _ = softmax_t(jnp.where(cmp_end <= t_lane, s_cmp, -jnp.inf))
    p_c = e_c * (1.0 / jnp.maximum(den_c, 1e-30))
    o_c = _dot(vct_ref[0, 0], p_c.astype(bf16))

    n_sel = ovt_ref.shape[0]
    p_pair = []
    for a in range(NSA_QPAIR):
        acc = p_c[:, a * blk_lanes:a * blk_lanes + LANES]
        for c in range(1, blk_lanes // LANES):
            acc = acc + p_c[:, a * blk_lanes + c * LANES:a * blk_lanes + (c + 1) * LANES]
        p_pair.append(acc[:, :qb_n] + acc[:, qb_n:])
    imp = _dot(ovt_ref[...], jnp.concatenate(p_pair, axis=1), HI)
    blk = _iota2((n_sel, NSA_QPAIR * qb_n), 0)
    cur = pair * NSA_QPAIR + _iota2((n_sel, NSA_QPAIR * qb_n), 1) // qb_n
    valid = blk <= cur
    forced = (blk == 0) | (blk == cur) | (blk == cur - 1)
    score = jnp.where(valid, jnp.where(forced, jnp.inf, imp), -jnp.inf)
    score_ref[...] = score

    e_d, _, m_s = softmax_t(s_diag)
    acc_ref[...] = _dot(vst_ref[0, 0, :, pl.ds(k0, NSA_QPAIR * SEL_BLOCK)], e_d.astype(bf16))

    @pl.when(w0 < WINDOW)
    def _():
        is_pad = _iota2((WINDOW, 1), 0) < WINDOW - w0
        sw_ref[pl.ds(0, WINDOW), :] = jnp.where(is_pad, -jnp.inf, sw_ref[pl.ds(0, WINDOW), :])

    def rank_step(jp, cnt):
        other = score_ref[pl.ds(jp, 1), :]
        ahead = (other > score) | ((other == score) & (blk > jp))
        return cnt + ahead.astype(jnp.int32)

    def rank_step2(jj, cnt):
        return rank_step(2 * jj + 1, rank_step(2 * jj, cnt))

    n_cand = pair * NSA_QPAIR + NSA_QPAIR
    cnt = lax.fori_loop(0, jnp.where(n_cand > SEL_TOPN, n_cand // 2, 0), rank_step2,
                        jnp.zeros((n_sel, NSA_QPAIR * qb_n), jnp.int32))
    bias = jnp.where((cnt < SEL_TOPN) & (blk < diag0), 0.0, NSA_MASKED).astype(bf16)
    bias = _dot(bias, rep_ref[...]).astype(bf16)
    if n_sel < NSA_HD:
        bias = jnp.concatenate([bias, jnp.zeros((NSA_HD - n_sel, lanes), bf16)], axis=0)
    qx = jnp.concatenate([qt, bias], axis=0)

    def score_tile(kt):
        kk0 = pl.multiple_of(kt * key_tile, key_tile)
        return _dot(ks_ref[0, 0, pl.ds(kk0, key_tile), :], qx)

    s_ref[0] = score_tile(0)

    e_w, _, _ = softmax_t(sw_ref[...])
    o_w = _dot(vwt_ref[0, 0, :, pl.ds(w0, span)], e_w.astype(bf16))

    def half_step(kt, slot, carry):
        m_old = carry
        s_ref[1 - slot] = score_tile(jnp.minimum(kt + 1, last_tile))
        sm = s_ref[slot]
        mx = jnp.maximum(m_old, jnp.max(sm, axis=0, keepdims=True))
        alpha = jnp.exp2(m_old - mx)
        p = jnp.exp2(sm - mx)
        kk0 = pl.multiple_of(kt * key_tile, key_tile)
        acc_ref[...] = alpha * acc_ref[...] + _dot(vst_ref[0, 0, :, pl.ds(kk0, key_tile)], p.astype(bf16))
        return mx

    def pair_step(pi, carry):
        return half_step(2 * pi + 1, 1, half_step(2 * pi, 0, carry))

    lax.fori_loop(0, (n_tiles + 1) // 2, pair_step, m_s)

    l_s = acc_ref[pl.ds(NSA_HD, 1), :]
    l_w = o_w[NSA_HD:NSA_HD + 1]
    o_t = (gates[0] * o_c + (gates[1] * (1.0 / l_s)) * acc_ref[pl.ds(0, NSA_HD), :]
           + (gates[2] * (1.0 / l_w)) * o_w[:NSA_HD])
    o_b = o_t.astype(bf16)
    q_idx = _iota2((qb_n, LANES), 0)
    l_idx = _iota2((qb_n, LANES), 1)
    pick = [(l_idx == q_idx + r * qb_n).astype(bf16) for r in range(LANES // qb_n)]
    for a in range(NSA_QPAIR):
        for c in range(blk_lanes // LANES):
            piece = o_b[:, a * blk_lanes + c * LANES:a * blk_lanes + (c + 1) * LANES]
            for r in range(LANES // qb_n):
                h = c * (LANES // qb_n) + r
                o_ref[0, a * qb_n:(a + 1) * qb_n, h * NSA_HD:(h + 1) * NSA_HD] = (
                    _dot_nt(pick[r], piece).astype(bf16))


def nsa_attention(qt, kc, vct, ks, vst, kw, vwt, f32_tail, overlap_t, key_tile=256):
    b, g, n_q, d, blk_lanes = qt.shape
    t = n_q * NSA_QBLOCK
    key_tile = min(key_tile, t)
    assert (t // key_tile) % 2 == 0, "an odd tile count borrows the (fully unselected) tile after the last one"
    assert NSA_QPAIR == 2, "the diagonal tile is masked by causality alone only for a pair of query blocks"
    n_cmp = kc.shape[2]
    n_sel = overlap_t.shape[0]
    lanes = NSA_QPAIR * blk_lanes
    q_rows = NSA_QPAIR * NSA_QBLOCK
    off = (jnp.arange(lanes) // blk_lanes) * NSA_QBLOCK + jnp.arange(lanes) % NSA_QBLOCK
    r_d = jnp.arange(NSA_QPAIR * SEL_BLOCK)[:, None]
    diag_bias = jnp.where(r_d <= off[None, :], 0.0, -jnp.inf).astype(f32)
    r_w = jnp.arange(WINDOW + q_rows)[:, None]
    win_bias = jnp.where((r_w > off[None, :]) & (r_w <= WINDOW + off[None, :]), 0.0, -jnp.inf).astype(f32)
    rep = (jnp.arange(q_rows)[:, None] == off[None, :]).astype(bf16)
    seq = pl.BlockSpec((1, 1, t, ks.shape[3]), lambda bi, gi, qi: (bi, gi, 0, 0))
    seq_t = pl.BlockSpec((1, 1, NSA_VT_ROWS, t), lambda bi, gi, qi: (bi, gi, 0, 0))
    const = lambda a: pl.BlockSpec(a.shape, lambda bi, gi, qi: (0, 0))
    return pl.pallas_call(
        functools.partial(_nsa_attn_body, key_tile=key_tile),
        grid=(b, g, n_q // NSA_QPAIR),
        in_specs=[pl.BlockSpec((1, 1, NSA_QPAIR, d, blk_lanes), lambda bi, gi, qi: (bi, gi, qi, 0, 0)),
                  pl.BlockSpec((1, 1, n_cmp, d), lambda bi, gi, qi: (bi, gi, 0, 0)),
                  pl.BlockSpec((1, 1, d, n_cmp), lambda bi, gi, qi: (bi, gi, 0, 0)),
                  seq, seq_t,
                  pl.BlockSpec((1, 1, t + WINDOW, d), lambda bi, gi, qi: (bi, gi, 0, 0)),
                  pl.BlockSpec((1, 1, NSA_VT_ROWS, t + WINDOW), lambda bi, gi, qi: (bi, gi, 0, 0)),
                  pl.BlockSpec((1, q_rows, LANES), lambda bi, gi, qi: (bi, qi, (NSA_GATE0 - NSA_TAIL0) // LANES + gi)),
                  const(overlap_t), const(diag_bias), const(win_bias), const(rep)],
        out_specs=pl.BlockSpec((1, q_rows, NSA_HPG * d), lambda bi, gi, qi: (bi, qi, gi)),
        out_shape=jax.ShapeDtypeStruct((b, t, g * NSA_HPG * d), bf16),
        scratch_shapes=[pltpu.VMEM((n_sel, q_rows), f32),
                        pltpu.VMEM((2, key_tile, lanes), f32), pltpu.VMEM((NSA_VT_ROWS, lanes), f32),
                        pltpu.VMEM((WINDOW + q_rows, lanes), f32)],
        compiler_params=_cparams(("parallel", "parallel", "arbitrary")),
        name="nsa_attn",
    )(qt, kc, vct, ks, vst, kw, vwt, f32_tail, overlap_t, diag_bias, win_bias, rep)


def _place(cols, total, pieces):
    out = jnp.zeros((cols, total), f32)
    for start, mat in pieces:
        out = lax.dynamic_update_slice(out, mat.astype(f32), (0, start))
    return out


def _layer0(h, b, t, g_pre, w_in, w_out, conv, a_log, dt_bias, gnorm, mu, w0, w2, a0, a2, g2, k_k, k_a,
            r_k, ln_w, ln_b):
    gdn_w = 4 * GDN_HEADS * GDN_D
    w_pad = _place(D_MODEL, AB_COLS, [
        (0, w_in[:, :gdn_w]),
        (AB_BA0, w_in[:, gdn_w:gdn_w + 2 * GDN_HEADS]),
        (AB_RWKV0, w_in[:, gdn_w + 2 * GDN_HEADS:gdn_w + 2 * GDN_HEADS + 3 * RWKV_W]),
        (AB_LORA0, w_in[:, gdn_w + 2 * GDN_HEADS + 3 * RWKV_W:]),
    ]).astype(bf16)
    f16, f32_tail = [a.reshape(b, t, -1) for a in norm_matmul(h, g_pre, w_pad)]
    arow = jnp.zeros((1, LANES), f32).at[0, GDN_HEADS:2 * GDN_HEADS].set(a_log)
    dtrow = jnp.zeros((1, LANES), f32).at[0, GDN_HEADS:2 * GDN_HEADS].set(dt_bias)
    conv3 = conv.reshape(GDN_CONV, 3, GDN_HEADS * GDN_D).transpose(1, 0, 2)
    o_a = gdn_mixer(f16, f32_tail, conv3, arow, dtrow, gnorm.reshape(1, GDN_D))
    o_b = rwkv_mixer(f16, f32_tail, mu, w0, w2, a0, a2, g2, k_k, k_a, r_k, ln_w, ln_b)
    m = b * t
    n_a = GDN_HEADS * GDN_D
    return [o_a.reshape(m, n_a), o_b.reshape(m, RWKV_W)], [w_out[:n_a].astype(bf16), w_out[n_a:].astype(bf16)]


def _layer1(h, b, t, positions, g_pre, w_in, w_out, pe_k, w1_k, w2_k, pe_v, w1_v, w2_v):
    qw = NSA_HEADS * NSA_HD
    kvw = 6 * NSA_GROUPS * NSA_HD
    gates = w_in[:, qw + kvw:].reshape(D_MODEL, NSA_GROUPS, NSA_HPG * 3)
    w_pad = _place(D_MODEL, NSA_COLS, [(0, w_in[:, :qw + kvw])]
                   + [(NSA_GATE0 + gi * LANES, gates[:, gi]) for gi in range(NSA_GROUPS)]).astype(bf16)
    f16, f32_tail = [a.reshape(b, t, -1) for a in norm_matmul(h, g_pre, w_pad)]
    inv_freq = ROPE_THETA ** (-jnp.arange(ROPE_HALF, dtype=f32) * (2.0 / ROPE_DIM))
    lane = jnp.arange(LANES)
    freq_row = jnp.where(lane % NSA_HD < ROPE_DIM, inv_freq[lane % ROPE_HALF], 0.0).reshape(1, LANES).astype(f32)
    posf = positions.astype(f32)
    qt, kc, vc, ks, vst, kw, vwt = nsa_prep(f16, f32_tail, posf.reshape(b, t, 1), freq_row,
                                            inv_freq.reshape(ROPE_HALF, 1))
    nrow = t // CMP_STRIDE
    cpos = jnp.concatenate([posf[:, CMP_LEN - 1::CMP_STRIDE], posf[:, -1:]], axis=1).reshape(b, nrow, 1)
    flat = lambda a: a.reshape(b, NSA_GROUPS, nrow, CMP_STRIDE * NSA_HD)
    kcc, vcct = nsa_compress(flat(kc), flat(vc), pe_k.reshape(1, -1), w1_k.astype(bf16), w2_k.astype(bf16),
                             pe_v.reshape(1, -1), w1_v.astype(bf16), w2_v.T.astype(bf16), cpos, freq_row)
    n_sel = t // SEL_BLOCK
    c_start = jnp.arange(nrow) * CMP_STRIDE
    s_start = jnp.arange(n_sel) * SEL_BLOCK
    overlap_t = jnp.clip(jnp.minimum(c_start[None, :] + CMP_LEN, s_start[:, None] + SEL_BLOCK)
                         - jnp.maximum(c_start[None, :], s_start[:, None]), 0, None).astype(f32) / CMP_LEN
    kw_pad = jnp.pad(kw, ((0, 0), (0, 0), (WINDOW, 0), (0, 0)))
    vwt_pad = jnp.pad(vwt, ((0, 0), (0, 0), (0, 0), (WINDOW, 0)))
    o = nsa_attention(qt, kcc, vcct, ks, vst, kw_pad, vwt_pad, f32_tail, overlap_t)
    return [o.reshape(b * t, qw)], [w_out.astype(bf16)]


def kernel(x, positions, norm_mix_pre, norm_mix_post, norm_ffn_pre, norm_ffn_post, w_ffn_up, w_ffn_down, ab_w_in,
           ab_w_out, gdn_conv, gdn_a_log, gdn_dt_bias, gdn_norm, rwkv_mu, rwkv_w0, rwkv_w2, rwkv_a0, rwkv_a2,
           rwkv_g2, rwkv_k_k, rwkv_k_a, rwkv_r_k, rwkv_ln_w, rwkv_ln_b, nsa_w_in, nsa_w_out, nsa_pe_k, nsa_w1_k,
           nsa_w2_k, nsa_pe_v, nsa_w1_v, nsa_w2_v):
    b, t, d = x.shape
    h = x.reshape(b * t, d)
    mix = _layer0(h, b, t, norm_mix_pre[0], ab_w_in[0], ab_w_out[0], gdn_conv[0], gdn_a_log[0],
                  gdn_dt_bias[0], gdn_norm[0], rwkv_mu[0], rwkv_w0[0], rwkv_w2[0], rwkv_a0[0], rwkv_a2[0],
                  rwkv_g2[0], rwkv_k_k[0], rwkv_k_a[0], rwkv_r_k[0].reshape(-1), rwkv_ln_w[0], rwkv_ln_b[0])
    h = out_ffn(*mix, h, norm_mix_post[0], norm_ffn_pre[0], w_ffn_up[0].astype(bf16), w_ffn_down[0].astype(bf16),
                norm_ffn_post[0])
    mix = _layer1(h, b, t, positions, norm_mix_pre[1], nsa_w_in[0], nsa_w_out[0], nsa_pe_k[0],
                  nsa_w1_k[0], nsa_w2_k[0], nsa_pe_v[0], nsa_w1_v[0], nsa_w2_v[0])
    h = out_ffn(*mix, h, norm_mix_post[1], norm_ffn_pre[1], w_ffn_up[1].astype(bf16), w_ffn_down[1].astype(bf16),
                norm_ffn_post[1])
    return h.reshape(b, t, d)
```

```python
import functools
import math

import jax
import jax.numpy as jnp
from jax import lax
from jax.experimental import pallas as pl
from jax.experimental.pallas import tpu as pltpu

f32 = jnp.float32
bf16 = jnp.bfloat16
HI = lax.Precision.HIGHEST

V7X_VMEM_LIMIT_BYTES = 56 * 1024 * 1024
LANES = 128
SUBLANES = 8

D_MODEL = 1024
DENSE_ROW_BLOCK = 256
DENSE_F32_TAIL = 1024
NORM_EPS = 1e-6
GDN_HEADS = 4
GDN_D = 128
GDN_CONV = 4
GDN_CHUNK = 128
GDN_CHUNK_GROUP = 4
RWKV_HEADS = 8
RWKV_N = 64
RWKV_W = RWKV_HEADS * RWKV_N
RWKV_CHUNK = 64
RWKV_CHUNK_GROUP = 4
RWKV_LN_EPS = 64e-5
RWKV_DECAY_LORA = 64
RWKV_LR_LORA = 64
RWKV_GATE_LORA = 128
NSA_HEADS = 16
NSA_GROUPS = 2
NSA_HPG = NSA_HEADS // NSA_GROUPS
NSA_HD = 64
CMP_LEN = 32
CMP_STRIDE = 16
SEL_BLOCK = 64
SEL_TOPN = 16
WINDOW = 512
NSA_QBLOCK = 64
NSA_QPAIR = 2
NSA_VT_ROWS = 80
NSA_MASKED = -(2.0 ** 126)
ROPE_THETA = 500000.0
ROPE_DIM = NSA_HD // 4
ROPE_HALF = ROPE_DIM // 2

AB_COLS = 4096
AB_RWKV0 = 2048
AB_LORA0 = 3584
AB_BA0 = 3840
AB_TAIL0 = AB_COLS - DENSE_F32_TAIL
NSA_COLS = 2048
NSA_KV0 = 1024
NSA_GATE0 = 1792
NSA_TAIL0 = NSA_COLS - DENSE_F32_TAIL


def _cparams(sem):
    return pltpu.CompilerParams(dimension_semantics=sem, vmem_limit_bytes=V7X_VMEM_LIMIT_BYTES)


def _rms(x, g):
    return x * lax.rsqrt(jnp.mean(x * x, axis=-1, keepdims=True) + NORM_EPS) * g


def _dot(a, b, precision=None):
    return jnp.dot(a, b, precision=precision, preferred_element_type=f32)


def _dot_nt(a, b, precision=None):
    return lax.dot_general(a, b, (((1,), (1,)), ((), ())), precision=precision, preferred_element_type=f32)


def _dot_tn(a, b, precision=None):
    return lax.dot_general(a, b, (((0,), (0,)), ((), ())), precision=precision, preferred_element_type=f32)


def _bdot(a, b):
    return _dot(a.astype(bf16), b.astype(bf16))


def _bdot_nt(a, b):
    return _dot_nt(a.astype(bf16), b.astype(bf16))


def _bdot_tn(a, b):
    return _dot_tn(a.astype(bf16), b.astype(bf16))


def _dot01(m01, x):
    m = m01.astype(bf16)
    hi = x.astype(bf16)
    rest = x - hi.astype(f32)
    mid = rest.astype(bf16)
    lo = (rest - mid.astype(f32)).astype(bf16)
    return _dot(m, hi) + _dot(m, mid) + _dot(m, lo)


def _iota2(shape, axis):
    return lax.broadcasted_iota(jnp.int32, shape, axis)


def _norm_matmul_body(x_ref, g_ref, w_ref, o16_ref, o32_ref, u_ref, *, last):
    j = pl.program_id(1)
    tm = x_ref.shape[0]
    row_blocks = [pl.ds(r0, min(tm, DENSE_ROW_BLOCK)) for r0 in range(0, tm, min(tm, DENSE_ROW_BLOCK))]

    @pl.when(j == 0)
    def _():
        for rows in row_blocks:
            u_ref[rows, :] = _rms(x_ref[rows, :], g_ref[...]).astype(bf16)
            o16_ref[rows, :] = _dot(u_ref[rows, :], w_ref[...]).astype(bf16)

    @pl.when((j > 0) & (j < last))
    def _():
        o16_ref[...] = _dot(u_ref[...], w_ref[...]).astype(bf16)

    @pl.when(j == last)
    def _():
        o32_ref[...] = _dot(u_ref[...], w_ref[...])


def norm_matmul(x, g, w, tm=2048, tn=DENSE_F32_TAIL):
    m, d = x.shape
    n = w.shape[1]
    tm = min(tm, m)
    n_col = n // tn
    assert n % tn == 0 and n_col >= 2
    return pl.pallas_call(
        functools.partial(_norm_matmul_body, last=n_col - 1),
        grid=(m // tm, n_col),
        in_specs=[pl.BlockSpec((tm, d), lambda i, j: (i, 0)),
                  pl.BlockSpec((1, d), lambda i, j: (0, 0)),
                  pl.BlockSpec((d, tn), lambda i, j: (0, j))],
        out_specs=[pl.BlockSpec((tm, tn), lambda i, j: (i, jnp.minimum(j, n_col - 2))),
                   pl.BlockSpec((tm, tn), lambda i, j: (i, 0))],
        out_shape=[jax.ShapeDtypeStruct((m, n - tn), bf16), jax.ShapeDtypeStruct((m, tn), f32)],
        scratch_shapes=[pltpu.VMEM((tm, d), bf16)],
        compiler_params=_cparams(("parallel", "arbitrary")),
        name="norm_matmul",
    )(x, g.reshape(1, d), w)


def _out_ffn_body(*refs, n_parts, last):
    a_refs = refs[:n_parts]
    w_refs = refs[n_parts:2 * n_parts]
    h_ref, gm_ref, g1_ref, wup_ref, wdn_ref, g2_ref, o_ref, h1_ref, u_ref, acc_ref = refs[2 * n_parts:]
    k = pl.program_id(1)
    tm = h_ref.shape[0]
    row_blocks = [pl.ds(r0, min(tm, DENSE_ROW_BLOCK)) for r0 in range(0, tm, min(tm, DENSE_ROW_BLOCK))]

    def mlp_step(rows):
        a = _dot(u_ref[rows, :], wup_ref[...])
        a = jnp.square(jnp.maximum(a, 0.0))
        return _dot(a.astype(bf16), wdn_ref[...])

    def out_proj(rows):
        y = _dot(a_refs[0][rows, :].astype(bf16), w_refs[0][...])
        for a_ref, w_ref in zip(a_refs[1:], w_refs[1:]):
            y = y + _dot(a_ref[rows, :].astype(bf16), w_ref[...])
        return y

    @pl.when(k == 0)
    def _():
        y = out_proj(row_blocks[0])
        for idx, rows in enumerate(row_blocks):
            h1 = h_ref[rows, :] + _rms(y, gm_ref[...])
            h1_ref[rows, :] = h1
            u_ref[rows, :] = _rms(h1, g1_ref[...]).astype(bf16)
            if idx + 1 < len(row_blocks):
                y = out_proj(row_blocks[idx + 1])
            acc_ref[rows, :] = mlp_step(rows)

    @pl.when((k > 0) & (k < last))
    def _():
        for rows in row_blocks:
            acc_ref[rows, :] += mlp_step(rows)

    @pl.when(k == last)
    def _():
        for rows in row_blocks:
            o_ref[rows, :] = h1_ref[rows, :] + _rms(acc_ref[rows, :] + mlp_step(rows), g2_ref[...])


def out_ffn(parts, w_parts, h, g_mix, g1, wup, wdn, g2, tm=1024, tf=1024):
    m, d = h.shape
    ff = wup.shape[1]
    tm = min(tm, m)
    assert ff // tf >= 2, "the first and the last reduction step are distinct branches"
    n_parts = len(parts)
    row = pl.BlockSpec((1, d), lambda i, k: (0, 0))
    in_specs = ([pl.BlockSpec((tm, p.shape[1]), lambda i, k: (i, 0)) for p in parts]
                + [pl.BlockSpec(w.shape, lambda i, k: (0, 0)) for w in w_parts]
                + [pl.BlockSpec((tm, d), lambda i, k: (i, 0)), row, row,
                   pl.BlockSpec((d, tf), lambda i, k: (0, k)),
                   pl.BlockSpec((tf, d), lambda i, k: (k, 0)), row])
    return pl.pallas_call(
        functools.partial(_out_ffn_body, n_parts=n_parts, last=ff // tf - 1),
        grid=(m // tm, ff // tf),
        in_specs=in_specs,
        out_specs=pl.BlockSpec((tm, d), lambda i, k: (i, 0)),
        out_shape=jax.ShapeDtypeStruct((m, d), f32),
        scratch_shapes=[pltpu.VMEM((tm, d), f32), pltpu.VMEM((tm, d), bf16), pltpu.VMEM((tm, d), f32)],
        compiler_params=_cparams(("parallel", "arbitrary")),
        name="out_ffn",
    )(*parts, *w_parts, h, g_mix.reshape(1, d), g1.reshape(1, d), wup, wdn, g2.reshape(1, d))


def _neumann_inverses(n_mats, size):
    eye = (_iota2((size, size), 0) == _iota2((size, size), 1)).astype(f32)
    ts = [eye + n for n in n_mats]
    ps = list(n_mats)
    for _ in range(int(math.log2(size)) - 1):
        ps = [_bdot(p, p) for p in ps]
        ts = [t + _bdot(t, p) for t, p in zip(ts, ps)]
    return ts


def _gdn_body(q_ref, k_ref, v_ref, z_ref, ba_ref, cw_ref, arow_ref, dtrow_ref, nw_ref, o_ref,
              xp_ref, qkv_ref, s_ref, *, tt):
    c = GDN_CHUNK
    d = GDN_D
    heads = range(GDN_HEADS)
    width = GDN_HEADS * d

    @pl.when(pl.program_id(1) == 0)
    def _():
        xp_ref[:, pl.ds(0, SUBLANES), :] = jnp.zeros((3, SUBLANES, width), f32)
        s_ref[...] = jnp.zeros_like(s_ref)

    conv_rows = min(tt, 64)
    for idx, ref in enumerate((q_ref, k_ref, v_ref)):
        xp_ref[idx, pl.ds(SUBLANES, tt), :] = ref[0].astype(f32)
        for h in heads:
            cols = slice(h * d, (h + 1) * d)
            w = cw_ref[idx, :, cols]
            for r0 in range(0, tt, conv_rows):
                y = xp_ref[idx, pl.ds(SUBLANES + r0, conv_rows), cols] * w[GDN_CONV - 1:GDN_CONV, :]
                for j in range(GDN_CONV - 1):
                    y = y + xp_ref[idx, pl.ds(SUBLANES - (GDN_CONV - 1) + j + r0, conv_rows), cols] * w[j:j + 1, :]
                y = y * jax.nn.sigmoid(y)
                if idx < 2:
                    y = y * lax.rsqrt(jnp.sum(y * y, axis=-1, keepdims=True) + 1e-6)
                if idx == 0:
                    y = y * (d ** -0.5)
                qkv_ref[idx, pl.ds(r0, conv_rows), cols] = y
        xp_ref[idx, pl.ds(0, SUBLANES), :] = xp_ref[idx, pl.ds(tt, SUBLANES), :]

    row = _iota2((c, c), 0)
    col = _iota2((c, c), 1)
    tril = row >= col
    strict = row > col
    eye = row == col
    cum_l = tril.astype(f32)
    last_row = _iota2((c, 1), 0) == c - 1

    group = GDN_CHUNK_GROUP
    nh = GDN_HEADS

    def chunk_group(gi, carry):
        rows = [pl.ds(pl.multiple_of((gi * group + j) * c, c), c) for j in range(group)]
        sig, gcum = [], []
        for r in rows:
            ba = ba_ref[0, r, :]
            sig.append(jax.nn.sigmoid(ba))
            gcum.append(_dot01(cum_l, -jnp.exp(arow_ref[...]) * jax.nn.softplus(ba + dtrow_ref[...])))
        units = lambda f: [f(j, h) for j in range(group) for h in heads]
        qn = units(lambda j, h: qkv_ref[0, rows[j], h * d:(h + 1) * d])
        kn = units(lambda j, h: qkv_ref[1, rows[j], h * d:(h + 1) * d])
        vv = units(lambda j, h: qkv_ref[2, rows[j], h * d:(h + 1) * d])
        beta = units(lambda j, h: sig[j][:, h:h + 1])
        gc = units(lambda j, h: gcum[j][:, nh + h:nh + h + 1])
        gc_row = [jnp.sum(jnp.where(eye, jnp.broadcast_to(g, (c, c)), 0.0), axis=0, keepdims=True) for g in gc]
        gc_last = [jnp.sum(jnp.where(last_row, g, 0.0), axis=0, keepdims=True) for g in gc]
        decay = [jnp.exp(jnp.where(tril, g - gr, -jnp.inf)) for g, gr in zip(gc, gc_row)]
        knb = [k.astype(bf16) for k in kn]
        kk = [_dot_nt(k, k) for k in knb]
        qk = [_dot_nt(q.astype(bf16), k) for q, k in zip(qn, knb)]
        t_inv = _neumann_inverses([-jnp.where(strict, b * x * dc, 0.0) for b, x, dc in zip(beta, kk, decay)], c)
        egc = [jnp.exp(g) for g in gc]
        u = [_bdot(t, v * b) for t, v, b in zip(t_inv, vv, beta)]
        w = [_bdot(t, k * (b * e)).astype(bf16) for t, k, b, e in zip(t_inv, kn, beta, egc)]
        intra = [jnp.where(tril, x * dc, 0.0).astype(bf16) for x, dc in zip(qk, decay)]
        q_g = [(q * e).astype(bf16) for q, e in zip(qn, egc)]
        k_g = [(k * jnp.exp(gl - g)).astype(bf16) for k, gl, g in zip(kn, gc_last, gc)]
        state = [s_ref[h] for h in heads]
        for j in range(group):
            sl = slice(j * nh, (j + 1) * nh)
            sb = [x.astype(bf16) for x in state]
            v_new = [(x - _dot(y, z)).astype(bf16) for x, y, z in zip(u[sl], w[sl], sb)]
            o = [_dot(q, z) + _dot(a, vn) for q, z, a, vn in zip(q_g[sl], sb, intra[sl], v_new)]
            state = [s * jnp.exp(gl) + _dot_tn(k, vn) for s, gl, k, vn in zip(state, gc_last[sl], k_g[sl], v_new)]
            for h in heads:
                z = z_ref[0, rows[j], h * d:(h + 1) * d].astype(f32)
                o_ref[0, rows[j], h * d:(h + 1) * d] = _rms(o[h], nw_ref[...]) * (z * jax.nn.sigmoid(z))
        for h in heads:
            s_ref[h] = state[h]
        return carry

    lax.fori_loop(0, tt // (c * group), chunk_group, 0)


def gdn_mixer(f16, f32_tail, conv_w, arow, dtrow, norm_w, tt=512):
    b, t, _ = f16.shape
    tt = min(tt, t)
    assert t % tt == 0 and tt % (GDN_CHUNK * GDN_CHUNK_GROUP) == 0
    width = GDN_HEADS * GDN_D
    col = lambda j: pl.BlockSpec((1, tt, width), lambda bi, ti, j=j: (bi, ti, j))
    return pl.pallas_call(
        functools.partial(_gdn_body, tt=tt),
        grid=(b, t // tt),
        in_specs=[col(0), col(1), col(2), col(3),
                  pl.BlockSpec((1, tt, LANES), lambda bi, ti: (bi, ti, (AB_BA0 - AB_TAIL0) // LANES)),
                  pl.BlockSpec((3, GDN_CONV, width), lambda bi, ti: (0, 0, 0)),
                  pl.BlockSpec((1, LANES), lambda bi, ti: (0, 0)),
                  pl.BlockSpec((1, LANES), lambda bi, ti: (0, 0)),
                  pl.BlockSpec((1, GDN_D), lambda bi, ti: (0, 0))],
        out_specs=pl.BlockSpec((1, tt, width), lambda bi, ti: (bi, ti, 0)),
        out_shape=jax.ShapeDtypeStruct((b, t, width), f32),
        scratch_shapes=[pltpu.VMEM((3, tt + SUBLANES, width), f32),
                        pltpu.VMEM((3, tt, width), f32),
                        pltpu.VMEM((GDN_HEADS, GDN_D, GDN_D), f32)],
        compiler_params=_cparams(("parallel", "arbitrary")),
        name="gdn",
    )(f16, f16, f16, f16, f32_tail, conv_w, arow, dtrow, norm_w)


def _rwkv_body(r_ref, k_ref, v_ref, l_ref, mu_ref, w0_ref, w2_ref, a0_ref, a2_ref, g2_ref, kkw_ref, ka_ref,
               rk_ref, lnw_ref, lnb_ref, o_ref, xp_ref, r_s, lw_s, k2_s, v_s, kk_s, a_s, g_s, s_ref, *, tt):
    c = RWKV_CHUNK
    n = RWKV_N
    w = RWKV_W

    @pl.when(pl.program_id(1) == 0)
    def _():
        xp_ref[pl.ds(0, SUBLANES), :] = jnp.zeros((SUBLANES, xp_ref.shape[1]), f32)
        s_ref[...] = jnp.zeros_like(s_ref)

    xp_ref[pl.ds(SUBLANES, tt), 0:w] = r_ref[0].astype(f32)
    xp_ref[pl.ds(SUBLANES, tt), w:2 * w] = k_ref[0].astype(f32)
    xp_ref[pl.ds(SUBLANES, tt), 2 * w:3 * w] = v_ref[0]
    xp_ref[pl.ds(SUBLANES, tt), 3 * w:] = l_ref[0]
    x = xp_ref[pl.ds(SUBLANES, tt), :]
    x = x + (xp_ref[pl.ds(SUBLANES - 1, tt), :] - x) * mu_ref[...]
    xp_ref[pl.ds(0, SUBLANES), :] = xp_ref[pl.ds(tt, SUBLANES), :]

    k = x[:, w:2 * w]
    lora0 = 3 * w
    wd = x[:, lora0:lora0 + RWKV_DECAY_LORA]
    ad = x[:, lora0 + RWKV_DECAY_LORA:lora0 + RWKV_DECAY_LORA + RWKV_LR_LORA]
    gd = x[:, lora0 + RWKV_DECAY_LORA + RWKV_LR_LORA:]
    w_log = -jax.nn.softplus(-(w0_ref[...] + _dot(jnp.tanh(wd), w2_ref[...], HI))) - 0.5
    lr = jax.nn.sigmoid(a0_ref[...] + _dot(ad, a2_ref[...], HI))
    r_s[...] = x[:, 0:w]
    lw_s[...] = -jnp.exp(w_log)
    k2_s[...] = k * (1.0 + (lr - 1.0) * ka_ref[...])
    v_s[...] = x[:, 2 * w:3 * w]
    kk_s[...] = k * kkw_ref[...]
    a_s[...] = lr
    g_s[...] = _dot(jax.nn.sigmoid(gd), g2_ref[...], HI)

    row = _iota2((c, c), 0)
    col = _iota2((c, c), 1)
    tril = row >= col
    strict = row > col
    cum_l = tril.astype(f32)
    last = _iota2((c, 1), 0) == c - 1

    heads = range(RWKV_HEADS)
    per_head = lambda x: [x[:, h * n:(h + 1) * n] for h in heads]

    group = RWKV_CHUNK_GROUP

    def chunk_group(gi, carry):
        rows = [pl.ds(pl.multiple_of((gi * group + j) * c, c), c) for j in range(group)]
        units = lambda f: [x for r in rows for x in f(r)]
        stack = lambda xs, ys: [jnp.concatenate([x, y], axis=0) for x, y in zip(xs, ys)]
        lw_all = [lw_s[r, :] for r in rows]
        p_all = [_dot01(cum_l, x) for x in lw_all]
        em_all = [jnp.exp(-p) for p in p_all]
        rh = units(lambda r: per_head(r_s[r, :]))
        k2 = units(lambda r: per_head(k2_s[r, :]))
        vh = units(lambda r: per_head(v_s[r, :]))
        r_t = [x for r, p in zip(rows, p_all) for x in per_head(r_s[r, :] * jnp.exp(p))]
        k_t = [x for r, e in zip(rows, em_all) for x in per_head(k2_s[r, :] * e)]
        e_prev = [x for p, lw in zip(p_all, lw_all) for x in per_head(jnp.exp(p - lw))]
        lr_em = [x for r, e in zip(rows, em_all) for x in per_head(a_s[r, :] * e)]
        kk = [x * lax.rsqrt(jnp.sum(x * x, axis=-1, keepdims=True) + 1e-6)
              for x in units(lambda r: per_head(kk_s[r, :]))]
        a_t = [-x * e for x, e in zip(kk, e_prev)]
        b_t = [x * e for x, e in zip(kk, lr_em)]
        ar = [x.astype(bf16) for x in stack(a_t, r_t)]
        bk = [x.astype(bf16) for x in stack(b_t, k_t)]
        ar_b = [_dot_nt(x, y[:c]) for x, y in zip(ar, bk)]
        ar_k = [_dot_nt(x, y[c:]) for x, y in zip(ar, bk)]
        m_ab = [jnp.where(strict, x[:c], 0.0) for x in ar_b]
        a_rb = [jnp.where(tril, x[c:], 0.0) for x in ar_b]
        mk_rk = [jnp.where(jnp.concatenate([strict, tril], axis=0), x, 0.0) for x in ar_k]
        eye = (row == col).astype(f32)
        t_inv = [eye + x for x in m_ab]
        pw = [_bdot(x, x) for x in m_ab]
        for _ in range(int(math.log2(c)) - 1):
            z = [_bdot(x, p) for x, p in zip(stack(pw, t_inv), pw)]
            pw = [x[:c] for x in z]
            t_inv = [t + x[c:] for t, x in zip(t_inv, z)]
        mv_yv = [_bdot(x, v) for x, v in zip(mk_rk, vh)]
        w1 = [_bdot(t, a) for t, a in zip(t_inv, a_t)]
        u2 = [_bdot(t, x[:c]) for t, x in zip(t_inv, mv_yv)]
        y_v = [x[c:] for x in mv_yv]
        w1r = stack(w1, r_t)
        decay_last = [x for p in p_all
                      for x in per_head(jnp.exp(jnp.sum(jnp.where(last, p, 0.0), axis=0, keepdims=True)))]
        g_h = units(lambda r: per_head(g_s[r, :]))
        bonus = [jnp.sum(r * k * rk_ref[:, (i % RWKV_HEADS) * n:(i % RWKV_HEADS + 1) * n], axis=-1, keepdims=True) * v
                 for i, (r, k, v) in enumerate(zip(rh, k2, vh))]
        state = [s_ref[h] for h in heads]
        for j in range(group):
            sl_u = slice(j * RWKV_HEADS, (j + 1) * RWKV_HEADS)
            ws = [_bdot_nt(x, s) for x, s in zip(w1r[sl_u], state)]
            u = [x[:c] + y for x, y in zip(ws, u2[sl_u])]
            y = [x[c:] + _bdot(a, z) + yv for x, a, z, yv in zip(ws, a_rb[sl_u], u, y_v[sl_u])]
            ds = [_dot_tn(x.astype(bf16), y) for x, y in zip(stack(u, vh[sl_u]), bk[sl_u])]
            state = [(s + d) * dl for s, d, dl in zip(state, ds, decay_last[sl_u])]
            for h in heads:
                sl = slice(h * n, (h + 1) * n)
                mean = jnp.mean(y[h], axis=-1, keepdims=True)
                yc = y[h] - mean
                var = jnp.mean(yc * yc, axis=-1, keepdims=True)
                yn = yc * lax.rsqrt(var + RWKV_LN_EPS) * lnw_ref[:, sl] + lnb_ref[:, sl]
                o_ref[0, rows[j], sl] = (yn + bonus[j * RWKV_HEADS + h]) * g_h[j * RWKV_HEADS + h]
        for h in heads:
            s_ref[h] = state[h]
        return carry

    lax.fori_loop(0, tt // (c * group), chunk_group, 0)


def rwkv_mixer(f16, f32_tail, mu, w0, w2, a0, a2, g2, k_k, k_a, r_k, ln_w, ln_b, tt=256):
    b, t, _ = f16.shape
    tt = min(tt, t)
    assert t % tt == 0 and tt % (RWKV_CHUNK * RWKV_CHUNK_GROUP) == 0
    w = RWKV_W
    lora = RWKV_DECAY_LORA + RWKV_LR_LORA + RWKV_GATE_LORA
    wide = 3 * w + lora
    row = lambda n: pl.BlockSpec((1, n), lambda bi, ti: (0, 0))
    full = lambda a: pl.BlockSpec(a.shape, lambda bi, ti: (0, 0))
    col = lambda j: pl.BlockSpec((1, tt, w), lambda bi, ti, j=j: (bi, ti, AB_RWKV0 // w + j))
    v_col = (AB_RWKV0 + 2 * w - AB_TAIL0) // w
    return pl.pallas_call(
        functools.partial(_rwkv_body, tt=tt),
        grid=(b, t // tt),
        in_specs=[col(0), col(1), pl.BlockSpec((1, tt, w), lambda bi, ti: (bi, ti, v_col)),
                  pl.BlockSpec((1, tt, lora), lambda bi, ti: (bi, ti, (AB_LORA0 - AB_TAIL0) // lora)),
                  row(wide), row(w), full(w2), row(w), full(a2), full(g2)] + [row(w)] * 5,
        out_specs=pl.BlockSpec((1, tt, w), lambda bi, ti: (bi, ti, 0)),
        out_shape=jax.ShapeDtypeStruct((b, t, w), f32),
        scratch_shapes=[pltpu.VMEM((tt + SUBLANES, wide), f32)] + [pltpu.VMEM((tt, w), f32)] * 7
                       + [pltpu.VMEM((RWKV_HEADS, RWKV_N, RWKV_N), f32)],
        compiler_params=_cparams(("parallel", "arbitrary")),
        name="rwkv",
    )(f16, f16, f32_tail, f32_tail, mu.reshape(1, wide), w0.reshape(1, w), w2, a0.reshape(1, w), a2, g2,
      k_k.reshape(1, w), k_a.reshape(1, w), r_k.reshape(1, w), ln_w.reshape(1, w), ln_b.reshape(1, w))


def _rope128(x, cos, sin, sign_lo, sign_hi):
    r_hi = pltpu.roll(x, ROPE_HALF, 1)
    r_lo = pltpu.roll(x, LANES - ROPE_HALF, 1)
    return x * cos + (r_lo * sign_lo + r_hi * sign_hi) * sin


def _rope_tables(pos, freq_row):
    ang = pos * freq_row
    m = _iota2((1, LANES), 1) % NSA_HD
    sign_lo = jnp.where(m < ROPE_HALF, -1.0, 0.0).astype(f32)
    sign_hi = jnp.where((m >= ROPE_HALF) & (m < ROPE_DIM), 1.0, 0.0).astype(f32)
    return jnp.cos(ang), jnp.sin(ang), sign_lo, sign_hi


def _nsa_prep_body(q_ref, kc_i, vc_i, ks_i, vs_i, kw_i, vw_i, pos_ref, posr_ref, freq_ref, fcol_ref,
                   qo_ref, kc_ref, vc_ref, ks_ref, vs_ref, kw_ref, vw_ref):
    cos, sin, s_lo, s_hi = _rope_tables(pos_ref[0], freq_ref[...])
    scale = NSA_HD ** -0.5 * math.log2(math.e)
    n_blk = q_ref.shape[1] // NSA_QBLOCK
    eye = (_iota2((LANES, LANES), 0) == _iota2((LANES, LANES), 1)).astype(bf16)

    def transpose_bf16(x):
        return _dot_nt(eye, x.astype(bf16))

    ang_t = fcol_ref[...] * posr_ref[0]
    cos_t, sin_t = jnp.cos(ang_t), jnp.sin(ang_t)
    for c in range(NSA_HEADS // 2):
        xt = transpose_bf16(q_ref[0, :, c * LANES:(c + 1) * LANES].astype(f32) * scale)
        for r in range(2):
            head = 2 * c + r
            g, hl = head // NSA_HPG, head % NSA_HPG
            x1 = xt[r * NSA_HD:r * NSA_HD + ROPE_HALF]
            x2 = xt[r * NSA_HD + ROPE_HALF:r * NSA_HD + ROPE_DIM]
            roped = jnp.concatenate([x1 * cos_t - x2 * sin_t, x2 * cos_t + x1 * sin_t,
                                     xt[r * NSA_HD + ROPE_DIM:(r + 1) * NSA_HD]], axis=0).astype(bf16)
            for i in range(n_blk):
                qo_ref[0, g, i, :, hl * NSA_QBLOCK:(hl + 1) * NSA_QBLOCK] = (
                    roped[:, i * NSA_QBLOCK:(i + 1) * NSA_QBLOCK])

    def split(src, ref, rope, dtype):
        x = src[0]
        if rope:
            x = _rope128(x, cos, sin, s_lo, s_hi)
        ref[0, 0] = x[:, :NSA_HD].astype(dtype)
        ref[0, 1] = x[:, NSA_HD:].astype(dtype)

    def split_t(src, ref):
        xt = transpose_bf16(src[0])
        extra = (_iota2((NSA_VT_ROWS - NSA_HD, xt.shape[1]), 0) == 0).astype(f32)
        for g in range(NSA_GROUPS):
            ref[0, g] = jnp.concatenate([xt[g * NSA_HD:(g + 1) * NSA_HD], extra], axis=0).astype(bf16)

    split(kc_i, kc_ref, False, f32)
    split(vc_i, vc_ref, False, f32)
    tt = ks_i.shape[1]
    tok = pl.program_id(1) * tt + _iota2((tt, NSA_HD), 0)
    block_onehot = (tok // SEL_BLOCK == _iota2((tt, NSA_HD), 1)).astype(f32)
    ks = _rope128(ks_i[0], cos, sin, s_lo, s_hi)
    for g in range(NSA_GROUPS):
        ks_ref[0, g] = jnp.concatenate([ks[:, g * NSA_HD:(g + 1) * NSA_HD], block_onehot], axis=1).astype(bf16)
    split_t(vs_i, vs_ref)
    split(kw_i, kw_ref, True, bf16)
    split_t(vw_i, vw_ref)


def nsa_prep(f16, f32_tail, pos3, freq_row, freq_col, tt=512):
    b, t, _ = f16.shape
    assert t // SEL_BLOCK <= NSA_HD, "the selection-block one-hot shares the key tile's second 64 lanes"
    tt = min(tt, t)
    g = NSA_GROUPS
    n_blk = tt // NSA_QBLOCK
    kv_in = [pl.BlockSpec((1, tt, LANES), lambda bi, ti, c=(NSA_KV0 - NSA_TAIL0) // LANES + i: (bi, ti, c))
             for i in range(6)]
    kv_spec = pl.BlockSpec((1, g, tt, NSA_HD), lambda bi, ti: (bi, 0, ti, 0))
    kvt_spec = pl.BlockSpec((1, g, NSA_VT_ROWS, tt), lambda bi, ti: (bi, 0, 0, ti))
    kv32 = jax.ShapeDtypeStruct((b, g, t, NSA_HD), f32)
    kv16 = jax.ShapeDtypeStruct((b, g, t, NSA_HD), bf16)
    kvt16 = jax.ShapeDtypeStruct((b, g, NSA_VT_ROWS, t), bf16)
    q_lanes = NSA_HPG * NSA_QBLOCK
    return pl.pallas_call(
        _nsa_prep_body,
        grid=(b, t // tt),
        in_specs=[pl.BlockSpec((1, tt, NSA_HEADS * NSA_HD), lambda bi, ti: (bi, ti, 0))] + kv_in
                 + [pl.BlockSpec((1, tt, 1), lambda bi, ti: (bi, ti, 0)),
                    pl.BlockSpec((1, 1, tt), lambda bi, ti: (bi, 0, ti)),
                    pl.BlockSpec((1, LANES), lambda bi, ti: (0, 0)),
                    pl.BlockSpec((ROPE_HALF, 1), lambda bi, ti: (0, 0))],
        out_specs=[pl.BlockSpec((1, g, n_blk, NSA_HD, q_lanes), lambda bi, ti: (bi, 0, ti, 0, 0)),
                   kv_spec, kv_spec, pl.BlockSpec((1, g, tt, 2 * NSA_HD), lambda bi, ti: (bi, 0, ti, 0)),
                   kvt_spec, kv_spec, kvt_spec],
        out_shape=[jax.ShapeDtypeStruct((b, g, t // NSA_QBLOCK, NSA_HD, q_lanes), bf16),
                   kv32, kv32, jax.ShapeDtypeStruct((b, g, t, 2 * NSA_HD), bf16), kvt16, kv16, kvt16],
        compiler_params=_cparams(("parallel", "parallel")),
        name="nsa_prep",
    )(f16, f32_tail, f32_tail, f32_tail, f32_tail, f32_tail, f32_tail, pos3, pos3.reshape(b, 1, t), freq_row,
      freq_col)


def _nsa_compress_body(kc_ref, vc_ref, pek_ref, w1k_ref, w2k_ref, pev_ref, w1v_ref, w2v_ref, pos_ref, freq_ref,
                       ko_ref, vo_ref):
    half = CMP_STRIDE * NSA_HD
    nrow = kc_ref.shape[2]
    last_row = _iota2((nrow, 1), 0) == nrow - 1

    def hidden(x, pe_ref, w1_ref):
        lo = _dot((x + pe_ref[:, :half]).astype(bf16), w1_ref[:half, :])
        hi = _dot((x + pe_ref[:, half:]).astype(bf16), w1_ref[half:, :])
        hi = jnp.where(last_row, 0.0, pltpu.roll(hi, nrow - 1, 0))
        hid = lo + hi
        return (hid * jax.nn.sigmoid(hid)).astype(bf16)

    ks = [_dot(hidden(kc_ref[0, g], pek_ref, w1k_ref), w2k_ref[...]) for g in range(NSA_GROUPS)]
    cos, sin, s_lo, s_hi = _rope_tables(pos_ref[0], freq_ref[...])
    kr = _rope128(jnp.concatenate(ks, axis=-1), cos, sin, s_lo, s_hi)
    for g in range(NSA_GROUPS):
        ko_ref[0, g] = kr[:, g * NSA_HD:(g + 1) * NSA_HD].astype(bf16)
        vo_ref[0, g] = _dot_nt(w2v_ref[...], hidden(vc_ref[0, g], pev_ref, w1v_ref)).astype(bf16)


def nsa_compress(kc4, vc4, pe_k, w1_k, w2_k, pe_v, w1_v, w2_vt, cpos3, freq_row):
    b, g, nrow, wide = kc4.shape
    full = lambda a: pl.BlockSpec(a.shape, lambda bi: (0,) * a.ndim)
    blk = pl.BlockSpec((1, g, nrow, wide), lambda bi: (bi, 0, 0, 0))
    out = pl.BlockSpec((1, g, nrow, NSA_HD), lambda bi: (bi, 0, 0, 0))
    out_t = pl.BlockSpec((1, g, NSA_HD, nrow), lambda bi: (bi, 0, 0, 0))
    return pl.pallas_call(
        _nsa_compress_body,
        grid=(b,),
        in_specs=[blk, blk, full(pe_k), full(w1_k), full(w2_k), full(pe_v), full(w1_v), full(w2_vt),
                  pl.BlockSpec((1, nrow, 1), lambda bi: (bi, 0, 0)), pl.BlockSpec((1, LANES), lambda bi: (0, 0))],
        out_specs=[out, out_t],
        out_shape=[jax.ShapeDtypeStruct((b, g, nrow, NSA_HD), bf16),
                   jax.ShapeDtypeStruct((b, g, NSA_HD, nrow), bf16)],
        compiler_params=_cparams(("parallel",)),
        name="nsa_compress",
    )(kc4, vc4, pe_k, w1_k, w2_k, pe_v, w1_v, w2_vt, cpos3, freq_row)


def _nsa_attn_body(qt_ref, kc_ref, vct_ref, ks_ref, vst_ref, kw_ref, vwt_ref, gate_ref, ovt_ref, dbias_ref,
                   wbias_ref, rep_ref, o_ref, score_ref, s_ref, acc_ref, sw_ref, *, key_tile):
    qb_n = NSA_QBLOCK
    blk_lanes = NSA_HPG * qb_n
    lanes = NSA_QPAIR * blk_lanes
    pair = pl.program_id(2)
    qt = jnp.concatenate([qt_ref[0, 0, a] for a in range(NSA_QPAIR)], axis=1)
    lane = _iota2((1, lanes), 1)
    t_lane = (pair * NSA_QPAIR + lane // blk_lanes) * qb_n + lane % qb_n

    def softmax_t(s):
        mx = jnp.max(s, axis=0, keepdims=True)
        mx = jnp.where(mx > -jnp.inf, mx, 0.0)
        e = jnp.exp2(s - mx)
        return e, jnp.sum(e, axis=0, keepdims=True), mx

    diag0 = pair * NSA_QPAIR
    n_tiles = (diag0 * SEL_BLOCK + key_tile - 1) // key_tile
    last_tile = ks_ref.shape[2] // key_tile - 1

    gt = jax.nn.sigmoid(gate_ref[0]).T
    gate = lambda br: jnp.concatenate([gt[3 * h + br:3 * h + br + 1, a * qb_n:(a + 1) * qb_n]
                                       for a in range(NSA_QPAIR) for h in range(NSA_HPG)], axis=1)
    gates = [gate(br) for br in range(3)]

    s_cmp = _dot(kc_ref[0, 0], qt)
    span = WINDOW + NSA_QPAIR * qb_n
    w0 = pl.multiple_of(pair * NSA_QPAIR * qb_n, LANES)
    sw_ref[...] = _dot(kw_ref[0, 0, pl.ds(w0, span), :], qt) + wbias_ref[...]
    k0 = pl.multiple_of(diag0 * SEL_BLOCK, NSA_QPAIR * SEL_BLOCK)
    s_diag = _dot(ks_ref[0, 0, pl.ds(k0, NSA_QPAIR * SEL_BLOCK), 0:NSA_HD], qt) + dbias_ref[...]

    n_cmp = kc_ref.shape[2]
    cmp_end = _iota2((n_cmp, 1), 0) * CMP_STRIDE + (CMP_LEN - 1)
    e_c, den_c, _ = softmax_t(jnp.where(cmp_end <= t_lane, s_cmp, -jnp.inf))
    p_c = e_c * (1.0 / jnp.maximum(den_c, 1e-30))
    o_c = _dot(vct_ref[0, 0], p_c.astype(bf16))

    n_sel = ovt_ref.shape[0]
    p_pair = []
    for a in range(NSA_QPAIR):
        acc = p_c[:, a * blk_lanes:a * blk_lanes + LANES]
        for c in range(1, blk_lanes // LANES):
            acc = acc + p_c[:, a * blk_lanes + c * LANES:a * blk_lanes + (c + 1) * LANES]
        p_pair.append(acc[:, :qb_n] + acc[:, qb_n:])
    imp = _dot(ovt_ref[...], jnp.concatenate(p_pair, axis=1), HI)
    blk = _iota2((n_sel, NSA_QPAIR * qb_n), 0)
    cur = pair * NSA_QPAIR + _iota2((n_sel, NSA_QPAIR * qb_n), 1) // qb_n
    valid = blk <= cur
    forced = (blk == 0) | (blk == cur) | (blk == cur - 1)
    score = jnp.where(valid, jnp.where(forced, jnp.inf, imp), -jnp.inf)
    score_ref[...] = score

    e_d, _, m_s = softmax_t(s_diag)
    acc_ref[...] = _dot(vst_ref[0, 0, :, pl.ds(k0, NSA_QPAIR * SEL_BLOCK)], e_d.astype(bf16))

    @pl.when(w0 < WINDOW)
    def _():
        is_pad = _iota2((WINDOW, 1), 0) < WINDOW - w0
        sw_ref[pl.ds(0, WINDOW), :] = jnp.where(is_pad, -jnp.inf, sw_ref[pl.ds(0, WINDOW), :])

    def rank_step(jp, cnt):
        other = score_ref[pl.ds(jp, 1), :]
        ahead = (other > score) | ((other == score) & (blk > jp))
        return cnt + ahead.astype(jnp.int32)

    def rank_step2(jj, cnt):
        return rank_step(2 * jj + 1, rank_step(2 * jj, cnt))

    n_cand = pair * NSA_QPAIR + NSA_QPAIR
    cnt = lax.fori_loop(0, jnp.where(n_cand > SEL_TOPN, n_cand // 2, 0), rank_step2,
                        jnp.zeros((n_sel, NSA_QPAIR * qb_n), jnp.int32))
    bias = jnp.where((cnt < SEL_TOPN) & (blk < diag0), 0.0, NSA_MASKED).astype(bf16)
    bias = _dot(bias, rep_ref[...]).astype(bf16)
    if n_sel < NSA_HD:
        bias = jnp.concatenate([bias, jnp.zeros((NSA_HD - n_sel, lanes), bf16)], axis=0)
    qx = jnp.concatenate([qt, bias], axis=0)

    def score_tile(kt):
        kk0 = pl.multiple_of(kt * key_tile, key_tile)
        return _dot(ks_ref[0, 0, pl.ds(kk0, key_tile), :], qx)

    s_ref[0] = score_tile(0)

    e_w, _, _ = softmax_t(sw_ref[...])
    o_w = _dot(vwt_ref[0, 0, :, pl.ds(w0, span)], e_w.astype(bf16))

    def update(kt, slot, m_old):
        sm = s_ref[slot]
        mx = jnp.maximum(m_old, jnp.max(sm, axis=0, keepdims=True))
        alpha = jnp.exp2(m_old - mx)
        p = jnp.exp2(sm - mx)
        kk0 = pl.multiple_of(kt * key_tile, key_tile)
        acc_ref[...] = alpha * acc_ref[...] + _dot(vst_ref[0, 0, :, pl.ds(kk0, key_tile)], p.astype(bf16))
        return mx

    def half_step(kt, slot, m_old):
        s_ref[1 - slot] = score_tile(jnp.minimum(kt + 1, last_tile))
        return update(kt, slot, m_old)

    def pair_step(pi, m_old):
        return half_step(2 * pi + 1, 1, half_step(2 * pi, 0, m_old))

    m_even = lax.fori_loop(0, n_tiles // 2, pair_step, m_s)

    @pl.when(n_tiles % 2 == 1)
    def _():
        update(n_tiles - 1, 0, m_even)

    l_s = acc_ref[pl.ds(NSA_HD, 1), :]
    l_w = o_w[NSA_HD:NSA_HD + 1]
    o_t = (gates[0] * o_c + (gates[1] * (1.0 / l_s)) * acc_ref[pl.ds(0, NSA_HD), :]
           + (gates[2] * (1.0 / l_w)) * o_w[:NSA_HD])
    o_b = o_t.astype(bf16)
    q_idx = _iota2((qb_n, LANES), 0)
    l_idx = _iota2((qb_n, LANES), 1)
    pick = [(l_idx == q_idx + r * qb_n).astype(bf16) for r in range(LANES // qb_n)]
    for a in range(NSA_QPAIR):
        for c in range(blk_lanes // LANES):
            piece = o_b[:, a * blk_lanes + c * LANES:a * blk_lanes + (c + 1) * LANES]
            for r in range(LANES // qb_n):
                h = c * (LANES // qb_n) + r
                o_ref[0, a * qb_n:(a + 1) * qb_n, h * NSA_HD:(h + 1) * NSA_HD] = (
                    _dot_nt(pick[r], piece).astype(bf16))


def nsa_attention(qt, kc, vct, ks, vst, kw, vwt, f32_tail, overlap_t, key_tile=256):
    b, g, n_q, d, blk_lanes = qt.shape
    t = n_q * NSA_QBLOCK
    key_tile = min(key_tile, t)
    assert NSA_QPAIR == 2, "the diagonal tile is masked by causality alone only for a pair of query blocks"
    n_cmp = kc.shape[2]
    n_sel = overlap_t.shape[0]
    lanes = NSA_QPAIR * blk_lanes
    q_rows = NSA_QPAIR * NSA_QBLOCK
    off = (jnp.arange(lanes) // blk_lanes) * NSA_QBLOCK + jnp.arange(lanes) % NSA_QBLOCK
    r_d = jnp.arange(NSA_QPAIR * SEL_BLOCK)[:, None]
    diag_bias = jnp.where(r_d <= off[None, :], 0.0, -jnp.inf).astype(f32)
    r_w = jnp.arange(WINDOW + q_rows)[:, None]
    win_bias = jnp.where((r_w > off[None, :]) & (r_w <= WINDOW + off[None, :]), 0.0, -jnp.inf).astype(f32)
    rep = (jnp.arange(q_rows)[:, None] == off[None, :]).astype(bf16)
    seq = pl.BlockSpec((1, 1, t, ks.shape[3]), lambda bi, gi, qi: (bi, gi, 0, 0))
    seq_t = pl.BlockSpec((1, 1, NSA_VT_ROWS, t), lambda bi, gi, qi: (bi, gi, 0, 0))
    const = lambda a: pl.BlockSpec(a.shape, lambda bi, gi, qi: (0, 0))
    return pl.pallas_call(
        functools.partial(_nsa_attn_body, key_tile=key_tile),
        grid=(b, g, n_q // NSA_QPAIR),
        in_specs=[pl.BlockSpec((1, 1, NSA_QPAIR, d, blk_lanes), lambda bi, gi, qi: (bi, gi, qi, 0, 0)),
                  pl.BlockSpec((1, 1, n_cmp, d), lambda bi, gi, qi: (bi, gi, 0, 0)),
                  pl.BlockSpec((1, 1, d, n_cmp), lambda bi, gi, qi: (bi, gi, 0, 0)),
                  seq, seq_t,
                  pl.BlockSpec((1, 1, t + WINDOW, d), lambda bi, gi, qi: (bi, gi, 0, 0)),
                  pl.BlockSpec((1, 1, NSA_VT_ROWS, t + WINDOW), lambda bi, gi, qi: (bi, gi, 0, 0)),
                  pl.BlockSpec((1, q_rows, LANES), lambda bi, gi, qi: (bi, qi, (NSA_GATE0 - NSA_TAIL0) // LANES + gi)),
                  const(overlap_t), const(diag_bias), const(win_bias), const(rep)],
        out_specs=pl.BlockSpec((1, q_rows, NSA_HPG * d), lambda bi, gi, qi: (bi, qi, gi)),
        out_shape=jax.ShapeDtypeStruct((b, t, g * NSA_HPG * d), bf16),
        scratch_shapes=[pltpu.VMEM((n_sel, q_rows), f32),
                        pltpu.VMEM((2, key_tile, lanes), f32), pltpu.VMEM((NSA_VT_ROWS, lanes), f32),
                        pltpu.VMEM((WINDOW + q_rows, lanes), f32)],
        compiler_params=_cparams(("parallel", "parallel", "arbitrary")),
        name="nsa_attn",
    )(qt, kc, vct, ks, vst, kw, vwt, f32_tail, overlap_t, diag_bias, win_bias, rep)


def _place(cols, total, pieces):
    out = jnp.zeros((cols, total), f32)
    for start, mat in pieces:
        out = lax.dynamic_update_slice(out, mat.astype(f32), (0, start))
    return out


def _layer0(h, b, t, g_pre, w_in, w_out, conv, a_log, dt_bias, gnorm, mu, w0, w2, a0, a2, g2, k_k, k_a,
            r_k, ln_w, ln_b):
    gdn_w = 4 * GDN_HEADS * GDN_D
    w_pad = _place(D_MODEL, AB_COLS, [
        (0, w_in[:, :gdn_w]),
        (AB_BA0, w_in[:, gdn_w:gdn_w + 2 * GDN_HEADS]),
        (AB_RWKV0, w_in[:, gdn_w + 2 * GDN_HEADS:gdn_w + 2 * GDN_HEADS + 3 * RWKV_W]),
        (AB_LORA0, w_in[:, gdn_w + 2 * GDN_HEADS + 3 * RWKV_W:]),
    ]).astype(bf16)
    f16, f32_tail = [a.reshape(b, t, -1) for a in norm_matmul(h, g_pre, w_pad)]
    arow = jnp.zeros((1, LANES), f32).at[0, GDN_HEADS:2 * GDN_HEADS].set(a_log)
    dtrow = jnp.zeros((1, LANES), f32).at[0, GDN_HEADS:2 * GDN_HEADS].set(dt_bias)
    conv3 = conv.reshape(GDN_CONV, 3, GDN_HEADS * GDN_D).transpose(1, 0, 2)
    o_a = gdn_mixer(f16, f32_tail, conv3, arow, dtrow, gnorm.reshape(1, GDN_D))
    o_b = rwkv_mixer(f16, f32_tail, mu, w0, w2, a0, a2, g2, k_k, k_a, r_k, ln_w, ln_b)
    m = b * t
    n_a = GDN_HEADS * GDN_D
    return [o_a.reshape(m, n_a), o_b.reshape(m, RWKV_W)], [w_out[:n_a].astype(bf16), w_out[n_a:].astype(bf16)]


def _layer1(h, b, t, positions, g_pre, w_in, w_out, pe_k, w1_k, w2_k, pe_v, w1_v, w2_v):
    qw = NSA_HEADS * NSA_HD
    kvw = 6 * NSA_GROUPS * NSA_HD
    gates = w_in[:, qw + kvw:].reshape(D_MODEL, NSA_GROUPS, NSA_HPG * 3)
    w_pad = _place(D_MODEL, NSA_COLS, [(0, w_in[:, :qw + kvw])]
                   + [(NSA_GATE0 + gi * LANES, gates[:, gi]) for gi in range(NSA_GROUPS)]).astype(bf16)
    f16, f32_tail = [a.reshape(b, t, -1) for a in norm_matmul(h, g_pre, w_pad)]
    inv_freq = ROPE_THETA ** (-jnp.arange(ROPE_HALF, dtype=f32) * (2.0 / ROPE_DIM))
    lane = jnp.arange(LANES)
    freq_row = jnp.where(lane % NSA_HD < ROPE_DIM, inv_freq[lane % ROPE_HALF], 0.0).reshape(1, LANES).astype(f32)
    posf = positions.astype(f32)
    qt, kc, vc, ks, vst, kw, vwt = nsa_prep(f16, f32_tail, posf.reshape(b, t, 1), freq_row,
                                            inv_freq.reshape(ROPE_HALF, 1))
    nrow = t // CMP_STRIDE
    cpos = jnp.concatenate([posf[:, CMP_LEN - 1::CMP_STRIDE], posf[:, -1:]], axis=1).reshape(b, nrow, 1)
    flat = lambda a: a.reshape(b, NSA_GROUPS, nrow, CMP_STRIDE * NSA_HD)
    kcc, vcct = nsa_compress(flat(kc), flat(vc), pe_k.reshape(1, -1), w1_k.astype(bf16), w2_k.astype(bf16),
                             pe_v.reshape(1, -1), w1_v.astype(bf16), w2_v.T.astype(bf16), cpos, freq_row)
    n_sel = t // SEL_BLOCK
    c_start = jnp.arange(nrow) * CMP_STRIDE
    s_start = jnp.arange(n_sel) * SEL_BLOCK
    overlap_t = jnp.clip(jnp.minimum(c_start[None, :] + CMP_LEN, s_start[:, None] + SEL_BLOCK)
                         - jnp.maximum(c_start[None, :], s_start[:, None]), 0, None).astype(f32) / CMP_LEN
    kw_pad = jnp.pad(kw, ((0, 0), (0, 0), (WINDOW, 0), (0, 0)))
    vwt_pad = jnp.pad(vwt, ((0, 0), (0, 0), (0, 0), (WINDOW, 0)))
    o = nsa_attention(qt, kcc, vcct, ks, vst, kw_pad, vwt_pad, f32_tail, overlap_t)
    return [o.reshape(b * t, qw)], [w_out.astype(bf16)]


def kernel(x, positions, norm_mix_pre, norm_mix_post, norm_ffn_pre, norm_ffn_post, w_ffn_up, w_ffn_down, ab_w_in,
           ab_w_out, gdn_conv, gdn_a_log, gdn_dt_bias, gdn_norm, rwkv_mu, rwkv_w0, rwkv_w2, rwkv_a0, rwkv_a2,
           rwkv_g2, rwkv_k_k, rwkv_k_a, rwkv_r_k, rwkv_ln_w, rwkv_ln_b, nsa_w_in, nsa_w_out, nsa_pe_k, nsa_w1_k,
           nsa_w2_k, nsa_pe_v, nsa_w1_v, nsa_w2_v):
    b, t, d = x.shape
    h = x.reshape(b * t, d)
    mix = _layer0(h, b, t, norm_mix_pre[0], ab_w_in[0], ab_w_out[0], gdn_conv[0], gdn_a_log[0],
                  gdn_dt_bias[0], gdn_norm[0], rwkv_mu[0], rwkv_w0[0], rwkv_w2[0], rwkv_a0[0], rwkv_a2[0],
                  rwkv_g2[0], rwkv_k_k[0], rwkv_k_a[0], rwkv_r_k[0].reshape(-1), rwkv_ln_w[0], rwkv_ln_b[0])
    h = out_ffn(*mix, h, norm_mix_post[0], norm_ffn_pre[0], w_ffn_up[0].astype(bf16), w_ffn_down[0].astype(bf16),
                norm_ffn_post[0])
    mix = _layer1(h, b, t, positions, norm_mix_pre[1], nsa_w_in[0], nsa_w_out[0], nsa_pe_k[0],
                  nsa_w1_k[0], nsa_w2_k[0], nsa_pe_v[0], nsa_w1_v[0], nsa_w2_v[0])
    h = out_ffn(*mix, h, norm_mix_post[1], norm_ffn_pre[1], w_ffn_up[1].astype(bf16), w_ffn_down[1].astype(bf16),
                norm_ffn_post[1])
    return h.reshape(b, t, d)
```
